```python
import jax, jax.numpy as jnp
from jax import lax
import numpy as np

D_MODEL = 1024
BATCH = 8
SEQ = 4096
DEPTH = 1

CHUNK = 64
Q_BLOCK = 128
HEAD_DIM = 64
FOX_HEADS = D_MODEL // 2 // HEAD_DIM
RWKV_HEADS = D_MODEL // 2 // HEAD_DIM
FOX_WIDTH = FOX_HEADS * HEAD_DIM
RWKV_WIDTH = RWKV_HEADS * HEAD_DIM
MIX_WIDTH = FOX_WIDTH + RWKV_WIDTH
DECAY_LORA = 64
AAA_LORA = 64
GATE_LORA = 160
D_FF = ((8 * D_MODEL // 3 + 255) // 256) * 256
NORM_EPS = 1e-6
LNX_EPS = 64e-5

FOX_COLS = 3 * FOX_WIDTH + FOX_HEADS
RWKV_COLS = 3 * RWKV_WIDTH + DECAY_LORA + AAA_LORA + GATE_LORA
W_IN_COLS = FOX_COLS + RWKV_COLS

kernel_name = "fox_rwkv7_hybrid_block"


def rmsnorm(x, g):
    x32 = x.astype(jnp.float32)
    y = x32 * lax.rsqrt(jnp.mean(x32 * x32, axis=-1, keepdims=True) + NORM_EPS)
    return y.astype(x.dtype) * g


def forgetting_attention(q, k, v, f_logit, f_bias):
    B, T, H, D = q.shape
    q = jnp.transpose(q, (0, 2, 1, 3))
    k = jnp.transpose(k, (0, 2, 1, 3))
    v = jnp.transpose(v, (0, 2, 1, 3))
    log_f = jax.nn.log_sigmoid(f_logit.astype(jnp.float32) + f_bias.astype(jnp.float32))
    c = jnp.cumsum(jnp.transpose(log_f, (0, 2, 1)), axis=-1)
    scale = D ** -0.5
    outs = []
    for i in range(T // Q_BLOCK):
        q0, q1 = i * Q_BLOCK, (i + 1) * Q_BLOCK
        s = jnp.einsum('bhqd,bhkd->bhqk', q[:, :, q0:q1], k[:, :, :q1]).astype(jnp.float32) * scale
        s = s + c[:, :, q0:q1, None] - c[:, :, None, :q1]
        mask = jnp.arange(q1)[None, :] <= (q0 + jnp.arange(Q_BLOCK))[:, None]
        s = jnp.where(mask, s, -jnp.inf)
        p = jax.nn.softmax(s, axis=-1).astype(v.dtype)
        outs.append(jnp.einsum('bhqk,bhkd->bhqd', p, v[:, :, :q1]))
    o = jnp.concatenate(outs, axis=2)
    return jnp.transpose(o, (0, 2, 1, 3)).reshape(B, T, H * D)


def rwkv7_scan(r, w, k, v, a, b):
    B, T, H, D = r.shape
    tm = lambda z: jnp.moveaxis(z.astype(jnp.float32), 1, 0)

    def step(S, inp):
        r_t, w_t, k_t, v_t, a_t, b_t = inp
        Sa = jnp.einsum('bhij,bhj->bhi', S, a_t)
        S = S * w_t[:, :, None, :] + Sa[..., None] * b_t[:, :, None, :] + v_t[..., None] * k_t[:, :, None, :]
        return S, jnp.einsum('bhij,bhj->bhi', S, r_t)

    S0 = jnp.zeros((B, H, D, D), jnp.float32)
    _, y = lax.scan(step, S0, (tm(r), tm(w), tm(k), tm(v), tm(a), tm(b)))
    return jnp.moveaxis(y, 0, 1).astype(r.dtype)


def rwkv7_time_mix(p, shift_mu, w0, w_up, a0, a_up, g_up, k_k, k_a, r_k, ln_w, ln_b):
    B, T, _ = p.shape
    H, D = RWKV_HEADS, HEAD_DIM
    p_prev = jnp.concatenate([jnp.zeros_like(p[:, :1]), p[:, :-1]], axis=1)
    p = p + shift_mu * (p_prev - p)
    r, k, v, w_lat, a_lat, g_lat = jnp.split(
        p, np.cumsum([RWKV_WIDTH, RWKV_WIDTH, RWKV_WIDTH, DECAY_LORA, AAA_LORA]).tolist(), axis=-1)
    w = -jax.nn.softplus(-(w0 + jnp.tanh(w_lat) @ w_up)) - 0.5
    decay = jnp.exp(-jnp.exp(w.astype(jnp.float32)))
    a = jax.nn.sigmoid(a0 + a_lat @ a_up)
    g = jax.nn.sigmoid(g_lat) @ g_up
    heads = lambda z: z.reshape(B, T, H, D)
    kk = heads(k * k_k).astype(jnp.float32)
    kk = kk / jnp.maximum(jnp.linalg.norm(kk, axis=-1, keepdims=True), 1e-12)
    k = k * (1.0 + (a - 1.0) * k_a)
    rh, kh, vh, ah = heads(r), heads(k), heads(v), heads(a)
    y = rwkv7_scan(rh, heads(decay), kh, vh, -kk, kk * ah.astype(jnp.float32))
    y32 = y.astype(jnp.float32)
    mu = jnp.mean(y32, axis=-1, keepdims=True)
    var = jnp.mean(jnp.square(y32 - mu), axis=-1, keepdims=True)
    y = ((y32 - mu) * lax.rsqrt(var + LNX_EPS)).astype(r.dtype).reshape(B, T, H * D) * ln_w + ln_b
    bonus = jnp.sum(rh * kh * r_k, axis=-1, keepdims=True) * vh
    return (y + bonus.reshape(B, T, H * D)) * g


def setup_inputs(seed: int = 0) -> dict:
    key = jax.random.key(seed)
    ks = jax.random.split(key, 24)
    L = DEPTH
    nrm = lambda k, shape, s: jax.random.normal(k, shape, jnp.float32) * s
    gain = lambda k, n: 1.0 + nrm(k, (L, n), 0.02)
    return {
        "x": jax.random.normal(ks[0], (BATCH, SEQ, D_MODEL), jnp.float32),
        "attn_norm_pre": gain(ks[1], D_MODEL),
        "attn_norm_post": gain(ks[2], D_MODEL),
        "w_in": nrm(ks[3], (L, D_MODEL, W_IN_COLS), D_MODEL ** -0.5),
        "fox_forget_bias": 2.0 + nrm(ks[4], (L, FOX_HEADS), 0.5),
        "shift_mu": jax.random.uniform(ks[5], (L, RWKV_COLS), jnp.float32),
        "rwkv_w0": -6.5 + 5.0 * jax.random.uniform(ks[6], (L, RWKV_WIDTH), jnp.float32),
        "rwkv_w_up": nrm(ks[7], (L, DECAY_LORA, RWKV_WIDTH), 0.1 * DECAY_LORA ** -0.5),
        "rwkv_a0": nrm(ks[8], (L, RWKV_WIDTH), 0.1),
        "rwkv_a_up": nrm(ks[9], (L, AAA_LORA, RWKV_WIDTH), AAA_LORA ** -0.5),
        "rwkv_g_up": nrm(ks[10], (L, GATE_LORA, RWKV_WIDTH), GATE_LORA ** -0.5),
        "rwkv_k_k": 0.85 + nrm(ks[11], (L, RWKV_WIDTH), 0.05),
        "rwkv_k_a": 1.0 + nrm(ks[12], (L, RWKV_WIDTH), 0.05),
        "rwkv_r_k": nrm(ks[13], (L, RWKV_HEADS, HEAD_DIM), 0.1),
        "rwkv_ln_w": 1.0 + nrm(ks[14], (L, RWKV_WIDTH), 0.02),
        "rwkv_ln_b": nrm(ks[15], (L, RWKV_WIDTH), 0.02),
        "w_out": nrm(ks[16], (L, MIX_WIDTH, D_MODEL), MIX_WIDTH ** -0.5),
        "ffn_norm_pre": gain(ks[17], D_MODEL),
        "ffn_norm_post": gain(ks[18], D_MODEL),
        "ffn_w_gate": nrm(ks[19], (L, D_MODEL, D_FF), D_MODEL ** -0.5),
        "ffn_w_up": nrm(ks[20], (L, D_MODEL, D_FF), D_MODEL ** -0.5),
        "ffn_w_down": nrm(ks[21], (L, D_FF, D_MODEL), D_FF ** -0.5),
    }


def reference(x, attn_norm_pre, attn_norm_post, w_in, fox_forget_bias, shift_mu,
              rwkv_w0, rwkv_w_up, rwkv_a0, rwkv_a_up, rwkv_g_up, rwkv_k_k, rwkv_k_a,
              rwkv_r_k, rwkv_ln_w, rwkv_ln_b, w_out, ffn_norm_pre, ffn_norm_post,
              ffn_w_gate, ffn_w_up, ffn_w_down):
    B, T, _ = x.shape
    h = x
    for l in range(DEPTH):
        u = rmsnorm(h, attn_norm_pre[l])
        proj = u @ w_in[l]
        p_fox, p_rwkv = proj[..., :FOX_COLS], proj[..., FOX_COLS:]
        fq, fk, fv, ff = jnp.split(p_fox, [FOX_WIDTH, 2 * FOX_WIDTH, 3 * FOX_WIDTH], axis=-1)
        fh = lambda z: z.reshape(B, T, FOX_HEADS, HEAD_DIM)
        o_fox = forgetting_attention(fh(fq), fh(fk), fh(fv), ff, fox_forget_bias[l])
        o_rwkv = rwkv7_time_mix(p_rwkv, shift_mu[l], rwkv_w0[l], rwkv_w_up[l], rwkv_a0[l],
                                rwkv_a_up[l], rwkv_g_up[l], rwkv_k_k[l], rwkv_k_a[l],
                                rwkv_r_k[l], rwkv_ln_w[l], rwkv_ln_b[l])
        mix = jnp.concatenate([o_fox, o_rwkv], axis=-1) @ w_out[l]
        h = h + rmsnorm(mix, attn_norm_post[l])
        z = rmsnorm(h, ffn_norm_pre[l])
        f = (jax.nn.silu(z @ ffn_w_gate[l]) * (z @ ffn_w_up[l])) @ ffn_w_down[l]
        h = h + rmsnorm(f, ffn_norm_post[l])
    return h
```

```python
import functools

import jax
import jax.numpy as jnp
from jax import lax
from jax.experimental import pallas as pl
from jax.experimental.pallas import tpu as pltpu

F32 = jnp.float32
BF16 = jnp.bfloat16

HEAD_DIM = 64
LANES = 128
NORM_EPS = 1e-6
LNX_EPS = 64e-5
DECAY_LORA = 64
AAA_LORA = 64
GATE_LORA = 160
CHUNK = 64
GROUP_HEADS = 4
GROUP_W = GROUP_HEADS * HEAD_DIM
VMEM_LIMIT = 56 * 1024 * 1024


def _dot(a, b):
    return jnp.dot(a, b, preferred_element_type=F32)


def _dot_nt(a, b):
    return lax.dot_general(a, b, (((1,), (1,)), ((), ())), preferred_element_type=F32)


def _dot_tn(a, b):
    return lax.dot_general(a, b, (((0,), (0,)), ((), ())), preferred_element_type=F32)


def _split2(x):
    hi = x.astype(BF16)
    lo = (x - hi.astype(F32)).astype(BF16)
    return hi, lo


def _split3(x):
    hi = x.astype(BF16)
    r = x - hi.astype(F32)
    mid = r.astype(BF16)
    lo = (r - mid.astype(F32)).astype(BF16)
    return hi, mid, lo


def _dot_x3(a, b):
    ah, al = _split2(a)
    bh, bl = _split2(b)
    return _dot(ah, bh) + (_dot(ah, bl) + _dot(al, bh))


def _dot_exact_rhs01(x, ones01):
    hi, mid, lo = _split3(x)
    return _dot(hi, ones01) + (_dot(mid, ones01) + _dot(lo, ones01))


def _dot_exact_lhs01(ones01, x):
    hi, mid, lo = _split3(x)
    return _dot(ones01, hi) + (_dot(ones01, mid) + _dot(ones01, lo))


def _rms(x):
    return x * lax.rsqrt(jnp.mean(x * x, axis=-1, keepdims=True) + NORM_EPS)


def _softplus(x):
    return jnp.maximum(x, 0.0) + jnp.log1p(jnp.exp(-jnp.abs(x)))


def _sigmoid(x):
    return 1.0 / (1.0 + jnp.exp(-x))


def _const_spec(shape):
    nd = len(shape)
    return pl.BlockSpec(shape, lambda *_: (0,) * nd)


def _inproj_kernel(x_ref, g_ref, w_ref, fb_ref, tri_ref, sel_ref, qc_ref,
                   q_out, k_out, v_out, rw_out, carry_ref, *, n_heads, rw_w):
    t = pl.program_id(1)

    @pl.when(t == 0)
    def _():
        carry_ref[...] = jnp.zeros_like(carry_ref)

    hw = n_heads * LANES
    u = (_rms(x_ref[0]) * g_ref[...]).astype(BF16)

    q = _dot(u, w_ref[:, 0:hw]) * (HEAD_DIM ** -0.5) + qc_ref[...]
    for h in range(n_heads):
        q_out[0, h] = q[:, h * LANES:(h + 1) * LANES].astype(BF16)

    f0 = 2 * hw + n_heads * HEAD_DIM
    ff = _dot(u, w_ref[:, f0:f0 + LANES]) + fb_ref[...]
    logf = -_softplus(-ff)
    c = _dot_exact_lhs01(tri_ref[...], logf) + carry_ref[...]
    tm = c.shape[0]
    carry_ref[...] = c[tm - 1:tm, :]

    nhi, nmid, nlo = _split3(-c)
    kadd = _dot(nhi, sel_ref[0]) + (_dot(nmid, sel_ref[1]) + _dot(nlo, sel_ref[2]))
    k = _dot(u, w_ref[:, hw:2 * hw]) + kadd
    for h in range(n_heads):
        k_out[0, h] = k[:, h * LANES:(h + 1) * LANES].astype(BF16)

    v_out[0] = _dot(u, w_ref[:, 2 * hw:f0]).astype(BF16)
    r0 = f0 + LANES
    rw_out[0] = _dot(u, w_ref[:, r0:r0 + rw_w])


def _inproj(x, g_pre, wp, fbias, tri, sel, qconst, *, n_heads, rw_w, tm):
    B, T, D = x.shape
    fw = n_heads * HEAD_DIM
    kern = functools.partial(_inproj_kernel, n_heads=n_heads, rw_w=rw_w)
    return pl.pallas_call(
        kern,
        grid=(B, T // tm),
        in_specs=[
            pl.BlockSpec((1, tm, D), lambda b, t: (b, t, 0)),
            _const_spec(g_pre.shape),
            _const_spec(wp.shape),
            _const_spec(fbias.shape),
            _const_spec(tri.shape),
            _const_spec(sel.shape),
            _const_spec(qconst.shape),
        ],
        out_specs=[
            pl.BlockSpec((1, n_heads, tm, LANES), lambda b, t: (b, 0, t, 0)),
            pl.BlockSpec((1, n_heads, tm, LANES), lambda b, t: (b, 0, t, 0)),
            pl.BlockSpec((1, tm, fw), lambda b, t: (b, t, 0)),
            pl.BlockSpec((1, tm, rw_w), lambda b, t: (b, t, 0)),
        ],
        out_shape=[
            jax.ShapeDtypeStruct((B, n_heads, T, LANES), BF16),
            jax.ShapeDtypeStruct((B, n_heads, T, LANES), BF16),
            jax.ShapeDtypeStruct((B, T, fw), BF16),
            jax.ShapeDtypeStruct((B, T, rw_w), F32),
        ],
        scratch_shapes=[pltpu.VMEM((1, LANES), F32)],
        compiler_params=pltpu.CompilerParams(
            dimension_semantics=("arbitrary", "arbitrary"), vmem_limit_bytes=VMEM_LIMIT),
        name="inproj",
    )(x, g_pre, wp, fbias, tri, sel, qconst)


def _fox_kernel(q_ref, k_ref, v_ref, o_ref, m_scr, l_scr, acc_scr, *, tq):
    qi = pl.program_id(2)
    reps = tq // LANES

    def step(h, j, masked):
        q = q_ref[0, h]
        kb = k_ref[0, h, pl.ds(pl.multiple_of(j * tq, tq), tq), :]
        vb = v_ref[0, pl.ds(pl.multiple_of(j * tq, tq), tq), :]
        s = _dot_nt(q, kb)
        if masked:
            row = lax.broadcasted_iota(jnp.int32, (tq, tq), 0)
            col = lax.broadcasted_iota(jnp.int32, (tq, tq), 1)
            s = jnp.where(col <= row, s, -jnp.inf)
        m_prev = m_scr[h]
        m_next = jnp.maximum(m_prev, jnp.max(s, axis=1, keepdims=True))
        p = jnp.exp(s - pltpu.repeat(m_next, reps, axis=1))
        alpha = jnp.exp(m_prev - m_next)
        l_scr[h] = alpha * l_scr[h] + jnp.sum(p, axis=1, keepdims=True)
        acc_scr[h] = alpha * acc_scr[h] + _dot(p.astype(BF16), vb)
        m_scr[h] = m_next

    for h in range(2):
        m_scr[h] = jnp.full(m_scr.shape[1:], -jnp.inf, F32)
        l_scr[h] = jnp.zeros(l_scr.shape[1:], F32)
        acc_scr[h] = jnp.zeros(acc_scr.shape[1:], F32)

        def body(j, carry, h=h):
            step(h, j, masked=False)
            return carry

        lax.fori_loop(0, qi, body, 0)
        step(h, qi, masked=True)

    lane = lax.broadcasted_iota(jnp.int32, acc_scr.shape[1:], 1)
    o = jnp.where(lane < HEAD_DIM, acc_scr[0] / l_scr[0], acc_scr[1] / l_scr[1])
    o_ref[0] = o.astype(o_ref.dtype)


def _fox(q_aug, k_aug, v, *, tq):
    B, H, T, _ = q_aug.shape
    kern = functools.partial(_fox_kernel, tq=tq)
    return pl.pallas_call(
        kern,
        grid=(B, H // 2, T // tq),
        in_specs=[
            pl.BlockSpec((1, 2, tq, LANES), lambda b, hp, qi: (b, hp, qi, 0)),
            pl.BlockSpec((1, 2, T, LANES), lambda b, hp, qi: (b, hp, 0, 0)),
            pl.BlockSpec((1, T, LANES), lambda b, hp, qi: (b, 0, hp)),
        ],
        out_specs=pl.BlockSpec((1, tq, LANES), lambda b, hp, qi: (b, qi, hp)),
        out_shape=jax.ShapeDtypeStruct(v.shape, BF16),
        scratch_shapes=[pltpu.VMEM((2, tq, LANES), F32)] * 3,
        compiler_params=pltpu.CompilerParams(
            dimension_semantics=("arbitrary", "arbitrary", "arbitrary"),
            vmem_limit_bytes=VMEM_LIMIT),
        name="fox",
    )(q_aug, k_aug, v)


def _unit_lower_inverse(L, mm):
    n = L.shape[0]
    r = lax.broadcasted_iota(jnp.int32, (n, n), 0)
    c = lax.broadcasted_iota(jnp.int32, (n, n), 1)
    same16 = (r >> 4) == (c >> 4)
    same32 = (r >> 5) == (c >> 5)
    d = jnp.where(same16, L, 0.0)
    p = jnp.where(r == c, 1.0, 0.0) + d
    m = d
    for _ in range(3):
        m = mm(m, m)
        p = p + mm(m, p)
    o1 = jnp.where(same32 & jnp.logical_not(same16), L, 0.0)
    p = p + mm(mm(p, o1), p)
    o2 = jnp.where(jnp.logical_not(same32), L, 0.0)
    return p + mm(mm(p, o2), p)


def _rwkv_kernel(rw_ref, rwprev_ref, mu_ref, w0_ref, a0_ref, kk_ref, ka_ref, rk_ref,
                 lnw_ref, lnb_ref, wup_ref, aup_ref, gup_ref, ones_ref, tri_ref,
                 o_ref, s_scr, a_scr, b_scr, r_scr, k_scr, v_scr, gc_scr, y_scr, *, width):
    t = pl.program_id(1)
    tc = rw_ref.shape[1]
    n_chunks = tc // CHUNK
    n_groups = width // GROUP_W

    @pl.when(t == 0)
    def _():
        s_scr[...] = jnp.zeros_like(s_scr)

    ones_blk = ones_ref[...]

    def segsum(x):
        hi, lo = _split2(x)
        return _dot(hi, ones_blk) + _dot(lo, ones_blk)

    p = rw_ref[0]
    prev_row = jnp.where(t > 0, rwprev_ref[0, 7:8, :], 0.0)
    row0 = lax.broadcasted_iota(jnp.int32, (tc, 1), 0) == 0
    p_prev = jnp.where(row0, prev_row, pltpu.roll(p, 1, 0))
    p = p + mu_ref[...] * (p_prev - p)
    r = p[:, 0:width]
    k = p[:, width:2 * width]
    v = p[:, 2 * width:3 * width]
    o = 3 * width
    w_lat = p[:, o:o + LANES]
    a_lat = p[:, o + LANES:o + 2 * LANES]
    g_lat = p[:, o + 2 * LANES:o + 4 * LANES]

    w_log = -_softplus(-(w0_ref[...] + _dot(jnp.tanh(w_lat).astype(BF16), wup_ref[...]))) - 0.5
    g_log = -jnp.exp(w_log)
    lr = _sigmoid(a0_ref[...] + _dot(a_lat.astype(BF16), aup_ref[...]))
    gate = _dot(_sigmoid(g_lat).astype(BF16), gup_ref[...])
    kk = k * kk_ref[...]
    kk = kk / jnp.maximum(jnp.sqrt(segsum(kk * kk)), 1e-12)
    k = k * (1.0 + (lr - 1.0) * ka_ref[...])
    bonus = segsum(r * k * rk_ref[...]) * v

    G = _dot_exact_lhs01(tri_ref[...], g_log)
    e_pos = jnp.exp(G)
    e_neg = jnp.exp(-G)
    a_scr[...] = (-kk * jnp.exp(G - g_log)).astype(BF16)
    b_scr[...] = (kk * lr * e_neg).astype(BF16)
    r_scr[...] = (r * e_pos).astype(BF16)
    k_scr[...] = (k * e_neg).astype(BF16)
    v_scr[...] = v.astype(BF16)
    for ci in range(n_chunks):
        gc_scr[ci:ci + 1, :] = e_pos[(ci + 1) * CHUNK - 1:(ci + 1) * CHUNK, :]

    lane_head = lax.broadcasted_iota(jnp.int32, (CHUNK, GROUP_W), 1) // HEAD_DIM
    head_masks = [lane_head == h for h in range(GROUP_HEADS)]
    rt = lax.broadcasted_iota(jnp.int32, (GROUP_W, GROUP_W), 0) % CHUNK
    ct = lax.broadcasted_iota(jnp.int32, (GROUP_W, GROUP_W), 1) % CHUNK
    strict = rt > ct
    incl = rt >= ct

    def stack(x):
        zero = jnp.zeros_like(x)
        return jnp.concatenate([jnp.where(mk, x, zero) for mk in head_masks], axis=0)

    def chunk_body(ci, carry):
        rows = pl.ds(pl.multiple_of(ci * CHUNK, CHUNK), CHUNK)
        for g in range(n_groups):
            cols = slice(g * GROUP_W, (g + 1) * GROUP_W)
            a_s = stack(a_scr[rows, cols])
            b_s = stack(b_scr[rows, cols])
            r_s = stack(r_scr[rows, cols])
            k_s = stack(k_scr[rows, cols])
            v_s = stack(v_scr[rows, cols])
            l_all = _dot_nt(jnp.concatenate([a_s, r_s], axis=0),
                            jnp.concatenate([b_s, k_s], axis=0))
            l_ab = jnp.where(strict, l_all[0:GROUP_W, 0:GROUP_W], 0.0)
            l_ak = jnp.where(strict, l_all[0:GROUP_W, GROUP_W:], 0.0)
            l_rb = jnp.where(incl, l_all[GROUP_W:, 0:GROUP_W], 0.0)
            l_rk = jnp.where(incl, l_all[GROUP_W:, GROUP_W:], 0.0)
            t_inv = _unit_lower_inverse(l_ab, _dot_x3).astype(BF16)
            akv = _dot(l_ak.astype(BF16), v_s).astype(BF16)
            wu = _dot(t_inv, jnp.concatenate([a_s, akv], axis=1))
            s_old = s_scr[g]
            s_b = s_old.astype(BF16)
            z_s = _dot_nt(wu[:, 0:GROUP_W].astype(BF16), s_b) + wu[:, GROUP_W:]
            z_b = z_s.astype(BF16)
            y_s = (_dot_nt(r_s, s_b) + _dot(l_rb.astype(BF16), z_b)) + _dot(l_rk.astype(BF16), v_s)
            y = (y_s[0:CHUNK] + y_s[CHUNK:2 * CHUNK]) + (y_s[2 * CHUNK:3 * CHUNK] + y_s[3 * CHUNK:])
            y_scr[rows, cols] = y
            gc = gc_scr[pl.ds(ci, 1), cols]
            bh_s = (b_s.astype(F32) * gc).astype(BF16)
            kh_s = (k_s.astype(F32) * gc).astype(BF16)
            s_scr[g] = s_old * gc + _dot_tn(jnp.concatenate([z_b, v_s], axis=0),
                                            jnp.concatenate([bh_s, kh_s], axis=0))
        return carry

    lax.fori_loop(0, n_chunks, chunk_body, 0)

    y = y_scr[...]
    d = y - segsum(y) * (1.0 / HEAD_DIM)
    var = segsum(d * d) * (1.0 / HEAD_DIM)
    yn = d * lax.rsqrt(var + LNX_EPS) * lnw_ref[...] + lnb_ref[...]
    o_ref[0] = ((yn + bonus) * gate).astype(o_ref.dtype)


def _rwkv(rw, mu, w0, a0, k_k, k_a, r_k, ln_w, ln_b, w_up, a_up, g_up, ones_blk, tri_blk, *, width, tc):
    B, T, rw_w = rw.shape
    n_groups = width // GROUP_W
    kern = functools.partial(_rwkv_kernel, width=width)
    sub = 8
    small = [mu, w0, a0, k_k, k_a, r_k, ln_w, ln_b, w_up, a_up, g_up, ones_blk, tri_blk]
    return pl.pallas_call(
        kern,
        grid=(B, T // tc),
        in_specs=[
            pl.BlockSpec((1, tc, rw_w), lambda b, t: (b, t, 0)),
            pl.BlockSpec((1, sub, rw_w), lambda b, t: (b, jnp.maximum(t * (tc // sub) - 1, 0), 0)),
        ] + [_const_spec(a.shape) for a in small],
        out_specs=pl.BlockSpec((1, tc, width), lambda b, t: (b, t, 0)),
        out_shape=jax.ShapeDtypeStruct((B, T, width), BF16),
        scratch_shapes=[
            pltpu.VMEM((n_groups, GROUP_W, GROUP_W), F32),
            pltpu.VMEM((tc, width), BF16),
            pltpu.VMEM((tc, width), BF16),
            pltpu.VMEM((tc, width), BF16),
            pltpu.VMEM((tc, width), BF16),
            pltpu.VMEM((tc, width), BF16),
            pltpu.VMEM((tc // CHUNK, width), F32),
            pltpu.VMEM((tc, width), F32),
        ],
        compiler_params=pltpu.CompilerParams(
            dimension_semantics=("arbitrary", "arbitrary"), vmem_limit_bytes=VMEM_LIMIT),
        name="rwkv",
    )(rw, rw, *small)


def _outffn_kernel(x_ref, of_ref, or_ref, wof_ref, wor_ref, gpost_ref, gfpre_ref, gfpost_ref,
                   wg_ref, wu_ref, wd_ref, o_ref):
    mix = _dot(of_ref[...], wof_ref[...]) + _dot(or_ref[...], wor_ref[...])
    h = x_ref[...] + _rms(mix) * gpost_ref[...]
    z = (_rms(h) * gfpre_ref[...]).astype(BF16)
    gate = _dot(z, wg_ref[...])
    up = _dot(z, wu_ref[...])
    act = (gate * _sigmoid(gate) * up).astype(BF16)
    f = _dot(act, wd_ref[...])
    o_ref[...] = h + _rms(f) * gfpost_ref[...]


def _outffn(x2, o_fox, o_rw, wof, wor, g_post, g_fpre, g_fpost, wg, wu, wd, *, tm):
    N, D = x2.shape
    consts = [wof, wor, g_post, g_fpre, g_fpost, wg, wu, wd]
    return pl.pallas_call(
        _outffn_kernel,
        grid=(N // tm,),
        in_specs=[
            pl.BlockSpec((tm, D), lambda i: (i, 0)),
            pl.BlockSpec((tm, o_fox.shape[1]), lambda i: (i, 0)),
            pl.BlockSpec((tm, o_rw.shape[1]), lambda i: (i, 0)),
        ] + [_const_spec(a.shape) for a in consts],
        out_specs=pl.BlockSpec((tm, D), lambda i: (i, 0)),
        out_shape=jax.ShapeDtypeStruct((N, D), F32),
        compiler_params=pltpu.CompilerParams(
            dimension_semantics=("arbitrary",), vmem_limit_bytes=VMEM_LIMIT),
        name="outffn",
    )(x2, o_fox, o_rw, *consts)


def _pad_cols(a, n):
    return jnp.pad(a, ((0, 0), (0, n - a.shape[1])))


def _pad_rows(a, n):
    return jnp.pad(a, ((0, n - a.shape[0]), (0, 0)))


def _block_layer(h, attn_norm_pre, attn_norm_post, w_in, fox_forget_bias, shift_mu, rwkv_w0,
                 rwkv_w_up, rwkv_a0, rwkv_a_up, rwkv_g_up, rwkv_k_k, rwkv_k_a, rwkv_r_k,
                 rwkv_ln_w, rwkv_ln_b, w_out, ffn_norm_pre, ffn_norm_post, ffn_w_gate, ffn_w_up,
                 ffn_w_down):
    B, T, D = h.shape
    fw = D // 2
    rwid = D // 2
    nh = fw // HEAD_DIM
    hw = nh * LANES

    fq, fk, fv = w_in[:, 0:fw], w_in[:, fw:2 * fw], w_in[:, 2 * fw:3 * fw]
    ffw = w_in[:, 3 * fw:3 * fw + nh]
    rcol = 3 * fw + nh

    def head_pad(w):
        return jnp.pad(w.reshape(D, nh, HEAD_DIM), ((0, 0), (0, 0), (0, LANES - HEAD_DIM))).reshape(D, hw)

    def rw_layout(a):
        o = 3 * rwid
        return jnp.concatenate([
            a[:, 0:o],
            _pad_cols(a[:, o:o + DECAY_LORA], LANES),
            _pad_cols(a[:, o + DECAY_LORA:o + DECAY_LORA + AAA_LORA], LANES),
            _pad_cols(a[:, o + DECAY_LORA + AAA_LORA:], 2 * LANES)], axis=1)

    w_rw = rw_layout(w_in[:, rcol:])
    rw_w = w_rw.shape[1]
    wp = jnp.concatenate([head_pad(fq), head_pad(fk), fv, _pad_cols(ffw, LANES), w_rw], axis=1).astype(BF16)
    mu = rw_layout(shift_mu[None, :])
    fbias = _pad_cols(fox_forget_bias[None, :], LANES)

    tm1 = 512
    ii = jnp.arange(tm1)
    tri = (ii[:, None] >= ii[None, :]).astype(BF16)
    lane = jnp.arange(LANES)[:, None]
    colq = jnp.arange(hw)[None, :]
    sel = jnp.stack([(colq == lane * LANES + HEAD_DIM + 3 + p) & (lane < nh) for p in range(3)]).astype(BF16)
    cl = jnp.arange(hw) % LANES
    qconst = ((cl >= HEAD_DIM + 3) & (cl < HEAD_DIM + 6)).astype(F32)[None, :]

    q_aug, k_aug, v, rw = _inproj(h, attn_norm_pre[None, :], wp, fbias, tri, sel, qconst,
                                  n_heads=nh, rw_w=rw_w, tm=tm1)

    o_fox = _fox(q_aug, k_aug, v, tq=512)

    tc = 256
    hh = jnp.arange(rwid) // HEAD_DIM
    ones_blk = (hh[:, None] == hh[None, :]).astype(BF16)
    jj = jnp.arange(tc)
    tri_blk = ((jj[:, None] >= jj[None, :]) & (jj[:, None] // CHUNK == jj[None, :] // CHUNK)).astype(BF16)
    row = lambda a: a.reshape(1, -1)
    o_rw = _rwkv(rw, mu, row(rwkv_w0), row(rwkv_a0), row(rwkv_k_k), row(rwkv_k_a), row(rwkv_r_k),
                 row(rwkv_ln_w), row(rwkv_ln_b),
                 _pad_rows(rwkv_w_up, LANES).astype(BF16), _pad_rows(rwkv_a_up, LANES).astype(BF16),
                 _pad_rows(rwkv_g_up, 2 * LANES).astype(BF16), ones_blk, tri_blk, width=rwid, tc=tc)

    out = _outffn(h.reshape(B * T, D), o_fox.reshape(B * T, fw), o_rw.reshape(B * T, rwid),
                  w_out[0:fw].astype(BF16), w_out[fw:].astype(BF16),
                  row(attn_norm_post), row(ffn_norm_pre), row(ffn_norm_post),
                  ffn_w_gate.astype(BF16), ffn_w_up.astype(BF16), ffn_w_down.astype(BF16), tm=256)
    return out.reshape(B, T, D)


def kernel(x, attn_norm_pre, attn_norm_post, w_in, fox_forget_bias, shift_mu, rwkv_w0, rwkv_w_up,
           rwkv_a0, rwkv_a_up, rwkv_g_up, rwkv_k_k, rwkv_k_a, rwkv_r_k, rwkv_ln_w, rwkv_ln_b, w_out,
           ffn_norm_pre, ffn_norm_post, ffn_w_gate, ffn_w_up, ffn_w_down):
    h = x
    for l in range(attn_norm_pre.shape[0]):
        h = _block_layer(h, attn_norm_pre[l], attn_norm_post[l], w_in[l], fox_forget_bias[l],
                         shift_mu[l], rwkv_w0[l], rwkv_w_up[l], rwkv_a0[l], rwkv_a_up[l],
                         rwkv_g_up[l], rwkv_k_k[l], rwkv_k_a[l], rwkv_r_k[l], rwkv_ln_w[l],
                         rwkv_ln_b[l], w_out[l], ffn_norm_pre[l], ffn_norm_post[l], ffn_w_gate[l],
                         ffn_w_up[l], ffn_w_down[l])
    return h
```

```python
import functools

import jax
import jax.numpy as jnp
from jax import lax
from jax.experimental import pallas as pl
from jax.experimental.pallas import tpu as pltpu

F32 = jnp.float32
BF16 = jnp.bfloat16

HEAD_DIM = 64
LANES = 128
NORM_EPS = 1e-6
LNX_EPS = 64e-5
DECAY_LORA = 64
AAA_LORA = 64
GATE_LORA = 160
CHUNK = 64
GROUP_HEADS = 4
GROUP_W = GROUP_HEADS * HEAD_DIM
VMEM_LIMIT = 56 * 1024 * 1024


def _dot(a, b):
    return jnp.dot(a, b, preferred_element_type=F32)


def _dot_nt(a, b):
    return lax.dot_general(a, b, (((1,), (1,)), ((), ())), preferred_element_type=F32)


def _dot_tn(a, b):
    return lax.dot_general(a, b, (((0,), (0,)), ((), ())), preferred_element_type=F32)


def _split2(x):
    hi = x.astype(BF16)
    lo = (x - hi.astype(F32)).astype(BF16)
    return hi, lo


def _split3(x):
    hi = x.astype(BF16)
    r = x - hi.astype(F32)
    mid = r.astype(BF16)
    lo = (r - mid.astype(F32)).astype(BF16)
    return hi, mid, lo


def _dot_x3(a, b):
    ah, al = _split2(a)
    bh, bl = _split2(b)
    return _dot(ah, bh) + (_dot(ah, bl) + _dot(al, bh))


def _dot_exact_rhs01(x, ones01):
    hi, mid, lo = _split3(x)
    return _dot(hi, ones01) + (_dot(mid, ones01) + _dot(lo, ones01))


def _dot_exact_lhs01(ones01, x):
    hi, mid, lo = _split3(x)
    return _dot(ones01, hi) + (_dot(ones01, mid) + _dot(ones01, lo))


def _rms(x):
    return x * lax.rsqrt(jnp.mean(x * x, axis=-1, keepdims=True) + NORM_EPS)


def _softplus(x):
    return jnp.maximum(x, 0.0) + jnp.log1p(jnp.exp(-jnp.abs(x)))


def _sigmoid(x):
    return 1.0 / (1.0 + jnp.exp(-x))


def _const_spec(shape):
    nd = len(shape)
    return pl.BlockSpec(shape, lambda *_: (0,) * nd)


def _inproj_kernel(x_ref, g_ref, w_ref, fb_ref, tri_ref, sel_ref, qc_ref,
                   q_out, k_out, v_out, rw_out, carry_ref, *, n_heads, rw_w):
    t = pl.program_id(1)

    @pl.when(t == 0)
    def _():
        carry_ref[...] = jnp.zeros_like(carry_ref)

    hw = n_heads * LANES
    u = (_rms(x_ref[0]) * g_ref[...]).astype(BF16)

    q = _dot(u, w_ref[:, 0:hw]) * (HEAD_DIM ** -0.5) + qc_ref[...]
    for h in range(n_heads):
        q_out[0, h] = q[:, h * LANES:(h + 1) * LANES].astype(BF16)

    f0 = 2 * hw + n_heads * HEAD_DIM
    ff = _dot(u, w_ref[:, f0:f0 + LANES]) + fb_ref[...]
    logf = -_softplus(-ff)
    c = _dot_exact_lhs01(tri_ref[...], logf) + carry_ref[...]
    tm = c.shape[0]
    carry_ref[...] = c[tm - 1:tm, :]

    nhi, nmid, nlo = _split3(-c)
    kadd = _dot(nhi, sel_ref[0]) + (_dot(nmid, sel_ref[1]) + _dot(nlo, sel_ref[2]))
    k = _dot(u, w_ref[:, hw:2 * hw]) + kadd
    for h in range(n_heads):
        k_out[0, h] = k[:, h * LANES:(h + 1) * LANES].astype(BF16)

    v_out[0] = _dot(u, w_ref[:, 2 * hw:f0]).astype(BF16)
    r0 = f0 + LANES
    rw_out[0] = _dot(u, w_ref[:, r0:r0 + rw_w])


def _inproj(x, g_pre, wp, fbias, tri, sel, qconst, *, n_heads, rw_w, tm):
    B, T, D = x.shape
    fw = n_heads * HEAD_DIM
    kern = functools.partial(_inproj_kernel, n_heads=n_heads, rw_w=rw_w)
    return pl.pallas_call(
        kern,
        grid=(B, T // tm),
        in_specs=[
            pl.BlockSpec((1, tm, D), lambda b, t: (b, t, 0)),
            _const_spec(g_pre.shape),
            _const_spec(wp.shape),
            _const_spec(fbias.shape),
            _const_spec(tri.shape),
            _const_spec(sel.shape),
            _const_spec(qconst.shape),
        ],
        out_specs=[
            pl.BlockSpec((1, n_heads, tm, LANES), lambda b, t: (b, 0, t, 0)),
            pl.BlockSpec((1, n_heads, tm, LANES), lambda b, t: (b, 0, t, 0)),
            pl.BlockSpec((1, tm, fw), lambda b, t: (b, t, 0)),
            pl.BlockSpec((1, tm, rw_w), lambda b, t: (b, t, 0)),
        ],
        out_shape=[
            jax.ShapeDtypeStruct((B, n_heads, T, LANES), BF16),
            jax.ShapeDtypeStruct((B, n_heads, T, LANES), BF16),
            jax.ShapeDtypeStruct((B, T, fw), BF16),
            jax.ShapeDtypeStruct((B, T, rw_w), F32),
        ],
        scratch_shapes=[pltpu.VMEM((1, LANES), F32)],
        compiler_params=pltpu.CompilerParams(
            dimension_semantics=("arbitrary", "arbitrary"), vmem_limit_bytes=VMEM_LIMIT),
        name="inproj",
    )(x, g_pre, wp, fbias, tri, sel, qconst)


def _fox_kernel(q_ref, k_ref, v_ref, o_ref, m_scr, l_scr, acc_scr, *, tq):
    qi = pl.program_id(2)
    reps = tq // LANES

    def step(h, j, masked):
        q = q_ref[0, h]
        kb = k_ref[0, h, pl.ds(pl.multiple_of(j * tq, tq), tq), :]
        vb = v_ref[0, pl.ds(pl.multiple_of(j * tq, tq), tq), :]
        s = _dot_nt(q, kb)
        if masked:
            row = lax.broadcasted_iota(jnp.int32, (tq, tq), 0)
            col = lax.broadcasted_iota(jnp.int32, (tq, tq), 1)
            s = jnp.where(col <= row, s, -jnp.inf)
        m_prev = m_scr[h]
        m_next = jnp.maximum(m_prev, jnp.max(s, axis=1, keepdims=True))
        p = jnp.exp(s - pltpu.repeat(m_next, reps, axis=1))
        alpha = jnp.exp(m_prev - m_next)
        l_scr[h] = alpha * l_scr[h] + jnp.sum(p, axis=1, keepdims=True)
        acc_scr[h] = alpha * acc_scr[h] + _dot(p.astype(BF16), vb)
        m_scr[h] = m_next

    for h in range(2):
        m_scr[h] = jnp.full(m_scr.shape[1:], -jnp.inf, F32)
        l_scr[h] = jnp.zeros(l_scr.shape[1:], F32)
        acc_scr[h] = jnp.zeros(acc_scr.shape[1:], F32)

        def body(j, carry, h=h):
            step(h, j, masked=False)
            return carry

        lax.fori_loop(0, qi, body, 0)
        step(h, qi, masked=True)

    lane = lax.broadcasted_iota(jnp.int32, acc_scr.shape[1:], 1)
    o = jnp.where(lane < HEAD_DIM, acc_scr[0] / l_scr[0], acc_scr[1] / l_scr[1])
    o_ref[0] = o.astype(o_ref.dtype)


def _fox(q_aug, k_aug, v, *, tq):
    B, H, T, _ = q_aug.shape
    kern = functools.partial(_fox_kernel, tq=tq)
    return pl.pallas_call(
        kern,
        grid=(B, H // 2, T // tq),
        in_specs=[
            pl.BlockSpec((1, 2, tq, LANES), lambda b, hp, qi: (b, hp, qi, 0)),
            pl.BlockSpec((1, 2, T, LANES), lambda b, hp, qi: (b, hp, 0, 0)),
            pl.BlockSpec((1, T, LANES), lambda b, hp, qi: (b, 0, hp)),
        ],
        out_specs=pl.BlockSpec((1, tq, LANES), lambda b, hp, qi: (b, qi, hp)),
        out_shape=jax.ShapeDtypeStruct(v.shape, BF16),
        scratch_shapes=[pltpu.VMEM((2, tq, LANES), F32)] * 3,
        compiler_params=pltpu.CompilerParams(
            dimension_semantics=("arbitrary", "arbitrary", "arbitrary"),
            vmem_limit_bytes=VMEM_LIMIT),
        name="fox",
    )(q_aug, k_aug, v)


def _unit_lower_inverse(Ls):
    n = Ls[0].shape[0]
    r = lax.broadcasted_iota(jnp.int32, (n, n), 0)
    c = lax.broadcasted_iota(jnp.int32, (n, n), 1)
    same = (r >> 1) == (c >> 1)
    eye = jnp.where(r == c, 1.0, 0.0)
    ps = [eye + jnp.where(same, L, 0.0) for L in Ls]
    for s in range(2, 7):
        merged = (r >> s) == (c >> s)
        sel = merged & jnp.logical_not(same)
        offs = [jnp.where(sel, L, 0.0).astype(BF16) for L in Ls]
        pbs = [p.astype(BF16) for p in ps]
        tmps = [_dot(pb, off).astype(BF16) for pb, off in zip(pbs, offs)]
        ps = [p + _dot(tmp, pb) for p, tmp, pb in zip(ps, tmps, pbs)]
        same = merged
    return ps


def _rwkv_kernel(rw_ref, rwprev_ref, mu_ref, w0_ref, a0_ref, kk_ref, ka_ref, rk_ref,
                 lnw_ref, lnb_ref, wup_ref, aup_ref, gup_ref, ones_ref, tri_ref,
                 o_ref, s_scr, a_scr, b_scr, r_scr, k_scr, v_scr, gc_scr, y_scr, *, width):
    t = pl.program_id(1)
    tc = rw_ref.shape[1]
    n_chunks = tc // CHUNK
    n_groups = width // GROUP_W

    @pl.when(t == 0)
    def _():
        s_scr[...] = jnp.zeros_like(s_scr)

    ones_blk = ones_ref[...]

    def segsum(x):
        hi, lo = _split2(x)
        return _dot(hi, ones_blk) + _dot(lo, ones_blk)

    p = rw_ref[0]
    prev_row = jnp.where(t > 0, rwprev_ref[0, 7:8, :], 0.0)
    row0 = lax.broadcasted_iota(jnp.int32, (tc, 1), 0) == 0
    p_prev = jnp.where(row0, prev_row, pltpu.roll(p, 1, 0))
    p = p + mu_ref[...] * (p_prev - p)
    r = p[:, 0:width]
    k = p[:, width:2 * width]
    v = p[:, 2 * width:3 * width]
    o = 3 * width
    w_lat = p[:, o:o + LANES]
    a_lat = p[:, o + LANES:o + 2 * LANES]
    g_lat = p[:, o + 2 * LANES:o + 4 * LANES]

    w_log = -_softplus(-(w0_ref[...] + _dot(jnp.tanh(w_lat).astype(BF16), wup_ref[...]))) - 0.5
    g_log = -jnp.exp(w_log)
    lr = _sigmoid(a0_ref[...] + _dot(a_lat.astype(BF16), aup_ref[...]))
    gate = _dot(_sigmoid(g_lat).astype(BF16), gup_ref[...])
    kk = k * kk_ref[...]
    kk = kk / jnp.maximum(jnp.sqrt(segsum(kk * kk)), 1e-12)
    k = k * (1.0 + (lr - 1.0) * ka_ref[...])
    bonus = segsum(r * k * rk_ref[...]) * v

    G = _dot_exact_lhs01(tri_ref[...], g_log)
    e_pos = jnp.exp(G)
    e_neg = jnp.exp(-G)
    a_scr[...] = (-kk * jnp.exp(G - g_log)).astype(BF16)
    b_scr[...] = (kk * lr * e_neg).astype(BF16)
    r_scr[...] = (r * e_pos).astype(BF16)
    k_scr[...] = (k * e_neg).astype(BF16)
    v_scr[...] = v.astype(BF16)
    for ci in range(n_chunks):
        gc_scr[ci:ci + 1, :] = e_pos[(ci + 1) * CHUNK - 1:(ci + 1) * CHUNK, :]

    lane_head = lax.broadcasted_iota(jnp.int32, (CHUNK, GROUP_W), 1) // HEAD_DIM
    head_masks = [lane_head == h for h in range(GROUP_HEADS)]
    rt = lax.broadcasted_iota(jnp.int32, (GROUP_W, GROUP_W), 0) % CHUNK
    ct = lax.broadcasted_iota(jnp.int32, (GROUP_W, GROUP_W), 1) % CHUNK
    strict = rt > ct
    incl = rt >= ct

    def stack(x):
        zero = jnp.zeros_like(x)
        return jnp.concatenate([jnp.where(mk, x, zero) for mk in head_masks], axis=0)

    units = [(ci, g) for ci in range(n_chunks) for g in range(n_groups)]
    rows = lambda ci: slice(ci * CHUNK, (ci + 1) * CHUNK)
    cols = lambda g: slice(g * GROUP_W, (g + 1) * GROUP_W)
    a_s = {u: stack(a_scr[rows(u[0]), cols(u[1])]) for u in units}
    b_s = {u: stack(b_scr[rows(u[0]), cols(u[1])]) for u in units}
    r_s = {u: stack(r_scr[rows(u[0]), cols(u[1])]) for u in units}
    k_s = {u: stack(k_scr[rows(u[0]), cols(u[1])]) for u in units}
    v_s = {u: stack(v_scr[rows(u[0]), cols(u[1])]) for u in units}
    l_all = {u: _dot_nt(jnp.concatenate([a_s[u], r_s[u]], axis=0),
                        jnp.concatenate([b_s[u], k_s[u]], axis=0)) for u in units}
    l_ab = {u: jnp.where(strict, l_all[u][0:GROUP_W, 0:GROUP_W], 0.0) for u in units}
    l_ak = {u: jnp.where(strict, l_all[u][0:GROUP_W, GROUP_W:], 0.0).astype(BF16) for u in units}
    l_rb = {u: jnp.where(incl, l_all[u][GROUP_W:, 0:GROUP_W], 0.0).astype(BF16) for u in units}
    l_rk = {u: jnp.where(incl, l_all[u][GROUP_W:, GROUP_W:], 0.0).astype(BF16) for u in units}
    akv = {u: _dot(l_ak[u], v_s[u]).astype(BF16) for u in units}
    y_rk = {u: _dot(l_rk[u], v_s[u]) for u in units}
    t_inv = _unit_lower_inverse([l_ab[u] for u in units])
    wu = {u: _dot(t_inv[i].astype(BF16), jnp.concatenate([a_s[u], akv[u]], axis=1))
          for i, u in enumerate(units)}

    state = [s_scr[g] for g in range(n_groups)]
    for ci in range(n_chunks):
        for g in range(n_groups):
            u = (ci, g)
            s_b = state[g].astype(BF16)
            z_b = (_dot_nt(wu[u][:, 0:GROUP_W].astype(BF16), s_b) + wu[u][:, GROUP_W:]).astype(BF16)
            y_s = (_dot_nt(r_s[u], s_b) + _dot(l_rb[u], z_b)) + y_rk[u]
            y = (y_s[0:CHUNK] + y_s[CHUNK:2 * CHUNK]) + (y_s[2 * CHUNK:3 * CHUNK] + y_s[3 * CHUNK:])
            y_scr[rows(ci), cols(g)] = y
            gc = gc_scr[ci:ci + 1, cols(g)]
            bh_s = (b_s[u].astype(F32) * gc).astype(BF16)
            kh_s = (k_s[u].astype(F32) * gc).astype(BF16)
            state[g] = state[g] * gc + _dot_tn(jnp.concatenate([z_b, v_s[u]], axis=0),
                                               jnp.concatenate([bh_s, kh_s], axis=0))
    for g in range(n_groups):
        s_scr[g] = state[g]

    y = y_scr[...]
    d = y - segsum(y) * (1.0 / HEAD_DIM)
    var = segsum(d * d) * (1.0 / HEAD_DIM)
    yn = d * lax.rsqrt(var + LNX_EPS) * lnw_ref[...] + lnb_ref[...]
    o_ref[0] = ((yn + bonus) * gate).astype(o_ref.dtype)


def _rwkv(rw, mu, w0, a0, k_k, k_a, r_k, ln_w, ln_b, w_up, a_up, g_up, ones_blk, tri_blk, *, width, tc):
    B, T, rw_w = rw.shape
    n_groups = width // GROUP_W
    kern = functools.partial(_rwkv_kernel, width=width)
    sub = 8
    small = [mu, w0, a0, k_k, k_a, r_k, ln_w, ln_b, w_up, a_up, g_up, ones_blk, tri_blk]
    return pl.pallas_call(
        kern,
        grid=(B, T // tc),
        in_specs=[
            pl.BlockSpec((1, tc, rw_w), lambda b, t: (b, t, 0)),
            pl.BlockSpec((1, sub, rw_w), lambda b, t: (b, jnp.maximum(t * (tc // sub) - 1, 0), 0)),
        ] + [_const_spec(a.shape) for a in small],
        out_specs=pl.BlockSpec((1, tc, width), lambda b, t: (b, t, 0)),
        out_shape=jax.ShapeDtypeStruct((B, T, width), BF16),
        scratch_shapes=[
            pltpu.VMEM((n_groups, GROUP_W, GROUP_W), F32),
            pltpu.VMEM((tc, width), BF16),
            pltpu.VMEM((tc, width), BF16),
            pltpu.VMEM((tc, width), BF16),
            pltpu.VMEM((tc, width), BF16),
            pltpu.VMEM((tc, width), BF16),
            pltpu.VMEM((tc // CHUNK, width), F32),
            pltpu.VMEM((tc, width), F32),
        ],
        compiler_params=pltpu.CompilerParams(
            dimension_semantics=("arbitrary", "arbitrary"), vmem_limit_bytes=VMEM_LIMIT),
        name="rwkv",
    )(rw, rw, *small)


def _outffn_kernel(x_ref, of_ref, or_ref, wof_ref, wor_ref, gpost_ref, gfpre_ref, gfpost_ref,
                   wg_ref, wu_ref, wd_ref, o_ref):
    mix = _dot(of_ref[...], wof_ref[...]) + _dot(or_ref[...], wor_ref[...])
    h = x_ref[...] + _rms(mix) * gpost_ref[...]
    z = (_rms(h) * gfpre_ref[...]).astype(BF16)
    gate = _dot(z, wg_ref[...])
    up = _dot(z, wu_ref[...])
    act = (gate * _sigmoid(gate) * up).astype(BF16)
    f = _dot(act, wd_ref[...])
    o_ref[...] = h + _rms(f) * gfpost_ref[...]


def _outffn(x2, o_fox, o_rw, wof, wor, g_post, g_fpre, g_fpost, wg, wu, wd, *, tm):
    N, D = x2.shape
    consts = [wof, wor, g_post, g_fpre, g_fpost, wg, wu, wd]
    return pl.pallas_call(
        _outffn_kernel,
        grid=(N // tm,),
        in_specs=[
            pl.BlockSpec((tm, D), lambda i: (i, 0)),
            pl.BlockSpec((tm, o_fox.shape[1]), lambda i: (i, 0)),
            pl.BlockSpec((tm, o_rw.shape[1]), lambda i: (i, 0)),
        ] + [_const_spec(a.shape) for a in consts],
        out_specs=pl.BlockSpec((tm, D), lambda i: (i, 0)),
        out_shape=jax.ShapeDtypeStruct((N, D), F32),
        compiler_params=pltpu.CompilerParams(
            dimension_semantics=("arbitrary",), vmem_limit_bytes=VMEM_LIMIT),
        name="outffn",
    )(x2, o_fox, o_rw, *consts)


def _pad_cols(a, n):
    return jnp.pad(a, ((0, 0), (0, n - a.shape[1])))


def _pad_rows(a, n):
    return jnp.pad(a, ((0, n - a.shape[0]), (0, 0)))


def _block_layer(h, attn_norm_pre, attn_norm_post, w_in, fox_forget_bias, shift_mu, rwkv_w0,
                 rwkv_w_up, rwkv_a0, rwkv_a_up, rwkv_g_up, rwkv_k_k, rwkv_k_a, rwkv_r_k,
                 rwkv_ln_w, rwkv_ln_b, w_out, ffn_norm_pre, ffn_norm_post, ffn_w_gate, ffn_w_up,
                 ffn_w_down):
    B, T, D = h.shape
    fw = D // 2
    rwid = D // 2
    nh = fw // HEAD_DIM
    hw = nh * LANES

    fq, fk, fv = w_in[:, 0:fw], w_in[:, fw:2 * fw], w_in[:, 2 * fw:3 * fw]
    ffw = w_in[:, 3 * fw:3 * fw + nh]
    rcol = 3 * fw + nh

    def head_pad(w):
        return jnp.pad(w.reshape(D, nh, HEAD_DIM), ((0, 0), (0, 0), (0, LANES - HEAD_DIM))).reshape(D, hw)

    def rw_layout(a):
        o = 3 * rwid
        return jnp.concatenate([
            a[:, 0:o],
            _pad_cols(a[:, o:o + DECAY_LORA], LANES),
            _pad_cols(a[:, o + DECAY_LORA:o + DECAY_LORA + AAA_LORA], LANES),
            _pad_cols(a[:, o + DECAY_LORA + AAA_LORA:], 2 * LANES)], axis=1)

    w_rw = rw_layout(w_in[:, rcol:])
    rw_w = w_rw.shape[1]
    wp = jnp.concatenate([head_pad(fq), head_pad(fk), fv, _pad_cols(ffw, LANES), w_rw], axis=1).astype(BF16)
    mu = rw_layout(shift_mu[None, :])
    fbias = _pad_cols(fox_forget_bias[None, :], LANES)

    tm1 = 512
    ii = jnp.arange(tm1)
    tri = (ii[:, None] >= ii[None, :]).astype(BF16)
    lane = jnp.arange(LANES)[:, None]
    colq = jnp.arange(hw)[None, :]
    sel = jnp.stack([(colq == lane * LANES + HEAD_DIM + 3 + p) & (lane < nh) for p in range(3)]).astype(BF16)
    cl = jnp.arange(hw) % LANES
    qconst = ((cl >= HEAD_DIM + 3) & (cl < HEAD_DIM + 6)).astype(F32)[None, :]

    q_aug, k_aug, v, rw = _inproj(h, attn_norm_pre[None, :], wp, fbias, tri, sel, qconst,
                                  n_heads=nh, rw_w=rw_w, tm=tm1)

    o_fox = _fox(q_aug, k_aug, v, tq=512)

    tc = 256
    hh = jnp.arange(rwid) // HEAD_DIM
    ones_blk = (hh[:, None] == hh[None, :]).astype(BF16)
    jj = jnp.arange(tc)
    tri_blk = ((jj[:, None] >= jj[None, :]) & (jj[:, None] // CHUNK == jj[None, :] // CHUNK)).astype(BF16)
    row = lambda a: a.reshape(1, -1)
    o_rw = _rwkv(rw, mu, row(rwkv_w0), row(rwkv_a0), row(rwkv_k_k), row(rwkv_k_a), row(rwkv_r_k),
                 row(rwkv_ln_w), row(rwkv_ln_b),
                 _pad_rows(rwkv_w_up, LANES).astype(BF16), _pad_rows(rwkv_a_up, LANES).astype(BF16),
                 _pad_rows(rwkv_g_up, 2 * LANES).astype(BF16), ones_blk, tri_blk, width=rwid, tc=tc)

    out = _outffn(h.reshape(B * T, D), o_fox.reshape(B * T, fw), o_rw.reshape(B * T, rwid),
                  w_out[0:fw].astype(BF16), w_out[fw:].astype(BF16),
                  row(attn_norm_post), row(ffn_norm_pre), row(ffn_norm_post),
                  ffn_w_gate.astype(BF16), ffn_w_up.astype(BF16), ffn_w_down.astype(BF16), tm=256)
    return out.reshape(B, T, D)


def kernel(x, attn_norm_pre, attn_norm_post, w_in, fox_forget_bias, shift_mu, rwkv_w0, rwkv_w_up,
           rwkv_a0, rwkv_a_up, rwkv_g_up, rwkv_k_k, rwkv_k_a, rwkv_r_k, rwkv_ln_w, rwkv_ln_b, w_out,
           ffn_norm_pre, ffn_norm_post, ffn_w_gate, ffn_w_up, ffn_w_down):
    h = x
    for l in range(attn_norm_pre.shape[0]):
        h = _block_layer(h, attn_norm_pre[l], attn_norm_post[l], w_in[l], fox_forget_bias[l],
                         shift_mu[l], rwkv_w0[l], rwkv_w_up[l], rwkv_a0[l], rwkv_a_up[l],
                         rwkv_g_up[l], rwkv_k_k[l], rwkv_k_a[l], rwkv_r_k[l], rwkv_ln_w[l],
                         rwkv_ln_b[l], w_out[l], ffn_norm_pre[l], ffn_norm_post[l], ffn_w_gate[l],
                         ffn_w_up[l], ffn_w_down[l])
    return h
```

```python
import functools

import jax
import jax.numpy as jnp
from jax import lax
from jax.experimental import pallas as pl
from jax.experimental.pallas import tpu as pltpu

F32 = jnp.float32
BF16 = jnp.bfloat16

HEAD_DIM = 64
LANES = 128
NORM_EPS = 1e-6
LNX_EPS = 64e-5
LOG2E = 1.4426950408889634
DECAY_LORA = 64
AAA_LORA = 64
GATE_LORA = 160
CHUNK = 64
GROUP_HEADS = 4
GROUP_W = GROUP_HEADS * HEAD_DIM
VMEM_LIMIT = 56 * 1024 * 1024


def _dot(a, b):
    return jnp.dot(a, b, preferred_element_type=F32)


def _dot_nt(a, b):
    return lax.dot_general(a, b, (((1,), (1,)), ((), ())), preferred_element_type=F32)


def _dot_tn(a, b):
    return lax.dot_general(a, b, (((0,), (0,)), ((), ())), preferred_element_type=F32)


def _split2(x):
    hi = x.astype(BF16)
    lo = (x - hi.astype(F32)).astype(BF16)
    return hi, lo


def _split3(x):
    hi = x.astype(BF16)
    r = x - hi.astype(F32)
    mid = r.astype(BF16)
    lo = (r - mid.astype(F32)).astype(BF16)
    return hi, mid, lo


def _dot_x3(a, b):
    ah, al = _split2(a)
    bh, bl = _split2(b)
    return _dot(ah, bh) + (_dot(ah, bl) + _dot(al, bh))


def _dot_exact_rhs01(x, ones01):
    hi, mid, lo = _split3(x)
    return _dot(hi, ones01) + (_dot(mid, ones01) + _dot(lo, ones01))


def _dot_exact_lhs01(ones01, x):
    hi, mid, lo = _split3(x)
    return _dot(ones01, hi) + (_dot(ones01, mid) + _dot(ones01, lo))


def _rms(x):
    return x * lax.rsqrt(jnp.mean(x * x, axis=-1, keepdims=True) + NORM_EPS)


def _softplus(x):
    return jnp.maximum(x, 0.0) + jnp.log1p(jnp.exp(-jnp.abs(x)))


def _sigmoid(x):
    return 1.0 / (1.0 + jnp.exp(-x))


def _const_spec(shape):
    nd = len(shape)
    return pl.BlockSpec(shape, lambda *_: (0,) * nd)


def _inproj_kernel(x_ref, g_ref, w_ref, fb_ref, tri_ref, sel_ref, qc_ref,
                   q_out, k_out, v_out, rw_out, carry_ref, *, n_heads, rw_w):
    t = pl.program_id(1)

    @pl.when(t == 0)
    def _():
        carry_ref[...] = jnp.zeros_like(carry_ref)

    hw = n_heads * LANES
    u = (_rms(x_ref[0]) * g_ref[...]).astype(BF16)

    q = _dot(u, w_ref[:, 0:hw]) * (HEAD_DIM ** -0.5 * LOG2E) + qc_ref[...]
    for h in range(n_heads):
        q_out[0, h] = q[:, h * LANES:(h + 1) * LANES].astype(BF16)

    f0 = 2 * hw + n_heads * HEAD_DIM
    ff = _dot(u, w_ref[:, f0:f0 + LANES]) + fb_ref[...]
    logf = -_softplus(-ff)
    c = _dot_exact_lhs01(tri_ref[...], logf) + carry_ref[...]
    tm = c.shape[0]
    carry_ref[...] = c[tm - 1:tm, :]

    nhi, nmid, nlo = _split3(c * (-LOG2E))
    kadd = _dot(nhi, sel_ref[0]) + (_dot(nmid, sel_ref[1]) + _dot(nlo, sel_ref[2]))
    k = _dot(u, w_ref[:, hw:2 * hw]) + kadd
    for h in range(n_heads):
        k_out[0, h] = k[:, h * LANES:(h + 1) * LANES].astype(BF16)

    v = _dot(u, w_ref[:, 2 * hw:f0])
    lane = lax.broadcasted_iota(jnp.int32, (tm, LANES), 1)
    one_even = jnp.where(lane == HEAD_DIM, 1.0, 0.0)
    one_odd = jnp.where(lane == 0, 1.0, 0.0)
    for hp in range(n_heads // 2):
        blk = v[:, hp * LANES:(hp + 1) * LANES]
        v_out[0, 2 * hp] = jnp.where(lane < HEAD_DIM, blk, one_even).astype(BF16)
        v_out[0, 2 * hp + 1] = jnp.where(lane >= HEAD_DIM, blk, one_odd).astype(BF16)
    r0 = f0 + LANES
    rw_out[0] = _dot(u, w_ref[:, r0:r0 + rw_w])


def _inproj(x, g_pre, wp, fbias, tri, sel, qconst, *, n_heads, rw_w, tm):
    B, T, D = x.shape
    fw = n_heads * HEAD_DIM
    kern = functools.partial(_inproj_kernel, n_heads=n_heads, rw_w=rw_w)
    return pl.pallas_call(
        kern,
        grid=(B, T // tm),
        in_specs=[
            pl.BlockSpec((1, tm, D), lambda b, t: (b, t, 0)),
            _const_spec(g_pre.shape),
            _const_spec(wp.shape),
            _const_spec(fbias.shape),
            _const_spec(tri.shape),
            _const_spec(sel.shape),
            _const_spec(qconst.shape),
        ],
        out_specs=[
            pl.BlockSpec((1, n_heads, tm, LANES), lambda b, t: (b, 0, t, 0)),
            pl.BlockSpec((1, n_heads, tm, LANES), lambda b, t: (b, 0, t, 0)),
            pl.BlockSpec((1, n_heads, tm, LANES), lambda b, t: (b, 0, t, 0)),
            pl.BlockSpec((1, tm, rw_w), lambda b, t: (b, t, 0)),
        ],
        out_shape=[
            jax.ShapeDtypeStruct((B, n_heads, T, LANES), BF16),
            jax.ShapeDtypeStruct((B, n_heads, T, LANES), BF16),
            jax.ShapeDtypeStruct((B, n_heads, T, LANES), BF16),
            jax.ShapeDtypeStruct((B, T, rw_w), F32),
        ],
        scratch_shapes=[pltpu.VMEM((1, LANES), F32)],
        compiler_params=pltpu.CompilerParams(
            dimension_semantics=("arbitrary", "arbitrary"), vmem_limit_bytes=VMEM_LIMIT),
        name="inproj",
    )(x, g_pre, wp, fbias, tri, sel, qconst)


def _fox_kernel(q_ref, k_ref, v_ref, o_ref, m_scr, acc_scr, *, tq):
    qi = pl.program_id(2)
    reps = tq // LANES
    heads = range(2)

    def step(j, masked):
        rows = pl.ds(pl.multiple_of(j * tq, tq), tq)
        s = [_dot_nt(q_ref[0, h], k_ref[0, h, rows, :]) for h in heads]
        if masked:
            row = lax.broadcasted_iota(jnp.int32, (tq, tq), 0)
            col = lax.broadcasted_iota(jnp.int32, (tq, tq), 1)
            s = [jnp.where(col <= row, sh, -jnp.inf) for sh in s]
        m_prev = [m_scr[h] for h in heads]
        m_next = [jnp.maximum(m_prev[h], jnp.max(s[h], axis=1, keepdims=True)) for h in heads]
        p = [jnp.exp2(s[h] - jnp.concatenate([m_next[h]] * reps, axis=1)).astype(BF16) for h in heads]
        for h in heads:
            alpha = jnp.exp2(m_prev[h] - m_next[h])
            acc_scr[h] = alpha * acc_scr[h] + _dot(p[h], v_ref[0, h, rows, :])
            m_scr[h] = m_next[h]

    for h in heads:
        m_scr[h] = jnp.full(m_scr.shape[1:], -jnp.inf, F32)
        acc_scr[h] = jnp.zeros(acc_scr.shape[1:], F32)

    def body(j, carry):
        step(j, masked=False)
        return carry

    lax.fori_loop(0, qi, body, 0)
    step(qi, masked=True)

    lane = lax.broadcasted_iota(jnp.int32, acc_scr.shape[1:], 1)
    acc0 = acc_scr[0]
    acc1 = acc_scr[1]
    o = jnp.where(lane < HEAD_DIM, acc0 / acc0[:, HEAD_DIM:HEAD_DIM + 1], acc1 / acc1[:, 0:1])
    o_ref[0] = o.astype(o_ref.dtype)


def _fox(q_aug, k_aug, v_aug, *, tq):
    B, H, T, _ = q_aug.shape
    kern = functools.partial(_fox_kernel, tq=tq)
    return pl.pallas_call(
        kern,
        grid=(B, H // 2, T // tq),
        in_specs=[
            pl.BlockSpec((1, 2, tq, LANES), lambda b, hp, qi: (b, hp, qi, 0)),
            pl.BlockSpec((1, 2, T, LANES), lambda b, hp, qi: (b, hp, 0, 0)),
            pl.BlockSpec((1, 2, T, LANES), lambda b, hp, qi: (b, hp, 0, 0)),
        ],
        out_specs=pl.BlockSpec((1, tq, LANES), lambda b, hp, qi: (b, qi, hp)),
        out_shape=jax.ShapeDtypeStruct((B, T, H * HEAD_DIM), BF16),
        scratch_shapes=[pltpu.VMEM((2, tq, LANES), F32)] * 2,
        compiler_params=pltpu.CompilerParams(
            dimension_semantics=("arbitrary", "arbitrary", "arbitrary"),
            vmem_limit_bytes=VMEM_LIMIT),
        name="fox",
    )(q_aug, k_aug, v_aug)


def _unit_lower_inverse(Ls):
    n = Ls[0].shape[0]
    r = lax.broadcasted_iota(jnp.int32, (n, n), 0)
    c = lax.broadcasted_iota(jnp.int32, (n, n), 1)
    same = (r >> 1) == (c >> 1)
    eye = jnp.where(r == c, 1.0, 0.0)
    ps = [eye + jnp.where(same, L, 0.0) for L in Ls]
    for s in range(2, 7):
        merged = (r >> s) == (c >> s)
        sel = merged & jnp.logical_not(same)
        offs = [jnp.where(sel, L, 0.0).astype(BF16) for L in Ls]
        pbs = [p.astype(BF16) for p in ps]
        tmps = [_dot(pb, off).astype(BF16) for pb, off in zip(pbs, offs)]
        ps = [p + _dot(tmp, pb) for p, tmp, pb in zip(ps, tmps, pbs)]
        same = merged
    return ps


def _rwkv_kernel(rw_ref, rwprev_ref, mu_ref, w0_ref, a0_ref, kk_ref, ka_ref, rk_ref,
                 lnw_ref, lnb_ref, wup_ref, aup_ref, gup_ref, ones_ref, tri_ref,
                 o_ref, s_scr, a_scr, b_scr, r_scr, k_scr, v_scr, gc_scr, y_scr, *, width):
    t = pl.program_id(1)
    tc = rw_ref.shape[1]
    n_chunks = tc // CHUNK
    n_groups = width // GROUP_W

    @pl.when(t == 0)
    def _():
        s_scr[...] = jnp.zeros_like(s_scr)

    ones_blk = ones_ref[...]

    def segsum(x):
        hi, lo = _split2(x)
        return _dot(hi, ones_blk) + _dot(lo, ones_blk)

    p = rw_ref[0]
    prev_row = jnp.where(t > 0, rwprev_ref[0, 7:8, :], 0.0)
    row0 = lax.broadcasted_iota(jnp.int32, (tc, 1), 0) == 0
    p_prev = jnp.where(row0, prev_row, pltpu.roll(p, 1, 0))
    p = p + mu_ref[...] * (p_prev - p)
    r = p[:, 0:width]
    k = p[:, width:2 * width]
    v = p[:, 2 * width:3 * width]
    o = 3 * width
    w_lat = p[:, o:o + LANES]
    a_lat = p[:, o + LANES:o + 2 * LANES]
    g_lat = p[:, o + 2 * LANES:o + 4 * LANES]

    w_log = -_softplus(-(w0_ref[...] + _dot(jnp.tanh(w_lat).astype(BF16), wup_ref[...]))) - 0.5
    g_log = -jnp.exp(w_log)
    lr = _sigmoid(a0_ref[...] + _dot(a_lat.astype(BF16), aup_ref[...]))
    gate = _dot(_sigmoid(g_lat).astype(BF16), gup_ref[...])
    kk = k * kk_ref[...]
    kk = kk / jnp.maximum(jnp.sqrt(segsum(kk * kk)), 1e-12)
    k = k * (1.0 + (lr - 1.0) * ka_ref[...])
    bonus = segsum(r * k * rk_ref[...]) * v

    G = _dot_exact_lhs01(tri_ref[...], g_log)
    e_pos = jnp.exp(G)
    e_neg = jnp.exp(-G)
    a_scr[...] = (-kk * jnp.exp(G - g_log)).astype(BF16)
    b_scr[...] = (kk * lr * e_neg).astype(BF16)
    r_scr[...] = (r * e_pos).astype(BF16)
    k_scr[...] = (k * e_neg).astype(BF16)
    v_scr[...] = v.astype(BF16)
    for ci in range(n_chunks):
        gc_scr[ci:ci + 1, :] = e_pos[(ci + 1) * CHUNK - 1:(ci + 1) * CHUNK, :]

    lane_head = lax.broadcasted_iota(jnp.int32, (CHUNK, GROUP_W), 1) // HEAD_DIM
    head_masks = [lane_head == h for h in range(GROUP_HEADS)]
    rt = lax.broadcasted_iota(jnp.int32, (GROUP_W, GROUP_W), 0) % CHUNK
    ct = lax.broadcasted_iota(jnp.int32, (GROUP_W, GROUP_W), 1) % CHUNK
    strict = rt > ct
    incl = rt >= ct

    def stack(x):
        zero = jnp.zeros_like(x)
        return jnp.concatenate([jnp.where(mk, x, zero) for mk in head_masks], axis=0)

    units = [(ci, g) for ci in range(n_chunks) for g in range(n_groups)]
    rows = lambda ci: slice(ci * CHUNK, (ci + 1) * CHUNK)
    cols = lambda g: slice(g * GROUP_W, (g + 1) * GROUP_W)
    a_s = {u: stack(a_scr[rows(u[0]), cols(u[1])]) for u in units}
    b_s = {u: stack(b_scr[rows(u[0]), cols(u[1])]) for u in units}
    r_s = {u: stack(r_scr[rows(u[0]), cols(u[1])]) for u in units}
    k_s = {u: stack(k_scr[rows(u[0]), cols(u[1])]) for u in units}
    v_s = {u: stack(v_scr[rows(u[0]), cols(u[1])]) for u in units}
    l_all = {u: _dot_nt(jnp.concatenate([a_s[u], r_s[u]], axis=0),
                        jnp.concatenate([b_s[u], k_s[u]], axis=0)) for u in units}
    l_ab = {u: jnp.where(strict, l_all[u][0:GROUP_W, 0:GROUP_W], 0.0) for u in units}
    l_ak = {u: jnp.where(strict, l_all[u][0:GROUP_W, GROUP_W:], 0.0).astype(BF16) for u in units}
    l_rb = {u: jnp.where(incl, l_all[u][GROUP_W:, 0:GROUP_W], 0.0).astype(BF16) for u in units}
    l_rk = {u: jnp.where(incl, l_all[u][GROUP_W:, GROUP_W:], 0.0).astype(BF16) for u in units}
    akv = {u: _dot(l_ak[u], v_s[u]).astype(BF16) for u in units}
    y_rk = {u: _dot(l_rk[u], v_s[u]) for u in units}
    t_inv = _unit_lower_inverse([l_ab[u] for u in units])
    wu = {u: _dot(t_inv[i].astype(BF16), jnp.concatenate([a_s[u], akv[u]], axis=1))
          for i, u in enumerate(units)}

    state = [s_scr[g] for g in range(n_groups)]
    for ci in range(n_chunks):
        for g in range(n_groups):
            u = (ci, g)
            s_b = state[g].astype(BF16)
            z_b = (_dot_nt(wu[u][:, 0:GROUP_W].astype(BF16), s_b) + wu[u][:, GROUP_W:]).astype(BF16)
            y_s = (_dot_nt(r_s[u], s_b) + _dot(l_rb[u], z_b)) + y_rk[u]
            y = (y_s[0:CHUNK] + y_s[CHUNK:2 * CHUNK]) + (y_s[2 * CHUNK:3 * CHUNK] + y_s[3 * CHUNK:])
            y_scr[rows(ci), cols(g)] = y
            gc = gc_scr[ci:ci + 1, cols(g)]
            bh_s = (b_s[u].astype(F32) * gc).astype(BF16)
            kh_s = (k_s[u].astype(F32) * gc).astype(BF16)
            state[g] = state[g] * gc + _dot_tn(jnp.concatenate([z_b, v_s[u]], axis=0),
                                               jnp.concatenate([bh_s, kh_s], axis=0))
    for g in range(n_groups):
        s_scr[g] = state[g]

    y = y_scr[...]
    d = y - segsum(y) * (1.0 / HEAD_DIM)
    var = segsum(d * d) * (1.0 / HEAD_DIM)
    yn = d * lax.rsqrt(var + LNX_EPS) * lnw_ref[...] + lnb_ref[...]
    o_ref[0] = ((yn + bonus) * gate).astype(o_ref.dtype)


def _rwkv(rw, mu, w0, a0, k_k, k_a, r_k, ln_w, ln_b, w_up, a_up, g_up, ones_blk, tri_blk, *, width, tc):
    B, T, rw_w = rw.shape
    n_groups = width // GROUP_W
    kern = functools.partial(_rwkv_kernel, width=width)
    sub = 8
    small = [mu, w0, a0, k_k, k_a, r_k, ln_w, ln_b, w_up, a_up, g_up, ones_blk, tri_blk]
    return pl.pallas_call(
        kern,
        grid=(B, T // tc),
        in_specs=[
            pl.BlockSpec((1, tc, rw_w), lambda b, t: (b, t, 0)),
            pl.BlockSpec((1, sub, rw_w), lambda b, t: (b, jnp.maximum(t * (tc // sub) - 1, 0), 0)),
        ] + [_const_spec(a.shape) for a in small],
        out_specs=pl.BlockSpec((1, tc, width), lambda b, t: (b, t, 0)),
        out_shape=jax.ShapeDtypeStruct((B, T, width), BF16),
        scratch_shapes=[
            pltpu.VMEM((n_groups, GROUP_W, GROUP_W), F32),
            pltpu.VMEM((tc, width), BF16),
            pltpu.VMEM((tc, width), BF16),
            pltpu.VMEM((tc, width), BF16),
            pltpu.VMEM((tc, width), BF16),
            pltpu.VMEM((tc, width), BF16),
            pltpu.VMEM((tc // CHUNK, width), F32),
            pltpu.VMEM((tc, width), F32),
        ],
        compiler_params=pltpu.CompilerParams(
            dimension_semantics=("arbitrary", "arbitrary"), vmem_limit_bytes=VMEM_LIMIT),
        name="rwkv",
    )(rw, rw, *small)


def _outffn_kernel(x_ref, of_ref, or_ref, wof_ref, wor_ref, gpost_ref, gfpre_ref, gfpost_ref,
                   wg_ref, wu_ref, wd_ref, o_ref):
    mix = _dot(of_ref[...], wof_ref[...]) + _dot(or_ref[...], wor_ref[...])
    h = x_ref[...] + _rms(mix) * gpost_ref[...]
    z = (_rms(h) * gfpre_ref[...]).astype(BF16)
    gate = _dot(z, wg_ref[...])
    up = _dot(z, wu_ref[...])
    act = (gate * _sigmoid(gate) * up).astype(BF16)
    f = _dot(act, wd_ref[...])
    o_ref[...] = h + _rms(f) * gfpost_ref[...]


def _outffn(x2, o_fox, o_rw, wof, wor, g_post, g_fpre, g_fpost, wg, wu, wd, *, tm):
    N, D = x2.shape
    consts = [wof, wor, g_post, g_fpre, g_fpost, wg, wu, wd]
    return pl.pallas_call(
        _outffn_kernel,
        grid=(N // tm,),
        in_specs=[
            pl.BlockSpec((tm, D), lambda i: (i, 0)),
            pl.BlockSpec((tm, o_fox.shape[1]), lambda i: (i, 0)),
            pl.BlockSpec((tm, o_rw.shape[1]), lambda i: (i, 0)),
        ] + [_const_spec(a.shape) for a in consts],
        out_specs=pl.BlockSpec((tm, D), lambda i: (i, 0)),
        out_shape=jax.ShapeDtypeStruct((N, D), F32),
        compiler_params=pltpu.CompilerParams(
            dimension_semantics=("arbitrary",), vmem_limit_bytes=VMEM_LIMIT),
        name="outffn",
    )(x2, o_fox, o_rw, *consts)


def _pad_cols(a, n):
    return jnp.pad(a, ((0, 0), (0, n - a.shape[1])))


def _pad_rows(a, n):
    return jnp.pad(a, ((0, n - a.shape[0]), (0, 0)))


def _block_layer(h, attn_norm_pre, attn_norm_post, w_in, fox_forget_bias, shift_mu, rwkv_w0,
                 rwkv_w_up, rwkv_a0, rwkv_a_up, rwkv_g_up, rwkv_k_k, rwkv_k_a, rwkv_r_k,
                 rwkv_ln_w, rwkv_ln_b, w_out, ffn_norm_pre, ffn_norm_post, ffn_w_gate, ffn_w_up,
                 ffn_w_down):
    B, T, D = h.shape
    fw = D // 2
    rwid = D // 2
    nh = fw // HEAD_DIM
    hw = nh * LANES

    fq, fk, fv = w_in[:, 0:fw], w_in[:, fw:2 * fw], w_in[:, 2 * fw:3 * fw]
    ffw = w_in[:, 3 * fw:3 * fw + nh]
    rcol = 3 * fw + nh

    def head_pad(w):
        return jnp.pad(w.reshape(D, nh, HEAD_DIM), ((0, 0), (0, 0), (0, LANES - HEAD_DIM))).reshape(D, hw)

    def rw_layout(a):
        o = 3 * rwid
        return jnp.concatenate([
            a[:, 0:o],
            _pad_cols(a[:, o:o + DECAY_LORA], LANES),
            _pad_cols(a[:, o + DECAY_LORA:o + DECAY_LORA + AAA_LORA], LANES),
            _pad_cols(a[:, o + DECAY_LORA + AAA_LORA:], 2 * LANES)], axis=1)

    w_rw = rw_layout(w_in[:, rcol:])
    rw_w = w_rw.shape[1]
    wp = jnp.concatenate([head_pad(fq), head_pad(fk), fv, _pad_cols(ffw, LANES), w_rw], axis=1).astype(BF16)
    mu = rw_layout(shift_mu[None, :])
    fbias = _pad_cols(fox_forget_bias[None, :], LANES)

    tm1 = 512
    ii = jnp.arange(tm1)
    tri = (ii[:, None] >= ii[None, :]).astype(BF16)
    lane = jnp.arange(LANES)[:, None]
    colq = jnp.arange(hw)[None, :]
    sel = jnp.stack([(colq == lane * LANES + HEAD_DIM + 3 + p) & (lane < nh) for p in range(3)]).astype(BF16)
    cl = jnp.arange(hw) % LANES
    qconst = ((cl >= HEAD_DIM + 3) & (cl < HEAD_DIM + 6)).astype(F32)[None, :]

    q_aug, k_aug, v, rw = _inproj(h, attn_norm_pre[None, :], wp, fbias, tri, sel, qconst,
                                  n_heads=nh, rw_w=rw_w, tm=tm1)

    o_fox = _fox(q_aug, k_aug, v, tq=512)

    tc = 256
    hh = jnp.arange(rwid) // HEAD_DIM
    ones_blk = (hh[:, None] == hh[None, :]).astype(BF16)
    jj = jnp.arange(tc)
    tri_blk = ((jj[:, None] >= jj[None, :]) & (jj[:, None] // CHUNK == jj[None, :] // CHUNK)).astype(BF16)
    row = lambda a: a.reshape(1, -1)
    o_rw = _rwkv(rw, mu, row(rwkv_w0), row(rwkv_a0), row(rwkv_k_k), row(rwkv_k_a), row(rwkv_r_k),
                 row(rwkv_ln_w), row(rwkv_ln_b),
                 _pad_rows(rwkv_w_up, LANES).astype(BF16), _pad_rows(rwkv_a_up, LANES).astype(BF16),
                 _pad_rows(rwkv_g_up, 2 * LANES).astype(BF16), ones_blk, tri_blk, width=rwid, tc=tc)

    out = _outffn(h.reshape(B * T, D), o_fox.reshape(B * T, fw), o_rw.reshape(B * T, rwid),
                  w_out[0:fw].astype(BF16), w_out[fw:].astype(BF16),
                  row(attn_norm_post), row(ffn_norm_pre), row(ffn_norm_post),
                  ffn_w_gate.astype(BF16), ffn_w_up.astype(BF16), ffn_w_down.astype(BF16), tm=256)
    return out.reshape(B, T, D)


def kernel(x, attn_norm_pre, attn_norm_post, w_in, fox_forget_bias, shift_mu, rwkv_w0, rwkv_w_up,
           rwkv_a0, rwkv_a_up, rwkv_g_up, rwkv_k_k, rwkv_k_a, rwkv_r_k, rwkv_ln_w, rwkv_ln_b, w_out,
           ffn_norm_pre, ffn_norm_post, ffn_w_gate, ffn_w_up, ffn_w_down):
    h = x
    for l in range(attn_norm_pre.shape[0]):
        h = _block_layer(h, attn_norm_pre[l], attn_norm_post[l], w_in[l], fox_forget_bias[l],
                         shift_mu[l], rwkv_w0[l], rwkv_w_up[l], rwkv_a0[l], rwkv_a_up[l],
                         rwkv_g_up[l], rwkv_k_k[l], rwkv_k_a[l], rwkv_r_k[l], rwkv_ln_w[l],
                         rwkv_ln_b[l], w_out[l], ffn_norm_pre[l], ffn_norm_post[l], ffn_w_gate[l],
                         ffn_w_up[l], ffn_w_down[l])
    return h
```

```python
import functools

import jax
import jax.numpy as jnp
from jax import lax
from jax.experimental import pallas as pl
from jax.experimental.pallas import tpu as pltpu

F32 = jnp.float32
BF16 = jnp.bfloat16

HEAD_DIM = 64
LANES = 128
NORM_EPS = 1e-6
LNX_EPS = 64e-5
LOG2E = 1.4426950408889634
DECAY_LORA = 64
AAA_LORA = 64
GATE_LORA = 160
CHUNK = 64
GROUP_HEADS = 4
GROUP_W = GROUP_HEADS * HEAD_DIM
VMEM_LIMIT = 56 * 1024 * 1024


def _dot(a, b):
    return jnp.dot(a, b, preferred_element_type=F32)


def _dot_nt(a, b):
    return lax.dot_general(a, b, (((1,), (1,)), ((), ())), preferred_element_type=F32)


def _dot_tn(a, b):
    return lax.dot_general(a, b, (((0,), (0,)), ((), ())), preferred_element_type=F32)


def _split2(x):
    hi = x.astype(BF16)
    lo = (x - hi.astype(F32)).astype(BF16)
    return hi, lo


def _split3(x):
    hi = x.astype(BF16)
    r = x - hi.astype(F32)
    mid = r.astype(BF16)
    lo = (r - mid.astype(F32)).astype(BF16)
    return hi, mid, lo


def _dot_exact_lhs01(ones01, x):
    hi, mid, lo = _split3(x)
    return _dot(ones01, hi) + (_dot(ones01, mid) + _dot(ones01, lo))


def _rms(x):
    return x * lax.rsqrt(jnp.mean(x * x, axis=-1, keepdims=True) + NORM_EPS)


def _softplus(x):
    return jnp.maximum(x, 0.0) + jnp.log1p(jnp.exp(-jnp.abs(x)))


def _sigmoid(x):
    return 1.0 / (1.0 + jnp.exp(-x))


def _const_spec(shape):
    nd = len(shape)
    return pl.BlockSpec(shape, lambda *_: (0,) * nd)


def _inproj_kernel(x_ref, g_ref, w_ref, fb_ref, tri_ref, sel_ref, qc_ref,
                   q_out, k_out, v_out, rw_out, carry_ref, *, n_heads, rw_w):
    t = pl.program_id(1)

    @pl.when(t == 0)
    def _():
        carry_ref[...] = jnp.zeros_like(carry_ref)

    hw = n_heads * LANES
    u = (_rms(x_ref[0]) * g_ref[...]).astype(BF16)

    q = _dot(u, w_ref[:, 0:hw]) * (HEAD_DIM ** -0.5 * LOG2E) + qc_ref[...]
    for h in range(n_heads):
        q_out[0, h] = q[:, h * LANES:(h + 1) * LANES].astype(BF16)

    f0 = 2 * hw + n_heads * HEAD_DIM
    ff = _dot(u, w_ref[:, f0:f0 + LANES]) + fb_ref[...]
    logf = -_softplus(-ff)
    c = _dot_exact_lhs01(tri_ref[...], logf) + carry_ref[...]
    tm = c.shape[0]
    carry_ref[...] = c[tm - 1:tm, :]

    nhi, nmid, nlo = _split3(c * (-LOG2E))
    kadd = _dot(nhi, sel_ref[0]) + (_dot(nmid, sel_ref[1]) + _dot(nlo, sel_ref[2]))
    k = _dot(u, w_ref[:, hw:2 * hw]) + kadd
    for h in range(n_heads):
        k_out[0, h] = k[:, h * LANES:(h + 1) * LANES].astype(BF16)

    v = _dot(u, w_ref[:, 2 * hw:f0])
    lane = lax.broadcasted_iota(jnp.int32, (tm, LANES), 1)
    one_even = jnp.where(lane == HEAD_DIM, 1.0, 0.0)
    one_odd = jnp.where(lane == 0, 1.0, 0.0)
    for hp in range(n_heads // 2):
        blk = v[:, hp * LANES:(hp + 1) * LANES]
        v_out[0, 2 * hp] = jnp.where(lane < HEAD_DIM, blk, one_even).astype(BF16)
        v_out[0, 2 * hp + 1] = jnp.where(lane >= HEAD_DIM, blk, one_odd).astype(BF16)
    r0 = f0 + LANES
    rw_out[0] = _dot(u, w_ref[:, r0:r0 + rw_w])


def _inproj(x, g_pre, wp, fbias, tri, sel, qconst, *, n_heads, rw_w, tm):
    B, T, D = x.shape
    kern = functools.partial(_inproj_kernel, n_heads=n_heads, rw_w=rw_w)
    return pl.pallas_call(
        kern,
        grid=(B, T // tm),
        in_specs=[
            pl.BlockSpec((1, tm, D), lambda b, t: (b, t, 0)),
            _const_spec(g_pre.shape),
            _const_spec(wp.shape),
            _const_spec(fbias.shape),
            _const_spec(tri.shape),
            _const_spec(sel.shape),
            _const_spec(qconst.shape),
        ],
        out_specs=[
            pl.BlockSpec((1, n_heads, tm, LANES), lambda b, t: (b, 0, t, 0)),
            pl.BlockSpec((1, n_heads, tm, LANES), lambda b, t: (b, 0, t, 0)),
            pl.BlockSpec((1, n_heads, tm, LANES), lambda b, t: (b, 0, t, 0)),
            pl.BlockSpec((1, tm, rw_w), lambda b, t: (b, t, 0)),
        ],
        out_shape=[
            jax.ShapeDtypeStruct((B, n_heads, T, LANES), BF16),
            jax.ShapeDtypeStruct((B, n_heads, T, LANES), BF16),
            jax.ShapeDtypeStruct((B, n_heads, T, LANES), BF16),
            jax.ShapeDtypeStruct((B, T, rw_w), F32),
        ],
        scratch_shapes=[pltpu.VMEM((1, LANES), F32)],
        compiler_params=pltpu.CompilerParams(
            dimension_semantics=("arbitrary", "arbitrary"), vmem_limit_bytes=VMEM_LIMIT),
        name="inproj",
    )(x, g_pre, wp, fbias, tri, sel, qconst)


def _fox_kernel(q_ref, k_ref, v_ref, o_ref, m_scr, acc_scr, *, tq):
    qi = pl.program_id(2)
    reps = tq // LANES
    heads = range(2)

    def step(j, masked):
        rows = pl.ds(pl.multiple_of(j * tq, tq), tq)
        s = [_dot_nt(q_ref[0, h], k_ref[0, h, rows, :]) for h in heads]
        if masked:
            row = lax.broadcasted_iota(jnp.int32, (tq, tq), 0)
            col = lax.broadcasted_iota(jnp.int32, (tq, tq), 1)
            s = [jnp.where(col <= row, sh, -jnp.inf) for sh in s]
        m_prev = [m_scr[h] for h in heads]
        m_next = [jnp.maximum(m_prev[h], jnp.max(s[h], axis=1, keepdims=True)) for h in heads]
        p = [jnp.exp2(s[h] - jnp.concatenate([m_next[h]] * reps, axis=1)).astype(BF16) for h in heads]
        for h in heads:
            alpha = jnp.exp2(m_prev[h] - m_next[h])
            acc_scr[h] = alpha * acc_scr[h] + _dot(p[h], v_ref[0, h, rows, :])
            m_scr[h] = m_next[h]

    for h in heads:
        m_scr[h] = jnp.full(m_scr.shape[1:], -jnp.inf, F32)
        acc_scr[h] = jnp.zeros(acc_scr.shape[1:], F32)

    def body(j, carry):
        step(j, masked=False)
        return carry

    lax.fori_loop(0, qi, body, 0)
    step(qi, masked=True)

    lane = lax.broadcasted_iota(jnp.int32, acc_scr.shape[1:], 1)
    acc0 = acc_scr[0]
    acc1 = acc_scr[1]
    o = jnp.where(lane < HEAD_DIM, acc0 / acc0[:, HEAD_DIM:HEAD_DIM + 1], acc1 / acc1[:, 0:1])
    o_ref[0] = o.astype(o_ref.dtype)


def _fox(q_aug, k_aug, v_aug, *, tq):
    B, H, T, _ = q_aug.shape
    kern = functools.partial(_fox_kernel, tq=tq)
    return pl.pallas_call(
        kern,
        grid=(B, H // 2, T // tq),
        in_specs=[
            pl.BlockSpec((1, 2, tq, LANES), lambda b, hp, qi: (b, hp, qi, 0)),
            pl.BlockSpec((1, 2, T, LANES), lambda b, hp, qi: (b, hp, 0, 0)),
            pl.BlockSpec((1, 2, T, LANES), lambda b, hp, qi: (b, hp, 0, 0)),
        ],
        out_specs=pl.BlockSpec((1, tq, LANES), lambda b, hp, qi: (b, qi, hp)),
        out_shape=jax.ShapeDtypeStruct((B, T, H * HEAD_DIM), BF16),
        scratch_shapes=[pltpu.VMEM((2, tq, LANES), F32)] * 2,
        compiler_params=pltpu.CompilerParams(
            dimension_semantics=("arbitrary", "arbitrary", "arbitrary"),
            vmem_limit_bytes=VMEM_LIMIT),
        name="fox",
    )(q_aug, k_aug, v_aug)


def _unit_lower_inverse(Ls):
    n = Ls[0].shape[0]
    r = lax.broadcasted_iota(jnp.int32, (n, n), 0)
    c = lax.broadcasted_iota(jnp.int32, (n, n), 1)
    same = (r >> 1) == (c >> 1)
    eye = jnp.where(r == c, 1.0, 0.0)
    ps = [eye + jnp.where(same, L, 0.0) for L in Ls]
    for s in range(2, 7):
        merged = (r >> s) == (c >> s)
        sel = merged & jnp.logical_not(same)
        offs = [jnp.where(sel, L, 0.0).astype(BF16) for L in Ls]
        pbs = [p.astype(BF16) for p in ps]
        tmps = [_dot(pb, off).astype(BF16) for pb, off in zip(pbs, offs)]
        ps = [p + _dot(tmp, pb) for p, tmp, pb in zip(ps, tmps, pbs)]
        same = merged
    return ps


def _rwkv_kernel(rw_ref, rwprev_ref, mu_ref, w0_ref, a0_ref, kk_ref, ka_ref, rk_ref,
                 lnw_ref, lnb_ref, wup_ref, aup_ref, gup_ref, ones_ref, tri_ref,
                 o_ref, s_scr, y_scr, *, width):
    t = pl.program_id(1)
    n_seq, tc, rw_w = rw_ref.shape
    n_rows = n_seq * tc
    n_chunks = tc // CHUNK
    n_groups = width // GROUP_W

    @pl.when(t == 0)
    def _():
        s_scr[...] = jnp.zeros_like(s_scr)

    ones_blk = ones_ref[...]

    def segsum(x):
        hi, lo = _split2(x)
        return _dot(hi, ones_blk) + _dot(lo, ones_blk)

    p = rw_ref[...].reshape(n_rows, rw_w)
    row = lax.broadcasted_iota(jnp.int32, (n_rows, 1), 0)
    p_prev = pltpu.roll(p, 1, 0)
    for s in range(n_seq):
        prev_row = jnp.where(t > 0, rwprev_ref[s, 7:8, :], 0.0)
        p_prev = jnp.where(row == s * tc, prev_row, p_prev)
    p = p + mu_ref[...] * (p_prev - p)
    r = p[:, 0:width]
    k = p[:, width:2 * width]
    v = p[:, 2 * width:3 * width]
    o = 3 * width
    w_lat = p[:, o:o + LANES]
    a_lat = p[:, o + LANES:o + 2 * LANES]
    g_lat = p[:, o + 2 * LANES:o + 4 * LANES]

    z = w0_ref[...] + _dot(jnp.tanh(w_lat).astype(BF16), wup_ref[...])
    g_log = _sigmoid(z) * (-(2.718281828459045 ** -0.5))
    lr = _sigmoid(a0_ref[...] + _dot(a_lat.astype(BF16), aup_ref[...]))
    gate = _dot(_sigmoid(g_lat).astype(BF16), gup_ref[...])
    kk = k * kk_ref[...]
    kk = kk * lax.rsqrt(jnp.maximum(segsum(kk * kk), 1e-24))
    k = k * (1.0 + (lr - 1.0) * ka_ref[...])
    bonus = segsum(r * k * rk_ref[...]) * v

    tri = tri_ref[...]
    G = jnp.concatenate([_dot_exact_lhs01(tri, g_log[s * tc:(s + 1) * tc]) for s in range(n_seq)], axis=0)
    e_pos = jnp.exp(G)
    e_neg = jnp.exp(-G)
    a_t = (-kk * jnp.exp(G - g_log)).astype(BF16)
    b_t = (kk * lr * e_neg).astype(BF16)
    r_t = (r * e_pos).astype(BF16)
    k_t = (k * e_neg).astype(BF16)
    v_t = v.astype(BF16)

    lane_head = lax.broadcasted_iota(jnp.int32, (CHUNK, GROUP_W), 1) // HEAD_DIM
    head_masks = [lane_head == h for h in range(GROUP_HEADS)]
    rt = lax.broadcasted_iota(jnp.int32, (GROUP_W, GROUP_W), 0) % CHUNK
    ct = lax.broadcasted_iota(jnp.int32, (GROUP_W, GROUP_W), 1) % CHUNK
    strict = rt > ct
    incl = rt >= ct
    rows = lambda s, ci: slice((s * n_chunks + ci) * CHUNK, (s * n_chunks + ci + 1) * CHUNK)
    cols = lambda g: slice(g * GROUP_W, (g + 1) * GROUP_W)

    def stack(x, u):
        x = x[rows(u[0], u[1]), cols(u[2])]
        zero = jnp.zeros_like(x)
        return jnp.concatenate([jnp.where(mk, x, zero) for mk in head_masks], axis=0)

    units = [(s, ci, g) for ci in range(n_chunks) for s in range(n_seq) for g in range(n_groups)]
    a_s = {u: stack(a_t, u) for u in units}
    b_s = {u: stack(b_t, u) for u in units}
    r_s = {u: stack(r_t, u) for u in units}
    k_s = {u: stack(k_t, u) for u in units}
    v_s = {u: stack(v_t, u) for u in units}
    l_all = {u: _dot_nt(jnp.concatenate([a_s[u], r_s[u]], axis=0),
                        jnp.concatenate([b_s[u], k_s[u]], axis=0)) for u in units}
    l_ab = {u: jnp.where(strict, l_all[u][0:GROUP_W, 0:GROUP_W], 0.0) for u in units}
    l_ak = {u: jnp.where(strict, l_all[u][0:GROUP_W, GROUP_W:], 0.0).astype(BF16) for u in units}
    l_rb = {u: jnp.where(incl, l_all[u][GROUP_W:, 0:GROUP_W], 0.0).astype(BF16) for u in units}
    l_rk = {u: jnp.where(incl, l_all[u][GROUP_W:, GROUP_W:], 0.0).astype(BF16) for u in units}
    akv = {u: _dot(l_ak[u], v_s[u]).astype(BF16) for u in units}
    y_rk = {u: _dot(l_rk[u], v_s[u]) for u in units}
    t_inv = _unit_lower_inverse([l_ab[u] for u in units])
    wu = {u: _dot(t_inv[n].astype(BF16), jnp.concatenate([a_s[u], akv[u]], axis=1))
          for n, u in enumerate(units)}

    state = {(s, g): s_scr[s * n_groups + g] for s in range(n_seq) for g in range(n_groups)}
    for u in units:
        s, ci, g = u
        st = state[s, g]
        s_b = st.astype(BF16)
        z_b = (_dot_nt(wu[u][:, 0:GROUP_W].astype(BF16), s_b) + wu[u][:, GROUP_W:]).astype(BF16)
        y_s = (_dot_nt(r_s[u], s_b) + _dot(l_rb[u], z_b)) + y_rk[u]
        y_scr[rows(s, ci), cols(g)] = ((y_s[0:CHUNK] + y_s[CHUNK:2 * CHUNK])
                                       + (y_s[2 * CHUNK:3 * CHUNK] + y_s[3 * CHUNK:]))
        last = (s * n_chunks + ci + 1) * CHUNK - 1
        gc = e_pos[last:last + 1, cols(g)]
        bh_s = (b_s[u].astype(F32) * gc).astype(BF16)
        kh_s = (k_s[u].astype(F32) * gc).astype(BF16)
        state[s, g] = st * gc + _dot_tn(jnp.concatenate([z_b, v_s[u]], axis=0),
                                        jnp.concatenate([bh_s, kh_s], axis=0))
    for s in range(n_seq):
        for g in range(n_groups):
            s_scr[s * n_groups + g] = state[s, g]

    y = y_scr[...]
    d = y - segsum(y) * (1.0 / HEAD_DIM)
    var = segsum(d * d) * (1.0 / HEAD_DIM)
    yn = d * lax.rsqrt(var + LNX_EPS) * lnw_ref[...] + lnb_ref[...]
    o_ref[...] = ((yn + bonus) * gate).astype(o_ref.dtype).reshape(n_seq, tc, width)


def _rwkv(rw, mu, w0, a0, k_k, k_a, r_k, ln_w, ln_b, w_up, a_up, g_up, ones_blk, tri_blk, *,
          width, tc, n_seq):
    B, T, rw_w = rw.shape
    n_groups = width // GROUP_W
    kern = functools.partial(_rwkv_kernel, width=width)
    sub = 8
    small = [mu, w0, a0, k_k, k_a, r_k, ln_w, ln_b, w_up, a_up, g_up, ones_blk, tri_blk]
    return pl.pallas_call(
        kern,
        grid=(B // n_seq, T // tc),
        in_specs=[
            pl.BlockSpec((n_seq, tc, rw_w), lambda b, t: (b, t, 0)),
            pl.BlockSpec((n_seq, sub, rw_w), lambda b, t: (b, jnp.maximum(t * (tc // sub) - 1, 0), 0)),
        ] + [_const_spec(a.shape) for a in small],
        out_specs=pl.BlockSpec((n_seq, tc, width), lambda b, t: (b, t, 0)),
        out_shape=jax.ShapeDtypeStruct((B, T, width), BF16),
        scratch_shapes=[
            pltpu.VMEM((n_seq * n_groups, GROUP_W, GROUP_W), F32),
            pltpu.VMEM((n_seq * tc, width), F32),
        ],
        compiler_params=pltpu.CompilerParams(
            dimension_semantics=("arbitrary", "arbitrary"), vmem_limit_bytes=VMEM_LIMIT),
        name="rwkv",
    )(rw, rw, *small)


def _outffn_kernel(x_ref, of_ref, or_ref, wof_ref, wor_ref, gpost_ref, gfpre_ref, gfpost_ref,
                   wg_ref, wu_ref, wd_ref, o_ref):
    mix = _dot(of_ref[...], wof_ref[...]) + _dot(or_ref[...], wor_ref[...])
    h = x_ref[...] + _rms(mix) * gpost_ref[...]
    z = (_rms(h) * gfpre_ref[...]).astype(BF16)
    gate = _dot(z, wg_ref[...])
    up = _dot(z, wu_ref[...])
    act = (gate * _sigmoid(gate) * up).astype(BF16)
    f = _dot(act, wd_ref[...])
    o_ref[...] = h + _rms(f) * gfpost_ref[...]


def _outffn(x2, o_fox, o_rw, wof, wor, g_post, g_fpre, g_fpost, wg, wu, wd, *, tm):
    N, D = x2.shape
    consts = [wof, wor, g_post, g_fpre, g_fpost, wg, wu, wd]
    return pl.pallas_call(
        _outffn_kernel,
        grid=(N // tm,),
        in_specs=[
            pl.BlockSpec((tm, D), lambda i: (i, 0)),
            pl.BlockSpec((tm, o_fox.shape[1]), lambda i: (i, 0)),
            pl.BlockSpec((tm, o_rw.shape[1]), lambda i: (i, 0)),
        ] + [_const_spec(a.shape) for a in consts],
        out_specs=pl.BlockSpec((tm, D), lambda i: (i, 0)),
        out_shape=jax.ShapeDtypeStruct((N, D), F32),
        compiler_params=pltpu.CompilerParams(
            dimension_semantics=("arbitrary",), vmem_limit_bytes=VMEM_LIMIT),
        name="outffn",
    )(x2, o_fox, o_rw, *consts)


def _pad_cols(a, n):
    return jnp.pad(a, ((0, 0), (0, n - a.shape[1])))


def _pad_rows(a, n):
    return jnp.pad(a, ((0, n - a.shape[0]), (0, 0)))


def _block_layer(h, attn_norm_pre, attn_norm_post, w_in, fox_forget_bias, shift_mu, rwkv_w0,
                 rwkv_w_up, rwkv_a0, rwkv_a_up, rwkv_g_up, rwkv_k_k, rwkv_k_a, rwkv_r_k,
                 rwkv_ln_w, rwkv_ln_b, w_out, ffn_norm_pre, ffn_norm_post, ffn_w_gate, ffn_w_up,
                 ffn_w_down):
    B, T, D = h.shape
    fw = D // 2
    rwid = D // 2
    nh = fw // HEAD_DIM
    hw = nh * LANES

    fq, fk, fv = w_in[:, 0:fw], w_in[:, fw:2 * fw], w_in[:, 2 * fw:3 * fw]
    ffw = w_in[:, 3 * fw:3 * fw + nh]
    rcol = 3 * fw + nh

    def head_pad(w):
        return jnp.pad(w.reshape(D, nh, HEAD_DIM), ((0, 0), (0, 0), (0, LANES - HEAD_DIM))).reshape(D, hw)

    def rw_layout(a):
        o = 3 * rwid
        return jnp.concatenate([
            a[:, 0:o],
            _pad_cols(a[:, o:o + DECAY_LORA], LANES),
            _pad_cols(a[:, o + DECAY_LORA:o + DECAY_LORA + AAA_LORA], LANES),
            _pad_cols(a[:, o + DECAY_LORA + AAA_LORA:], 2 * LANES)], axis=1)

    w_rw = rw_layout(w_in[:, rcol:])
    rw_w = w_rw.shape[1]
    wp = jnp.concatenate([head_pad(fq), head_pad(fk), fv, _pad_cols(ffw, LANES), w_rw], axis=1).astype(BF16)
    mu = rw_layout(shift_mu[None, :])
    fbias = _pad_cols(fox_forget_bias[None, :], LANES)

    tm1 = 512
    ii = jnp.arange(tm1)
    tri = (ii[:, None] >= ii[None, :]).astype(BF16)
    lane = jnp.arange(LANES)[:, None]
    colq = jnp.arange(hw)[None, :]
    sel = jnp.stack([(colq == lane * LANES + HEAD_DIM + 3 + p) & (lane < nh) for p in range(3)]).astype(BF16)
    cl = jnp.arange(hw) % LANES
    qconst = ((cl >= HEAD_DIM + 3) & (cl < HEAD_DIM + 6)).astype(F32)[None, :]

    q_aug, k_aug, v, rw = _inproj(h, attn_norm_pre[None, :], wp, fbias, tri, sel, qconst,
                                  n_heads=nh, rw_w=rw_w, tm=tm1)

    o_fox = _fox(q_aug, k_aug, v, tq=512)

    tc = 256
    hh = jnp.arange(rwid) // HEAD_DIM
    ones_blk = (hh[:, None] == hh[None, :]).astype(BF16)
    jj = jnp.arange(tc)
    tri_blk = ((jj[:, None] >= jj[None, :]) & (jj[:, None] // CHUNK == jj[None, :] // CHUNK)).astype(BF16)
    row = lambda a: a.reshape(1, -1)
    o_rw = _rwkv(rw, mu, row(rwkv_w0), row(rwkv_a0), row(rwkv_k_k), row(rwkv_k_a), row(rwkv_r_k),
                 row(rwkv_ln_w), row(rwkv_ln_b),
                 _pad_rows(rwkv_w_up, LANES).astype(BF16), _pad_rows(rwkv_a_up, LANES).astype(BF16),
                 _pad_rows(rwkv_g_up, 2 * LANES).astype(BF16), ones_blk, tri_blk,
                 width=rwid, tc=tc, n_seq=2)

    out = _outffn(h.reshape(B * T, D), o_fox.reshape(B * T, fw), o_rw.reshape(B * T, rwid),
                  w_out[0:fw].astype(BF16), w_out[fw:].astype(BF16),
                  row(attn_norm_post), row(ffn_norm_pre), row(ffn_norm_post),
                  ffn_w_gate.astype(BF16), ffn_w_up.astype(BF16), ffn_w_down.astype(BF16), tm=256)
    return out.reshape(B, T, D)


def kernel(x, attn_norm_pre, attn_norm_post, w_in, fox_forget_bias, shift_mu, rwkv_w0, rwkv_w_up,
           rwkv_a0, rwkv_a_up, rwkv_g_up, rwkv_k_k, rwkv_k_a, rwkv_r_k, rwkv_ln_w, rwkv_ln_b, w_out,
           ffn_norm_pre, ffn_norm_post, ffn_w_gate, ffn_w_up, ffn_w_down):
    h = x
    for l in range(attn_norm_pre.shape[0]):
        h = _block_layer(h, attn_norm_pre[l], attn_norm_post[l], w_in[l], fox_forget_bias[l],
                         shift_mu[l], rwkv_w0[l], rwkv_w_up[l], rwkv_a0[l], rwkv_a_up[l],
                         rwkv_g_up[l], rwkv_k_k[l], rwkv_k_a[l], rwkv_r_k[l], rwkv_ln_w[l],
                         rwkv_ln_b[l], w_out[l], ffn_norm_pre[l], ffn_norm_post[l], ffn_w_gate[l],
                         ffn_w_up[l], ffn_w_down[l])
    return h
```

```python
import functools

import jax
import jax.numpy as jnp
import numpy as np
from jax import lax
from jax.experimental import pallas as pl
from jax.experimental.pallas import tpu as pltpu

F32 = jnp.float32
BF16 = jnp.bfloat16

HEAD_DIM = 64
LANES = 128
NORM_EPS = 1e-6
LNX_EPS = 64e-5
LOG2E = 1.4426950408889634
DECAY_LORA = 64
AAA_LORA = 64
GATE_LORA = 160
CHUNK = 64
GROUP_HEADS = 4
GROUP_W = GROUP_HEADS * HEAD_DIM
FORGET_LANE = DECAY_LORA
VMEM_LIMIT = 56 * 1024 * 1024


def _dot(a, b):
    return jnp.dot(a, b, preferred_element_type=F32)


def _dot_nt(a, b):
    return lax.dot_general(a, b, (((1,), (1,)), ((), ())), preferred_element_type=F32)


def _dot_tn(a, b):
    return lax.dot_general(a, b, (((0,), (0,)), ((), ())), preferred_element_type=F32)


def _split2(x):
    hi = x.astype(BF16)
    lo = (x - hi.astype(F32)).astype(BF16)
    return hi, lo


def _split3(x):
    hi = x.astype(BF16)
    r = x - hi.astype(F32)
    mid = r.astype(BF16)
    lo = (r - mid.astype(F32)).astype(BF16)
    return hi, mid, lo


def _dot_exact_lhs01(ones01, x):
    hi, mid, lo = _split3(x)
    return _dot(ones01, hi) + (_dot(ones01, mid) + _dot(ones01, lo))


def _rms(x):
    return x * lax.rsqrt(jnp.mean(x * x, axis=-1, keepdims=True) + NORM_EPS)


def _softplus(x):
    return jnp.maximum(x, 0.0) + jnp.log1p(jnp.exp(-jnp.abs(x)))


def _sigmoid(x):
    return 1.0 / (1.0 + jnp.exp(-x))


def _const_spec(shape):
    nd = len(shape)
    return pl.BlockSpec(shape, lambda *_: (0,) * nd, pipeline_mode=pl.Buffered(1))


def _inproj_kernel(x_ref, g_ref, w_ref, fb_ref, tri_ref, sel_ref, qc_ref,
                   q_out, k_out, v_out, rw_out, carry_ref, *, n_heads, rw_w, f_col):
    t = pl.program_id(1)

    @pl.when(t == 0)
    def _():
        carry_ref[...] = jnp.zeros_like(carry_ref)

    fw = n_heads * HEAD_DIM
    tm = x_ref.shape[1]
    u = (_rms(x_ref[0]) * g_ref[...]).astype(BF16)
    lane = lax.broadcasted_iota(jnp.int32, (tm, LANES), 1)
    lower = lane < HEAD_DIM

    rw = _dot(u, w_ref[:, 3 * fw:3 * fw + rw_w])
    rw_out[0] = rw

    fo = f_col
    in_f = (lane >= FORGET_LANE) & (lane < FORGET_LANE + n_heads)

    def pack3(x):
        hi = x.astype(BF16).astype(F32)
        r1 = x - hi
        mid = r1.astype(BF16).astype(F32)
        lo = r1 - mid
        keep = lambda a: jnp.where(in_f, a, 0.0)
        return (keep(hi) + pltpu.roll(keep(mid), 8, 1) + pltpu.roll(keep(lo), 16, 1)).astype(BF16)

    def unpack3(x):
        return x + pltpu.roll(x, LANES - 8, 1) + pltpu.roll(x, LANES - 16, 1)

    logf = -_softplus(-(rw[:, fo:fo + LANES] + fb_ref[...]))
    c = unpack3(_dot(tri_ref[...], pack3(logf))) + carry_ref[...]
    carry_ref[...] = c[tm - 1:tm, :]

    qk = _dot(u, w_ref[:, 0:2 * fw])
    k_bias = _dot(pack3(c * (-LOG2E)), sel_ref[...])
    scale = HEAD_DIM ** -0.5 * LOG2E
    for hp in range(n_heads // 2):
        qb = qk[:, hp * LANES:(hp + 1) * LANES] * scale
        kb = qk[:, fw + hp * LANES:fw + (hp + 1) * LANES]
        for h, keep in ((2 * hp, lower), (2 * hp + 1, jnp.logical_not(lower))):
            q_out[0, h] = jnp.where(keep, qb, qc_ref[:, h * LANES:(h + 1) * LANES]).astype(BF16)
            k_out[0, h] = jnp.where(keep, kb, k_bias[:, h * LANES:(h + 1) * LANES]).astype(BF16)

    v = _dot(u, w_ref[:, 2 * fw:3 * fw])
    one_even = jnp.where(lane == HEAD_DIM, 1.0, 0.0)
    one_odd = jnp.where(lane == 0, 1.0, 0.0)
    for hp in range(n_heads // 2):
        blk = v[:, hp * LANES:(hp + 1) * LANES]
        v_out[0, 2 * hp] = jnp.where(lower, blk, one_even).astype(BF16)
        v_out[0, 2 * hp + 1] = jnp.where(lower, one_odd, blk).astype(BF16)


def _inproj(x, g_pre, wp, fbias, tri, sel, qconst, *, n_heads, rw_w, f_col, tm):
    B, T, D = x.shape
    kern = functools.partial(_inproj_kernel, n_heads=n_heads, rw_w=rw_w, f_col=f_col)
    return pl.pallas_call(
        kern,
        grid=(B, T // tm),
        in_specs=[
            pl.BlockSpec((1, tm, D), lambda b, t: (b, t, 0)),
            _const_spec(g_pre.shape),
            _const_spec(wp.shape),
            _const_spec(fbias.shape),
            _const_spec(tri.shape),
            _const_spec(sel.shape),
            _const_spec(qconst.shape),
        ],
        out_specs=[
            pl.BlockSpec((1, n_heads, tm, LANES), lambda b, t: (b, 0, t, 0)),
            pl.BlockSpec((1, n_heads, tm, LANES), lambda b, t: (b, 0, t, 0)),
            pl.BlockSpec((1, n_heads, tm, LANES), lambda b, t: (b, 0, t, 0)),
            pl.BlockSpec((1, tm, rw_w), lambda b, t: (b, t, 0)),
        ],
        out_shape=[
            jax.ShapeDtypeStruct((B, n_heads, T, LANES), BF16),
            jax.ShapeDtypeStruct((B, n_heads, T, LANES), BF16),
            jax.ShapeDtypeStruct((B, n_heads, T, LANES), BF16),
            jax.ShapeDtypeStruct((B, T, rw_w), F32),
        ],
        scratch_shapes=[pltpu.VMEM((1, LANES), F32)],
        compiler_params=pltpu.CompilerParams(
            dimension_semantics=("arbitrary", "arbitrary"), vmem_limit_bytes=VMEM_LIMIT),
        name="inproj",
    )(x, g_pre, wp, fbias, tri, sel, qconst)


def _fox_kernel(q_ref, k_ref, v_ref, o_ref, m_scr, acc_scr, *, tq):
    qi = pl.program_id(2)
    reps = tq // LANES
    heads = range(2)

    def step(j, masked):
        rows = pl.ds(pl.multiple_of(j * tq, tq), tq)
        s = [_dot_nt(q_ref[0, h], k_ref[0, h, rows, :]) for h in heads]
        if masked:
            row = lax.broadcasted_iota(jnp.int32, (tq, tq), 0)
            col = lax.broadcasted_iota(jnp.int32, (tq, tq), 1)
            s = [jnp.where(col <= row, sh, -jnp.inf) for sh in s]
        m_prev = [m_scr[h] for h in heads]
        m_next = [jnp.maximum(m_prev[h], jnp.max(s[h], axis=1, keepdims=True)) for h in heads]
        p = [jnp.exp2(s[h] - jnp.concatenate([m_next[h]] * reps, axis=1)).astype(BF16) for h in heads]
        for h in heads:
            alpha = jnp.exp2(m_prev[h] - m_next[h])
            acc_scr[h] = alpha * acc_scr[h] + _dot(p[h], v_ref[0, h, rows, :])
            m_scr[h] = m_next[h]

    for h in heads:
        m_scr[h] = jnp.full(m_scr.shape[1:], -jnp.inf, F32)
        acc_scr[h] = jnp.zeros(acc_scr.shape[1:], F32)

    def body(j, carry):
        step(j, masked=False)
        return carry

    lax.fori_loop(0, qi, body, 0)
    step(qi, masked=True)

    lane = lax.broadcasted_iota(jnp.int32, acc_scr.shape[1:], 1)
    acc0 = acc_scr[0]
    acc1 = acc_scr[1]
    o = jnp.where(lane < HEAD_DIM, acc0 / acc0[:, HEAD_DIM:HEAD_DIM + 1], acc1 / acc1[:, 0:1])
    o_ref[0] = o.astype(o_ref.dtype)


def _fox(q_aug, k_aug, v_aug, *, tq):
    B, H, T, _ = q_aug.shape
    kern = functools.partial(_fox_kernel, tq=tq)
    return pl.pallas_call(
        kern,
        grid=(B, H // 2, T // tq),
        in_specs=[
            pl.BlockSpec((1, 2, tq, LANES), lambda b, hp, qi: (b, hp, qi, 0)),
            pl.BlockSpec((1, 2, T, LANES), lambda b, hp, qi: (b, hp, 0, 0)),
            pl.BlockSpec((1, 2, T, LANES), lambda b, hp, qi: (b, hp, 0, 0)),
        ],
        out_specs=pl.BlockSpec((1, tq, LANES), lambda b, hp, qi: (b, qi, hp)),
        out_shape=jax.ShapeDtypeStruct((B, T, H * HEAD_DIM), BF16),
        scratch_shapes=[pltpu.VMEM((2, tq, LANES), F32)] * 2,
        compiler_params=pltpu.CompilerParams(
            dimension_semantics=("arbitrary", "arbitrary", "arbitrary"),
            vmem_limit_bytes=VMEM_LIMIT),
        name="fox",
    )(q_aug, k_aug, v_aug)


def _unit_lower_inverse(Ls):
    n = Ls[0].shape[0]
    r = lax.broadcasted_iota(jnp.int32, (n, n), 0)
    c = lax.broadcasted_iota(jnp.int32, (n, n), 1)
    same = (r >> 1) == (c >> 1)
    eye = jnp.where(r == c, 1.0, 0.0)
    ps = [eye + jnp.where(same, L, 0.0) for L in Ls]
    for s in range(2, 7):
        merged = (r >> s) == (c >> s)
        sel = merged & jnp.logical_not(same)
        offs = [jnp.where(sel, L, 0.0).astype(BF16) for L in Ls]
        pbs = [p.astype(BF16) for p in ps]
        tmps = [_dot(pb, off).astype(BF16) for pb, off in zip(pbs, offs)]
        ps = [p + _dot(tmp, pb) for p, tmp, pb in zip(ps, tmps, pbs)]
        same = merged
    return ps


def _rwkv_kernel(rw_ref, rwprev_ref, mu_ref, w0_ref, a0_ref, kk_ref, ka_ref, rk_ref,
                 lnw_ref, lnb_ref, wup_ref, aup_ref, gup_ref, ones_ref, tri_ref,
                 o_ref, s_scr, y_scr, *, width):
    t = pl.program_id(1)
    n_seq, tc, rw_w = rw_ref.shape
    n_rows = n_seq * tc
    n_chunks = tc // CHUNK
    n_groups = width // GROUP_W

    @pl.when(t == 0)
    def _():
        s_scr[...] = jnp.zeros_like(s_scr)

    ones_blk = ones_ref[...]

    def segsum(x):
        hi, lo = _split2(x)
        return _dot(hi, ones_blk) + _dot(lo, ones_blk)

    p = rw_ref[...].reshape(n_rows, rw_w)
    row = lax.broadcasted_iota(jnp.int32, (n_rows, 1), 0)
    p_prev = pltpu.roll(p, 1, 0)
    for s in range(n_seq):
        prev_row = jnp.where(t > 0, rwprev_ref[s, 7:8, :], 0.0)
        p_prev = jnp.where(row == s * tc, prev_row, p_prev)
    p = p + mu_ref[...] * (p_prev - p)
    r = p[:, 0:width]
    k = p[:, width:2 * width]
    v = p[:, 2 * width:3 * width]
    o = 3 * width
    w_lat = p[:, o:o + LANES]
    a_lat = p[:, o + LANES:o + 2 * LANES]
    g_lat = p[:, o + 2 * LANES:o + 4 * LANES]

    z = w0_ref[...] + _dot(jnp.tanh(w_lat).astype(BF16), wup_ref[...])
    g_log = _sigmoid(z) * (-(2.718281828459045 ** -0.5))
    lr = _sigmoid(a0_ref[...] + _dot(a_lat.astype(BF16), aup_ref[...]))
    gate = _dot(_sigmoid(g_lat).astype(BF16), gup_ref[...])
    kk = k * kk_ref[...]
    kk = kk * lax.rsqrt(jnp.maximum(segsum(kk * kk), 1e-24))
    k = k * (1.0 + (lr - 1.0) * ka_ref[...])
    bonus = segsum(r * k * rk_ref[...]) * v

    tri = tri_ref[...]
    G = jnp.concatenate([_dot_exact_lhs01(tri, g_log[s * tc:(s + 1) * tc]) for s in range(n_seq)], axis=0)
    e_pos = jnp.exp(G)
    e_neg = jnp.exp(-G)
    a_t = (-kk * jnp.exp(G - g_log)).astype(BF16)
    b_t = (kk * lr * e_neg).astype(BF16)
    r_t = (r * e_pos).astype(BF16)
    k_t = (k * e_neg).astype(BF16)
    v_t = v.astype(BF16)

    lane_head = lax.broadcasted_iota(jnp.int32, (CHUNK, GROUP_W), 1) // HEAD_DIM
    head_masks = [lane_head == h for h in range(GROUP_HEADS)]
    rt = lax.broadcasted_iota(jnp.int32, (GROUP_W, GROUP_W), 0) % CHUNK
    ct = lax.broadcasted_iota(jnp.int32, (GROUP_W, GROUP_W), 1) % CHUNK
    strict = rt > ct
    incl = rt >= ct
    rows = lambda s, ci: slice((s * n_chunks + ci) * CHUNK, (s * n_chunks + ci + 1) * CHUNK)
    cols = lambda g: slice(g * GROUP_W, (g + 1) * GROUP_W)

    def stack(x, u):
        x = x[rows(u[0], u[1]), cols(u[2])]
        zero = jnp.zeros_like(x)
        return jnp.concatenate([jnp.where(mk, x, zero) for mk in head_masks], axis=0)

    units = [(s, ci, g) for ci in range(n_chunks) for s in range(n_seq) for g in range(n_groups)]
    a_s = {u: stack(a_t, u) for u in units}
    b_s = {u: stack(b_t, u) for u in units}
    r_s = {u: stack(r_t, u) for u in units}
    k_s = {u: stack(k_t, u) for u in units}
    v_s = {u: stack(v_t, u) for u in units}
    l_all = {u: _dot_nt(jnp.concatenate([a_s[u], r_s[u]], axis=0),
                        jnp.concatenate([b_s[u], k_s[u]], axis=0)) for u in units}
    l_ab = {u: jnp.where(strict, l_all[u][0:GROUP_W, 0:GROUP_W], 0.0) for u in units}
    l_ak = {u: jnp.where(strict, l_all[u][0:GROUP_W, GROUP_W:], 0.0).astype(BF16) for u in units}
    l_rb = {u: jnp.where(incl, l_all[u][GROUP_W:, 0:GROUP_W], 0.0).astype(BF16) for u in units}
    l_rk = {u: jnp.where(incl, l_all[u][GROUP_W:, GROUP_W:], 0.0).astype(BF16) for u in units}
    akv = {u: _dot(l_ak[u], v_s[u]).astype(BF16) for u in units}
    y_rk = {u: _dot(l_rk[u], v_s[u]) for u in units}
    t_inv = _unit_lower_inverse([l_ab[u] for u in units])
    wu = {u: _dot(t_inv[n].astype(BF16), jnp.concatenate([a_s[u], akv[u]], axis=1))
          for n, u in enumerate(units)}

    state = {(s, g): s_scr[s * n_groups + g] for s in range(n_seq) for g in range(n_groups)}
    for u in units:
        s, ci, g = u
        st = state[s, g]
        s_b = st.astype(BF16)
        z_b = (_dot_nt(wu[u][:, 0:GROUP_W].astype(BF16), s_b) + wu[u][:, GROUP_W:]).astype(BF16)
        y_s = (_dot_nt(r_s[u], s_b) + _dot(l_rb[u], z_b)) + y_rk[u]
        y_scr[rows(s, ci), cols(g)] = ((y_s[0:CHUNK] + y_s[CHUNK:2 * CHUNK])
                                       + (y_s[2 * CHUNK:3 * CHUNK] + y_s[3 * CHUNK:]))
        last = (s * n_chunks + ci + 1) * CHUNK - 1
        gc = e_pos[last:last + 1, cols(g)]
        bh_s = (b_s[u].astype(F32) * gc).astype(BF16)
        kh_s = (k_s[u].astype(F32) * gc).astype(BF16)
        state[s, g] = st * gc + _dot_tn(jnp.concatenate([z_b, v_s[u]], axis=0),
                                        jnp.concatenate([bh_s, kh_s], axis=0))
    for s in range(n_seq):
        for g in range(n_groups):
            s_scr[s * n_groups + g] = state[s, g]

    y = y_scr[...]
    d = y - segsum(y) * (1.0 / HEAD_DIM)
    var = segsum(d * d) * (1.0 / HEAD_DIM)
    yn = d * lax.rsqrt(var + LNX_EPS) * lnw_ref[...] + lnb_ref[...]
    o_ref[...] = ((yn + bonus) * gate).astype(o_ref.dtype).reshape(n_seq, tc, width)


def _rwkv(rw, mu, w0, a0, k_k, k_a, r_k, ln_w, ln_b, w_up, a_up, g_up, ones_blk, tri_blk, *,
          width, tc, n_seq):
    B, T, rw_w = rw.shape
    n_groups = width // GROUP_W
    kern = functools.partial(_rwkv_kernel, width=width)
    sub = 8
    small = [mu, w0, a0, k_k, k_a, r_k, ln_w, ln_b, w_up, a_up, g_up, ones_blk, tri_blk]
    return pl.pallas_call(
        kern,
        grid=(B // n_seq, T // tc),
        in_specs=[
            pl.BlockSpec((n_seq, tc, rw_w), lambda b, t: (b, t, 0)),
            pl.BlockSpec((n_seq, sub, rw_w), lambda b, t: (b, jnp.maximum(t * (tc // sub) - 1, 0), 0)),
        ] + [_const_spec(a.shape) for a in small],
        out_specs=pl.BlockSpec((n_seq, tc, width), lambda b, t: (b, t, 0)),
        out_shape=jax.ShapeDtypeStruct((B, T, width), BF16),
        scratch_shapes=[
            pltpu.VMEM((n_seq * n_groups, GROUP_W, GROUP_W), F32),
            pltpu.VMEM((n_seq * tc, width), F32),
        ],
        compiler_params=pltpu.CompilerParams(
            dimension_semantics=("arbitrary", "arbitrary"), vmem_limit_bytes=VMEM_LIMIT),
        name="rwkv",
    )(rw, rw, *small)


def _outffn_kernel(x_ref, of_ref, or_ref, wof_ref, wor_ref, gpost_ref, gfpre_ref, gfpost_ref,
                   wg_ref, wu_ref, wd_ref, o_ref, *, n_sub):
    tm = x_ref.shape[0]
    sub = tm // n_sub
    parts = [slice(i * sub, (i + 1) * sub) for i in range(n_sub)]
    mix = [_dot(of_ref[sl, :], wof_ref[...]) + _dot(or_ref[sl, :], wor_ref[...]) for sl in parts]
    h = [x_ref[sl, :] + _rms(m) * gpost_ref[...] for sl, m in zip(parts, mix)]
    z = [(_rms(hh) * gfpre_ref[...]).astype(BF16) for hh in h]
    gate = [_dot(zz, wg_ref[...]) for zz in z]
    up = [_dot(zz, wu_ref[...]) for zz in z]
    act = [(g * _sigmoid(g) * u).astype(BF16) for g, u in zip(gate, up)]
    f = [_dot(a, wd_ref[...]) for a in act]
    for sl, hh, ff in zip(parts, h, f):
        o_ref[sl, :] = hh + _rms(ff) * gfpost_ref[...]


def _outffn(x2, o_fox, o_rw, wof, wor, g_post, g_fpre, g_fpost, wg, wu, wd, *, tm, n_sub):
    N, D = x2.shape
    consts = [wof, wor, g_post, g_fpre, g_fpost, wg, wu, wd]
    return pl.pallas_call(
        functools.partial(_outffn_kernel, n_sub=n_sub),
        grid=(N // tm,),
        in_specs=[
            pl.BlockSpec((tm, D), lambda i: (i, 0)),
            pl.BlockSpec((tm, o_fox.shape[1]), lambda i: (i, 0)),
            pl.BlockSpec((tm, o_rw.shape[1]), lambda i: (i, 0)),
        ] + [_const_spec(a.shape) for a in consts],
        out_specs=pl.BlockSpec((tm, D), lambda i: (i, 0)),
        out_shape=jax.ShapeDtypeStruct((N, D), F32),
        compiler_params=pltpu.CompilerParams(
            dimension_semantics=("arbitrary",), vmem_limit_bytes=VMEM_LIMIT),
        name="outffn",
    )(x2, o_fox, o_rw, *consts)


def _pad_cols(a, n):
    return jnp.pad(a, ((0, 0), (0, n - a.shape[1])))


def _pad_rows(a, n):
    return jnp.pad(a, ((0, n - a.shape[0]), (0, 0)))


def _block_layer(h, attn_norm_pre, attn_norm_post, w_in, fox_forget_bias, shift_mu, rwkv_w0,
                 rwkv_w_up, rwkv_a0, rwkv_a_up, rwkv_g_up, rwkv_k_k, rwkv_k_a, rwkv_r_k,
                 rwkv_ln_w, rwkv_ln_b, w_out, ffn_norm_pre, ffn_norm_post, ffn_w_gate, ffn_w_up,
                 ffn_w_down):
    B, T, D = h.shape
    fw = D // 2
    rwid = D // 2
    nh = fw // HEAD_DIM
    hw = nh * LANES

    fq, fk, fv = w_in[:, 0:fw], w_in[:, fw:2 * fw], w_in[:, 2 * fw:3 * fw]
    ffw = w_in[:, 3 * fw:3 * fw + nh]
    rcol = 3 * fw + nh

    def rw_layout(a, spare):
        o = 3 * rwid
        return jnp.concatenate([
            a[:, 0:o],
            _pad_cols(jnp.concatenate([a[:, o:o + DECAY_LORA], spare], axis=1), LANES),
            _pad_cols(a[:, o + DECAY_LORA:o + DECAY_LORA + AAA_LORA], LANES),
            _pad_cols(a[:, o + DECAY_LORA + AAA_LORA:], 2 * LANES)], axis=1)

    w_rw = rw_layout(w_in[:, rcol:], ffw)
    rw_w = w_rw.shape[1]
    wp = jnp.concatenate([fq, fk, fv, w_rw], axis=1).astype(BF16)
    mu = rw_layout(shift_mu[None, :], jnp.zeros((1, nh), F32))
    fbias = _pad_cols(jnp.pad(fox_forget_bias[None, :], ((0, 0), (FORGET_LANE, 0))), LANES)

    tm1 = 512
    ii = jnp.arange(tm1)
    tri = (ii[:, None] >= ii[None, :]).astype(BF16)
    bias_lane = lambda hd, p: hd * LANES + (HEAD_DIM if hd % 2 == 0 else 0) + 3 + p
    sel = np.zeros((LANES, hw), np.float32)
    qconst = np.zeros((1, hw), np.float32)
    for hd in range(nh):
        for p in range(3):
            sel[FORGET_LANE + hd + 8 * p, bias_lane(hd, p)] = 1.0
            qconst[0, bias_lane(hd, p)] = 1.0
    sel = jnp.asarray(sel, BF16)
    qconst = jnp.asarray(qconst)

    q_aug, k_aug, v, rw = _inproj(h, attn_norm_pre[None, :], wp, fbias, tri, sel, qconst,
                                  n_heads=nh, rw_w=rw_w, f_col=3 * rwid, tm=tm1)

    o_fox = _fox(q_aug, k_aug, v, tq=512)

    tc = 256
    hh = jnp.arange(rwid) // HEAD_DIM
    ones_blk = (hh[:, None] == hh[None, :]).astype(BF16)
    jj = jnp.arange(tc)
    tri_blk = ((jj[:, None] >= jj[None, :]) & (jj[:, None] // CHUNK == jj[None, :] // CHUNK)).astype(BF16)
    row = lambda a: a.reshape(1, -1)
    o_rw = _rwkv(rw, mu, row(rwkv_w0), row(rwkv_a0), row(rwkv_k_k), row(rwkv_k_a), row(rwkv_r_k),
                 row(rwkv_ln_w), row(rwkv_ln_b),
                 _pad_rows(rwkv_w_up, LANES).astype(BF16), _pad_rows(rwkv_a_up, LANES).astype(BF16),
                 _pad_rows(rwkv_g_up, 2 * LANES).astype(BF16), ones_blk, tri_blk,
                 width=rwid, tc=tc, n_seq=2)

    out = _outffn(h.reshape(B * T, D), o_fox.reshape(B * T, fw), o_rw.reshape(B * T, rwid),
                  w_out[0:fw].astype(BF16), w_out[fw:].astype(BF16),
                  row(attn_norm_post), row(ffn_norm_pre), row(ffn_norm_post),
                  ffn_w_gate.astype(BF16), ffn_w_up.astype(BF16), ffn_w_down.astype(BF16),
                  tm=512, n_sub=2)
    return out.reshape(B, T, D)


def kernel(x, attn_norm_pre, attn_norm_post, w_in, fox_forget_bias, shift_mu, rwkv_w0, rwkv_w_up,
           rwkv_a0, rwkv_a_up, rwkv_g_up, rwkv_k_k, rwkv_k_a, rwkv_r_k, rwkv_ln_w, rwkv_ln_b, w_out,
           ffn_norm_pre, ffn_norm_post, ffn_w_gate, ffn_w_up, ffn_w_down):
    h = x
    for l in range(attn_norm_pre.shape[0]):
        h = _block_layer(h, attn_norm_pre[l], attn_norm_post[l], w_in[l], fox_forget_bias[l],
                         shift_mu[l], rwkv_w0[l], rwkv_w_up[l], rwkv_a0[l], rwkv_a_up[l],
                         rwkv_g_up[l], rwkv_k_k[l], rwkv_k_a[l], rwkv_r_k[l], rwkv_ln_w[l],
                         rwkv_ln_b[l], w_out[l], ffn_norm_pre[l], ffn_norm_post[l], ffn_w_gate[l],
                         ffn_w_up[l], ffn_w_down[l])
    return h
```

```python
import functools

import jax
import jax.numpy as jnp
import numpy as np
from jax import lax
from jax.experimental import pallas as pl
from jax.experimental.pallas import tpu as pltpu

F32 = jnp.float32
BF16 = jnp.bfloat16

HEAD_DIM = 64
LANES = 128
NORM_EPS = 1e-6
LNX_EPS = 64e-5
LOG2E = 1.4426950408889634
DECAY_LORA = 64
AAA_LORA = 64
GATE_LORA = 160
CHUNK = 64
GROUP_HEADS = 4
GROUP_W = GROUP_HEADS * HEAD_DIM
FORGET_LANE = DECAY_LORA
VMEM_LIMIT = 56 * 1024 * 1024


def _dot(a, b):
    return jnp.dot(a, b, preferred_element_type=F32)


def _dot_nt(a, b):
    return lax.dot_general(a, b, (((1,), (1,)), ((), ())), preferred_element_type=F32)


def _dot_tn(a, b):
    return lax.dot_general(a, b, (((0,), (0,)), ((), ())), preferred_element_type=F32)


def _split2(x):
    hi = x.astype(BF16)
    lo = (x - hi.astype(F32)).astype(BF16)
    return hi, lo


def _split3(x):
    hi = x.astype(BF16)
    r = x - hi.astype(F32)
    mid = r.astype(BF16)
    lo = (r - mid.astype(F32)).astype(BF16)
    return hi, mid, lo


def _dot_exact_lhs01(ones01, x):
    hi, mid, lo = _split3(x)
    return _dot(ones01, hi) + (_dot(ones01, mid) + _dot(ones01, lo))


def _rms(x):
    return x * lax.rsqrt(jnp.mean(x * x, axis=-1, keepdims=True) + NORM_EPS)


def _softplus(x):
    return jnp.maximum(x, 0.0) + jnp.log1p(jnp.exp(-jnp.abs(x)))


def _sigmoid(x):
    return 1.0 / (1.0 + jnp.exp(-x))


def _const_spec(shape):
    nd = len(shape)
    return pl.BlockSpec(shape, lambda *_: (0,) * nd, pipeline_mode=pl.Buffered(1))


def _inproj_kernel(x_ref, g_ref, w_ref, fb_ref, tri_ref, sel_ref, qc_ref, mu_ref,
                   q_out, k_out, v_out, rw_out, carry_ref, rwlast_ref, *, n_heads, rw_w, f_col):
    t = pl.program_id(1)

    @pl.when(t == 0)
    def _():
        carry_ref[...] = jnp.zeros_like(carry_ref)
        rwlast_ref[...] = jnp.zeros_like(rwlast_ref)

    fw = n_heads * HEAD_DIM
    tm = x_ref.shape[1]
    u = (_rms(x_ref[0]) * g_ref[...]).astype(BF16)
    lane = lax.broadcasted_iota(jnp.int32, (tm, LANES), 1)
    lower = lane < HEAD_DIM

    rw = _dot(u, w_ref[:, 3 * fw:3 * fw + rw_w])
    row0 = lax.broadcasted_iota(jnp.int32, (tm, 1), 0) == 0
    rw_prev = jnp.where(row0, rwlast_ref[...], pltpu.roll(rw, 1, 0))
    rwlast_ref[...] = rw[tm - 1:tm, :]
    rw_out[0] = rw + mu_ref[...] * (rw_prev - rw)

    fo = f_col
    in_f = (lane >= FORGET_LANE) & (lane < FORGET_LANE + n_heads)

    def pack3(x):
        hi = x.astype(BF16).astype(F32)
        r1 = x - hi
        mid = r1.astype(BF16).astype(F32)
        lo = r1 - mid
        keep = lambda a: jnp.where(in_f, a, 0.0)
        return (keep(hi) + pltpu.roll(keep(mid), 8, 1) + pltpu.roll(keep(lo), 16, 1)).astype(BF16)

    def unpack3(x):
        return x + pltpu.roll(x, LANES - 8, 1) + pltpu.roll(x, LANES - 16, 1)

    logf = -_softplus(-(rw[:, fo:fo + LANES] + fb_ref[...]))
    c = unpack3(_dot(tri_ref[...], pack3(logf))) + carry_ref[...]
    carry_ref[...] = c[tm - 1:tm, :]

    qk = _dot(u, w_ref[:, 0:2 * fw])
    k_bias = _dot(pack3(c * (-LOG2E)), sel_ref[...])
    scale = HEAD_DIM ** -0.5 * LOG2E
    for hp in range(n_heads // 2):
        qb = qk[:, hp * LANES:(hp + 1) * LANES] * scale
        kb = qk[:, fw + hp * LANES:fw + (hp + 1) * LANES]
        for h, keep in ((2 * hp, lower), (2 * hp + 1, jnp.logical_not(lower))):
            q_out[0, h] = jnp.where(keep, qb, qc_ref[:, h * LANES:(h + 1) * LANES]).astype(BF16)
            k_out[0, h] = jnp.where(keep, kb, k_bias[:, h * LANES:(h + 1) * LANES]).astype(BF16)

    v = _dot(u, w_ref[:, 2 * fw:3 * fw])
    one_even = jnp.where(lane == HEAD_DIM, 1.0, 0.0)
    one_odd = jnp.where(lane == 0, 1.0, 0.0)
    for hp in range(n_heads // 2):
        blk = v[:, hp * LANES:(hp + 1) * LANES]
        v_out[0, 2 * hp] = jnp.where(lower, blk, one_even).astype(BF16)
        v_out[0, 2 * hp + 1] = jnp.where(lower, one_odd, blk).astype(BF16)


def _inproj(x, g_pre, wp, fbias, tri, sel, qconst, mu, *, n_heads, rw_w, f_col, tm):
    B, T, D = x.shape
    assert T % tm == 0 and n_heads % 2 == 0 and tm == tri.shape[0]
    kern = functools.partial(_inproj_kernel, n_heads=n_heads, rw_w=rw_w, f_col=f_col)
    return pl.pallas_call(
        kern,
        grid=(B, T // tm),
        in_specs=[
            pl.BlockSpec((1, tm, D), lambda b, t: (b, t, 0)),
            _const_spec(g_pre.shape),
            _const_spec(wp.shape),
            _const_spec(fbias.shape),
            _const_spec(tri.shape),
            _const_spec(sel.shape),
            _const_spec(qconst.shape),
            _const_spec(mu.shape),
        ],
        out_specs=[
            pl.BlockSpec((1, n_heads, tm, LANES), lambda b, t: (b, 0, t, 0)),
            pl.BlockSpec((1, n_heads, tm, LANES), lambda b, t: (b, 0, t, 0)),
            pl.BlockSpec((1, n_heads, tm, LANES), lambda b, t: (b, 0, t, 0)),
            pl.BlockSpec((1, tm, rw_w), lambda b, t: (b, t, 0)),
        ],
        out_shape=[
            jax.ShapeDtypeStruct((B, n_heads, T, LANES), BF16),
            jax.ShapeDtypeStruct((B, n_heads, T, LANES), BF16),
            jax.ShapeDtypeStruct((B, n_heads, T, LANES), BF16),
            jax.ShapeDtypeStruct((B, T, rw_w), F32),
        ],
        scratch_shapes=[pltpu.VMEM((1, LANES), F32),
                        pltpu.VMEM((1, rw_w), F32)],
        compiler_params=pltpu.CompilerParams(
            dimension_semantics=("arbitrary", "arbitrary"), vmem_limit_bytes=VMEM_LIMIT),
        name="inproj",
    )(x, g_pre, wp, fbias, tri, sel, qconst, mu)


def _fox_kernel(q_ref, k_ref, v_ref, o_ref, m_scr, acc_scr, *, tq):
    qi = pl.program_id(2)
    heads = range(2)

    def step(start, tk, masked):
        rows = pl.ds(pl.multiple_of(start, tk), tk)
        s = [_dot_nt(q_ref[0, h], k_ref[0, h, rows, :]) for h in heads]
        if masked:
            row = lax.broadcasted_iota(jnp.int32, (tq, tk), 0)
            col = lax.broadcasted_iota(jnp.int32, (tq, tk), 1)
            s = [jnp.where(col <= row + (tk - tq), sh, -jnp.inf) for sh in s]
        m_prev = [m_scr[h] for h in heads]
        m_next = [jnp.maximum(m_prev[h], jnp.max(s[h], axis=1, keepdims=True)) for h in heads]
        p = [jnp.exp2(s[h] - jnp.concatenate([m_next[h]] * (tk // LANES), axis=1)).astype(BF16)
             for h in heads]
        for h in heads:
            alpha = jnp.exp2(m_prev[h] - m_next[h])
            acc_scr[h] = alpha * acc_scr[h] + _dot(p[h], v_ref[0, h, rows, :])
            m_scr[h] = m_next[h]

    for h in heads:
        m_scr[h] = jnp.full(m_scr.shape[1:], -jnp.inf, F32)
        acc_scr[h] = jnp.zeros(acc_scr.shape[1:], F32)

    def body(j, carry):
        step(j * (2 * tq), 2 * tq, masked=False)
        return carry

    lax.fori_loop(0, qi // 2, body, 0)

    @pl.when(qi % 2 == 1)
    def _():
        step((qi - 1) * tq, 2 * tq, masked=True)

    @pl.when(qi % 2 == 0)
    def _():
        step(qi * tq, tq, masked=True)

    lane = lax.broadcasted_iota(jnp.int32, acc_scr.shape[1:], 1)
    acc0 = acc_scr[0]
    acc1 = acc_scr[1]
    o = jnp.where(lane < HEAD_DIM, acc0 / acc0[:, HEAD_DIM:HEAD_DIM + 1], acc1 / acc1[:, 0:1])
    o_ref[0] = o.astype(o_ref.dtype)


def _fox(q_aug, k_aug, v_aug, *, tq):
    B, H, T, _ = q_aug.shape
    assert T % tq == 0 and H % 2 == 0
    kern = functools.partial(_fox_kernel, tq=tq)
    return pl.pallas_call(
        kern,
        grid=(B, H // 2, T // tq),
        in_specs=[
            pl.BlockSpec((1, 2, tq, LANES), lambda b, hp, qi: (b, hp, qi, 0)),
            pl.BlockSpec((1, 2, T, LANES), lambda b, hp, qi: (b, hp, 0, 0)),
            pl.BlockSpec((1, 2, T, LANES), lambda b, hp, qi: (b, hp, 0, 0)),
        ],
        out_specs=pl.BlockSpec((1, tq, LANES), lambda b, hp, qi: (b, qi, hp)),
        out_shape=jax.ShapeDtypeStruct((B, T, H * HEAD_DIM), BF16),
        scratch_shapes=[pltpu.VMEM((2, tq, LANES), F32)] * 2,
        compiler_params=pltpu.CompilerParams(
            dimension_semantics=("arbitrary", "arbitrary", "arbitrary"),
            vmem_limit_bytes=VMEM_LIMIT),
        name="fox",
    )(q_aug, k_aug, v_aug)


def _unit_lower_inverse(Ls):
    n = Ls[0].shape[0]
    r = lax.broadcasted_iota(jnp.int32, (n, n), 0)
    c = lax.broadcasted_iota(jnp.int32, (n, n), 1)
    same = (r >> 1) == (c >> 1)
    eye = jnp.where(r == c, 1.0, 0.0)
    ps = [eye + jnp.where(same, L, 0.0) for L in Ls]
    for s in range(2, 7):
        merged = (r >> s) == (c >> s)
        sel = merged & jnp.logical_not(same)
        offs = [jnp.where(sel, L, 0.0).astype(BF16) for L in Ls]
        pbs = [p.astype(BF16) for p in ps]
        tmps = [_dot(pb, off).astype(BF16) for pb, off in zip(pbs, offs)]
        ps = [p + _dot(tmp, pb) for p, tmp, pb in zip(ps, tmps, pbs)]
        same = merged
    return ps


def _rwkv_kernel(rw_ref, w0_ref, a0_ref, kk_ref, ka_ref, rk_ref,
                 lnw_ref, lnb_ref, wup_ref, aup_ref, gup_ref, ones_ref, tri_ref,
                 o_ref, s_scr, y_scr, *, width):
    t = pl.program_id(1)
    n_seq, tc, rw_w = rw_ref.shape
    n_rows = n_seq * tc
    n_chunks = tc // CHUNK
    n_groups = width // GROUP_W

    @pl.when(t == 0)
    def _():
        s_scr[...] = jnp.zeros_like(s_scr)

    ones_blk = ones_ref[...]

    def segsum(x):
        hi, lo = _split2(x)
        return _dot(hi, ones_blk) + _dot(lo, ones_blk)

    p = rw_ref[...].reshape(n_rows, rw_w)
    r = p[:, 0:width]
    k = p[:, width:2 * width]
    v = p[:, 2 * width:3 * width]
    o = 3 * width
    w_lat = p[:, o:o + LANES]
    a_lat = p[:, o + LANES:o + 2 * LANES]
    g_lat = p[:, o + 2 * LANES:o + 4 * LANES]

    z = w0_ref[...] + _dot(jnp.tanh(w_lat).astype(BF16), wup_ref[...])
    g_log = _sigmoid(z) * (-(2.718281828459045 ** -0.5))
    lr = _sigmoid(a0_ref[...] + _dot(a_lat.astype(BF16), aup_ref[...]))
    gate = _dot(_sigmoid(g_lat).astype(BF16), gup_ref[...])
    kk = k * kk_ref[...]
    kk = kk * lax.rsqrt(jnp.maximum(segsum(kk * kk), 1e-24))
    k = k * (1.0 + (lr - 1.0) * ka_ref[...])
    bonus = segsum(r * k * rk_ref[...]) * v

    tri = tri_ref[...]
    G = jnp.concatenate([_dot_exact_lhs01(tri, g_log[s * tc:(s + 1) * tc]) for s in range(n_seq)], axis=0)
    e_pos = jnp.exp(G)
    e_neg = jnp.exp(-G)
    a_t = (-kk * jnp.exp(G - g_log)).astype(BF16)
    b_t = (kk * lr * e_neg).astype(BF16)
    r_t = (r * e_pos).astype(BF16)
    k_t = (k * e_neg).astype(BF16)
    v_t = v.astype(BF16)

    lane_head = lax.broadcasted_iota(jnp.int32, (CHUNK, GROUP_W), 1) // HEAD_DIM
    head_masks = [lane_head == h for h in range(GROUP_HEADS)]
    rt = lax.broadcasted_iota(jnp.int32, (GROUP_W, GROUP_W), 0) % CHUNK
    ct = lax.broadcasted_iota(jnp.int32, (GROUP_W, GROUP_W), 1) % CHUNK
    strict = rt > ct
    incl = rt >= ct
    rows = lambda s, ci: slice((s * n_chunks + ci) * CHUNK, (s * n_chunks + ci + 1) * CHUNK)
    cols = lambda g: slice(g * GROUP_W, (g + 1) * GROUP_W)

    def stack(x, u):
        x = x[rows(u[0], u[1]), cols(u[2])]
        zero = jnp.zeros_like(x)
        return jnp.concatenate([jnp.where(mk, x, zero) for mk in head_masks], axis=0)

    units = [(s, ci, g) for ci in range(n_chunks) for s in range(n_seq) for g in range(n_groups)]
    a_s = {u: stack(a_t, u) for u in units}
    b_s = {u: stack(b_t, u) for u in units}
    r_s = {u: stack(r_t, u) for u in units}
    k_s = {u: stack(k_t, u) for u in units}
    v_s = {u: stack(v_t, u) for u in units}
    l_all = {u: _dot_nt(jnp.concatenate([a_s[u], r_s[u]], axis=0),
                        jnp.concatenate([b_s[u], k_s[u]], axis=0)) for u in units}
    l_ab = {u: jnp.where(strict, l_all[u][0:GROUP_W, 0:GROUP_W], 0.0) for u in units}
    l_ak = {u: jnp.where(strict, l_all[u][0:GROUP_W, GROUP_W:], 0.0).astype(BF16) for u in units}
    l_rb = {u: jnp.where(incl, l_all[u][GROUP_W:, 0:GROUP_W], 0.0).astype(BF16) for u in units}
    l_rk = {u: jnp.where(incl, l_all[u][GROUP_W:, GROUP_W:], 0.0).astype(BF16) for u in units}
    akv = {u: _dot(l_ak[u], v_s[u]).astype(BF16) for u in units}
    y_rk = {u: _dot(l_rk[u], v_s[u]) for u in units}
    t_inv = _unit_lower_inverse([l_ab[u] for u in units])
    wu = {u: _dot(t_inv[n].astype(BF16), jnp.concatenate([a_s[u], akv[u]], axis=1))
          for n, u in enumerate(units)}

    state = {(s, g): s_scr[s * n_groups + g] for s in range(n_seq) for g in range(n_groups)}
    for u in units:
        s, ci, g = u
        st = state[s, g]
        s_b = st.astype(BF16)
        z_b = (_dot_nt(wu[u][:, 0:GROUP_W].astype(BF16), s_b) + wu[u][:, GROUP_W:]).astype(BF16)
        y_s = (_dot_nt(r_s[u], s_b) + _dot(l_rb[u], z_b)) + y_rk[u]
        y_scr[rows(s, ci), cols(g)] = ((y_s[0:CHUNK] + y_s[CHUNK:2 * CHUNK])
                                       + (y_s[2 * CHUNK:3 * CHUNK] + y_s[3 * CHUNK:]))
        last = (s * n_chunks + ci + 1) * CHUNK - 1
        gc = e_pos[last:last + 1, cols(g)]
        bh_s = (b_s[u].astype(F32) * gc).astype(BF16)
        kh_s = (k_s[u].astype(F32) * gc).astype(BF16)
        state[s, g] = st * gc + _dot_tn(jnp.concatenate([z_b, v_s[u]], axis=0),
                                        jnp.concatenate([bh_s, kh_s], axis=0))
    for s in range(n_seq):
        for g in range(n_groups):
            s_scr[s * n_groups + g] = state[s, g]

    y = y_scr[...]
    d = y - segsum(y) * (1.0 / HEAD_DIM)
    var = segsum(d * d) * (1.0 / HEAD_DIM)
    yn = d * lax.rsqrt(var + LNX_EPS) * lnw_ref[...] + lnb_ref[...]
    o_ref[...] = ((yn + bonus) * gate).astype(o_ref.dtype).reshape(n_seq, tc, width)


def _rwkv(rw, w0, a0, k_k, k_a, r_k, ln_w, ln_b, w_up, a_up, g_up, ones_blk, tri_blk, *,
          width, tc, n_seq):
    B, T, rw_w = rw.shape
    assert B % n_seq == 0 and T % tc == 0 and tc % CHUNK == 0 and width % GROUP_W == 0
    n_groups = width // GROUP_W
    kern = functools.partial(_rwkv_kernel, width=width)
    small = [w0, a0, k_k, k_a, r_k, ln_w, ln_b, w_up, a_up, g_up, ones_blk, tri_blk]
    return pl.pallas_call(
        kern,
        grid=(B // n_seq, T // tc),
        in_specs=[
            pl.BlockSpec((n_seq, tc, rw_w), lambda b, t: (b, t, 0)),
        ] + [_const_spec(a.shape) for a in small],
        out_specs=pl.BlockSpec((n_seq, tc, width), lambda b, t: (b, t, 0)),
        out_shape=jax.ShapeDtypeStruct((B, T, width), BF16),
        scratch_shapes=[
            pltpu.VMEM((n_seq * n_groups, GROUP_W, GROUP_W), F32),
            pltpu.VMEM((n_seq * tc, width), F32),
        ],
        compiler_params=pltpu.CompilerParams(
            dimension_semantics=("arbitrary", "arbitrary"), vmem_limit_bytes=VMEM_LIMIT),
        name="rwkv",
    )(rw, *small)


def _outffn_kernel(x_ref, of_ref, or_ref, wof_ref, wor_ref, gpost_ref, gfpre_ref, gfpost_ref,
                   wg_ref, wu_ref, wd_ref, o_ref, *, n_sub):
    tm = x_ref.shape[0]
    sub = tm // n_sub
    parts = [slice(i * sub, (i + 1) * sub) for i in range(n_sub)]
    mix = [_dot(of_ref[sl, :], wof_ref[...]) + _dot(or_ref[sl, :], wor_ref[...]) for sl in parts]
    h = [x_ref[sl, :] + _rms(m) * gpost_ref[...] for sl, m in zip(parts, mix)]
    z = [(_rms(hh) * gfpre_ref[...]).astype(BF16) for hh in h]
    gate = [_dot(zz, wg_ref[...]) for zz in z]
    up = [_dot(zz, wu_ref[...]) for zz in z]
    act = [(g * _sigmoid(g) * u).astype(BF16) for g, u in zip(gate, up)]
    f = [_dot(a, wd_ref[...]) for a in act]
    for sl, hh, ff in zip(parts, h, f):
        o_ref[sl, :] = hh + _rms(ff) * gfpost_ref[...]


def _outffn(x2, o_fox, o_rw, wof, wor, g_post, g_fpre, g_fpost, wg, wu, wd, *, tm, n_sub):
    N, D = x2.shape
    assert N % tm == 0 and tm % n_sub == 0
    consts = [wof, wor, g_post, g_fpre, g_fpost, wg, wu, wd]
    return pl.pallas_call(
        functools.partial(_outffn_kernel, n_sub=n_sub),
        grid=(N // tm,),
        in_specs=[
            pl.BlockSpec((tm, D), lambda i: (i, 0)),
            pl.BlockSpec((tm, o_fox.shape[1]), lambda i: (i, 0)),
            pl.BlockSpec((tm, o_rw.shape[1]), lambda i: (i, 0)),
        ] + [_const_spec(a.shape) for a in consts],
        out_specs=pl.BlockSpec((tm, D), lambda i: (i, 0)),
        out_shape=jax.ShapeDtypeStruct((N, D), F32),
        compiler_params=pltpu.CompilerParams(
            dimension_semantics=("arbitrary",), vmem_limit_bytes=VMEM_LIMIT),
        name="outffn",
    )(x2, o_fox, o_rw, *consts)


def _pad_cols(a, n):
    return jnp.pad(a, ((0, 0), (0, n - a.shape[1])))


def _pad_rows(a, n):
    return jnp.pad(a, ((0, n - a.shape[0]), (0, 0)))


def _block_layer(h, attn_norm_pre, attn_norm_post, w_in, fox_forget_bias, shift_mu, rwkv_w0,
                 rwkv_w_up, rwkv_a0, rwkv_a_up, rwkv_g_up, rwkv_k_k, rwkv_k_a, rwkv_r_k,
                 rwkv_ln_w, rwkv_ln_b, w_out, ffn_norm_pre, ffn_norm_post, ffn_w_gate, ffn_w_up,
                 ffn_w_down):
    B, T, D = h.shape
    fw = D // 2
    rwid = D // 2
    nh = fw // HEAD_DIM
    hw = nh * LANES

    fq, fk, fv = w_in[:, 0:fw], w_in[:, fw:2 * fw], w_in[:, 2 * fw:3 * fw]
    ffw = w_in[:, 3 * fw:3 * fw + nh]
    rcol = 3 * fw + nh

    def rw_layout(a, spare):
        o = 3 * rwid
        return jnp.concatenate([
            a[:, 0:o],
            _pad_cols(jnp.concatenate([a[:, o:o + DECAY_LORA], spare], axis=1), LANES),
            _pad_cols(a[:, o + DECAY_LORA:o + DECAY_LORA + AAA_LORA], LANES),
            _pad_cols(a[:, o + DECAY_LORA + AAA_LORA:], 2 * LANES)], axis=1)

    w_rw = rw_layout(w_in[:, rcol:], ffw)
    rw_w = w_rw.shape[1]
    wp = jnp.concatenate([fq, fk, fv, w_rw], axis=1).astype(BF16)
    mu = rw_layout(shift_mu[None, :], jnp.zeros((1, nh), F32))
    fbias = _pad_cols(jnp.pad(fox_forget_bias[None, :], ((0, 0), (FORGET_LANE, 0))), LANES)

    tm1 = 512
    ii = jnp.arange(tm1)
    tri = (ii[:, None] >= ii[None, :]).astype(BF16)
    bias_lane = lambda hd, p: hd * LANES + (HEAD_DIM if hd % 2 == 0 else 0) + 3 + p
    sel = np.zeros((LANES, hw), np.float32)
    qconst = np.zeros((1, hw), np.float32)
    for hd in range(nh):
        for p in range(3):
            sel[FORGET_LANE + hd + 8 * p, bias_lane(hd, p)] = 1.0
            qconst[0, bias_lane(hd, p)] = 1.0
    sel = jnp.asarray(sel, BF16)
    qconst = jnp.asarray(qconst)

    q_aug, k_aug, v, rw = _inproj(h, attn_norm_pre[None, :], wp, fbias, tri, sel, qconst, mu,
                                  n_heads=nh, rw_w=rw_w, f_col=3 * rwid, tm=tm1)

    o_fox = _fox(q_aug, k_aug, v, tq=512)

    tc = 256
    hh = jnp.arange(rwid) // HEAD_DIM
    ones_blk = (hh[:, None] == hh[None, :]).astype(BF16)
    jj = jnp.arange(tc)
    tri_blk = ((jj[:, None] >= jj[None, :]) & (jj[:, None] // CHUNK == jj[None, :] // CHUNK)).astype(BF16)
    row = lambda a: a.reshape(1, -1)
    o_rw = _rwkv(rw, row(rwkv_w0), row(rwkv_a0), row(rwkv_k_k), row(rwkv_k_a), row(rwkv_r_k),
                 row(rwkv_ln_w), row(rwkv_ln_b),
                 _pad_rows(rwkv_w_up, LANES).astype(BF16), _pad_rows(rwkv_a_up, LANES).astype(BF16),
                 _pad_rows(rwkv_g_up, 2 * LANES).astype(BF16), ones_blk, tri_blk,
                 width=rwid, tc=tc, n_seq=2)

    out = _outffn(h.reshape(B * T, D), o_fox.reshape(B * T, fw), o_rw.reshape(B * T, rwid),
                  w_out[0:fw].astype(BF16), w_out[fw:].astype(BF16),
                  row(attn_norm_post), row(ffn_norm_pre), row(ffn_norm_post),
                  ffn_w_gate.astype(BF16), ffn_w_up.astype(BF16), ffn_w_down.astype(BF16),
                  tm=512, n_sub=2)
    return out.reshape(B, T, D)


def kernel(x, attn_norm_pre, attn_norm_post, w_in, fox_forget_bias, shift_mu, rwkv_w0, rwkv_w_up,
           rwkv_a0, rwkv_a_up, rwkv_g_up, rwkv_k_k, rwkv_k_a, rwkv_r_k, rwkv_ln_w, rwkv_ln_b, w_out,
           ffn_norm_pre, ffn_norm_post, ffn_w_gate, ffn_w_up, ffn_w_down):
    h = x
    for l in range(attn_norm_pre.shape[0]):
        h = _block_layer(h, attn_norm_pre[l], attn_norm_post[l], w_in[l], fox_forget_bias[l],
                         shift_mu[l], rwkv_w0[l], rwkv_w_up[l], rwkv_a0[l], rwkv_a_up[l],
                         rwkv_g_up[l], rwkv_k_k[l], rwkv_k_a[l], rwkv_r_k[l], rwkv_ln_w[l],
                         rwkv_ln_b[l], w_out[l], ffn_norm_pre[l], ffn_norm_post[l], ffn_w_gate[l],
                         ffn_w_up[l], ffn_w_down[l])
    return h
```

```python
import functools

import jax
import jax.numpy as jnp
import numpy as np
from jax import lax
from jax.experimental import pallas as pl
from jax.experimental.pallas import tpu as pltpu

F32 = jnp.float32
BF16 = jnp.bfloat16

HEAD_DIM = 64
LANES = 128
NORM_EPS = 1e-6
LNX_EPS = 64e-5
LOG2E = 1.4426950408889634
DECAY_LORA = 64
AAA_LORA = 64
GATE_LORA = 160
CHUNK = 64
GROUP_HEADS = 4
GROUP_W = GROUP_HEADS * HEAD_DIM
FORGET_LANE = DECAY_LORA
VMEM_LIMIT = 56 * 1024 * 1024


def _dot(a, b):
    return jnp.dot(a, b, preferred_element_type=F32)


def _dot_nt(a, b):
    return lax.dot_general(a, b, (((1,), (1,)), ((), ())), preferred_element_type=F32)


def _dot_tn(a, b):
    return lax.dot_general(a, b, (((0,), (0,)), ((), ())), preferred_element_type=F32)


def _split2(x):
    hi = x.astype(BF16)
    lo = (x - hi.astype(F32)).astype(BF16)
    return hi, lo


def _split3(x):
    hi = x.astype(BF16)
    r = x - hi.astype(F32)
    mid = r.astype(BF16)
    lo = (r - mid.astype(F32)).astype(BF16)
    return hi, mid, lo


def _dot_exact_lhs01(ones01, x):
    hi, mid, lo = _split3(x)
    return _dot(ones01, hi) + (_dot(ones01, mid) + _dot(ones01, lo))


def _rms(x):
    return x * lax.rsqrt(jnp.mean(x * x, axis=-1, keepdims=True) + NORM_EPS)


def _softplus(x):
    return jnp.maximum(x, 0.0) + jnp.log1p(jnp.exp(-jnp.abs(x)))


def _sigmoid(x):
    return 1.0 / (1.0 + jnp.exp(-x))


def _const_spec(shape):
    nd = len(shape)
    return pl.BlockSpec(shape, lambda *_: (0,) * nd, pipeline_mode=pl.Buffered(1))


def _inproj_kernel(x_ref, g_ref, w_ref, fb_ref, tri_ref, sel_ref, qc_ref, mu_ref,
                   q_out, k_out, v_out, rw_out, carry_ref, rwlast_ref, *, n_heads, rw_w, f_col):
    t = pl.program_id(1)

    @pl.when(t == 0)
    def _():
        carry_ref[...] = jnp.zeros_like(carry_ref)
        rwlast_ref[...] = jnp.zeros_like(rwlast_ref)

    fw = n_heads * HEAD_DIM
    tm = x_ref.shape[1]
    u = (_rms(x_ref[0]) * g_ref[...]).astype(BF16)
    lane = lax.broadcasted_iota(jnp.int32, (tm, LANES), 1)
    lower = lane < HEAD_DIM

    rw = _dot(u, w_ref[:, 3 * fw:3 * fw + rw_w])
    row0 = lax.broadcasted_iota(jnp.int32, (tm, 1), 0) == 0
    rw_prev = jnp.where(row0, rwlast_ref[...], pltpu.roll(rw, 1, 0))
    rwlast_ref[...] = rw[tm - 1:tm, :]
    rw_out[0] = rw + mu_ref[...] * (rw_prev - rw)

    fo = f_col
    in_f = (lane >= FORGET_LANE) & (lane < FORGET_LANE + n_heads)

    def pack3(x):
        hi = x.astype(BF16).astype(F32)
        r1 = x - hi
        mid = r1.astype(BF16).astype(F32)
        lo = r1 - mid
        keep = lambda a: jnp.where(in_f, a, 0.0)
        return (keep(hi) + pltpu.roll(keep(mid), 8, 1) + pltpu.roll(keep(lo), 16, 1)).astype(BF16)

    def unpack3(x):
        return x + pltpu.roll(x, LANES - 8, 1) + pltpu.roll(x, LANES - 16, 1)

    logf = -_softplus(-(rw[:, fo:fo + LANES] + fb_ref[...]))
    c = unpack3(_dot(tri_ref[...], pack3(logf))) + carry_ref[...]
    carry_ref[...] = c[tm - 1:tm, :]

    qk = _dot(u, w_ref[:, 0:2 * fw])
    k_bias = _dot(pack3(c * (-LOG2E)), sel_ref[...])
    scale = HEAD_DIM ** -0.5 * LOG2E
    for hp in range(n_heads // 2):
        qb = qk[:, hp * LANES:(hp + 1) * LANES] * scale
        kb = qk[:, fw + hp * LANES:fw + (hp + 1) * LANES]
        for h, keep in ((2 * hp, lower), (2 * hp + 1, jnp.logical_not(lower))):
            q_out[0, h] = jnp.where(keep, qb, qc_ref[:, h * LANES:(h + 1) * LANES]).astype(BF16)
            k_out[0, h] = jnp.where(keep, kb, k_bias[:, h * LANES:(h + 1) * LANES]).astype(BF16)

    v = _dot(u, w_ref[:, 2 * fw:3 * fw])
    one_even = jnp.where(lane == HEAD_DIM, 1.0, 0.0)
    one_odd = jnp.where(lane == 0, 1.0, 0.0)
    for hp in range(n_heads // 2):
        blk = v[:, hp * LANES:(hp + 1) * LANES]
        v_out[0, 2 * hp] = jnp.where(lower, blk, one_even).astype(BF16)
        v_out[0, 2 * hp + 1] = jnp.where(lower, one_odd, blk).astype(BF16)


def _inproj(x, g_pre, wp, fbias, tri, sel, qconst, mu, *, n_heads, rw_w, f_col, tm):
    B, T, D = x.shape
    assert T % tm == 0 and n_heads % 2 == 0 and tm == tri.shape[0]
    kern = functools.partial(_inproj_kernel, n_heads=n_heads, rw_w=rw_w, f_col=f_col)
    return pl.pallas_call(
        kern,
        grid=(B, T // tm),
        in_specs=[
            pl.BlockSpec((1, tm, D), lambda b, t: (b, t, 0)),
            _const_spec(g_pre.shape),
            _const_spec(wp.shape),
            _const_spec(fbias.shape),
            _const_spec(tri.shape),
            _const_spec(sel.shape),
            _const_spec(qconst.shape),
            _const_spec(mu.shape),
        ],
        out_specs=[
            pl.BlockSpec((1, n_heads, tm, LANES), lambda b, t: (b, 0, t, 0)),
            pl.BlockSpec((1, n_heads, tm, LANES), lambda b, t: (b, 0, t, 0)),
            pl.BlockSpec((1, n_heads, tm, LANES), lambda b, t: (b, 0, t, 0)),
            pl.BlockSpec((1, tm, rw_w), lambda b, t: (b, t, 0)),
        ],
        out_shape=[
            jax.ShapeDtypeStruct((B, n_heads, T, LANES), BF16),
            jax.ShapeDtypeStruct((B, n_heads, T, LANES), BF16),
            jax.ShapeDtypeStruct((B, n_heads, T, LANES), BF16),
            jax.ShapeDtypeStruct((B, T, rw_w), F32),
        ],
        scratch_shapes=[pltpu.VMEM((1, LANES), F32),
                        pltpu.VMEM((1, rw_w), F32)],
        compiler_params=pltpu.CompilerParams(
            dimension_semantics=("arbitrary", "arbitrary"), vmem_limit_bytes=VMEM_LIMIT),
        name="inproj",
    )(x, g_pre, wp, fbias, tri, sel, qconst, mu)


def _fox_kernel(q_ref, k_ref, v_ref, o_ref, m_scr, acc_scr, *, tq):
    qi = pl.program_id(2)
    heads = range(2)

    def step(start, tk, masked):
        rows = pl.ds(pl.multiple_of(start, tk), tk)
        s = [_dot_nt(q_ref[0, h], k_ref[0, h, rows, :]) for h in heads]
        if masked:
            row = lax.broadcasted_iota(jnp.int32, (tq, tk), 0)
            col = lax.broadcasted_iota(jnp.int32, (tq, tk), 1)
            s = [jnp.where(col <= row + (tk - tq), sh, -jnp.inf) for sh in s]
        m_prev = [m_scr[h] for h in heads]
        m_next = [jnp.maximum(m_prev[h], jnp.max(s[h], axis=1, keepdims=True)) for h in heads]
        p = [jnp.exp2(s[h] - jnp.concatenate([m_next[h]] * (tk // LANES), axis=1)).astype(BF16)
             for h in heads]
        for h in heads:
            alpha = jnp.exp2(m_prev[h] - m_next[h])
            acc_scr[h] = alpha * acc_scr[h] + _dot(p[h], v_ref[0, h, rows, :])
            m_scr[h] = m_next[h]

    for h in heads:
        m_scr[h] = jnp.full(m_scr.shape[1:], -jnp.inf, F32)
        acc_scr[h] = jnp.zeros(acc_scr.shape[1:], F32)

    def body(j, carry):
        step(j * (2 * tq), 2 * tq, masked=False)
        return carry

    lax.fori_loop(0, qi // 2, body, 0)

    @pl.when(qi % 2 == 1)
    def _():
        step((qi - 1) * tq, 2 * tq, masked=True)

    @pl.when(qi % 2 == 0)
    def _():
        step(qi * tq, tq, masked=True)

    lane = lax.broadcasted_iota(jnp.int32, acc_scr.shape[1:], 1)
    acc0 = acc_scr[0]
    acc1 = acc_scr[1]
    o = jnp.where(lane < HEAD_DIM, acc0 / acc0[:, HEAD_DIM:HEAD_DIM + 1], acc1 / acc1[:, 0:1])
    o_ref[0] = o.astype(o_ref.dtype)


def _fox(q_aug, k_aug, v_aug, *, tq):
    B, H, T, _ = q_aug.shape
    assert T % tq == 0 and H % 2 == 0
    kern = functools.partial(_fox_kernel, tq=tq)
    return pl.pallas_call(
        kern,
        grid=(B, H // 2, T // tq),
        in_specs=[
            pl.BlockSpec((1, 2, tq, LANES), lambda b, hp, qi: (b, hp, qi, 0)),
            pl.BlockSpec((1, 2, T, LANES), lambda b, hp, qi: (b, hp, 0, 0)),
            pl.BlockSpec((1, 2, T, LANES), lambda b, hp, qi: (b, hp, 0, 0)),
        ],
        out_specs=pl.BlockSpec((1, tq, LANES), lambda b, hp, qi: (b, qi, hp)),
        out_shape=jax.ShapeDtypeStruct((B, T, H * HEAD_DIM), BF16),
        scratch_shapes=[pltpu.VMEM((2, tq, LANES), F32)] * 2,
        compiler_params=pltpu.CompilerParams(
            dimension_semantics=("arbitrary", "arbitrary", "arbitrary"),
            vmem_limit_bytes=VMEM_LIMIT),
        name="fox",
    )(q_aug, k_aug, v_aug)


def _unit_lower_inverse(Ls):
    n = Ls[0].shape[0]
    r = lax.broadcasted_iota(jnp.int32, (n, n), 0)
    c = lax.broadcasted_iota(jnp.int32, (n, n), 1)
    same = (r >> 1) == (c >> 1)
    eye = jnp.where(r == c, 1.0, 0.0)
    ps = [eye + jnp.where(same, L, 0.0) for L in Ls]
    for s in range(2, 7):
        merged = (r >> s) == (c >> s)
        sel = merged & jnp.logical_not(same)
        offs = [jnp.where(sel, L, 0.0).astype(BF16) for L in Ls]
        pbs = [p.astype(BF16) for p in ps]
        tmps = [_dot(pb, off).astype(BF16) for pb, off in zip(pbs, offs)]
        ps = [p + _dot(tmp, pb) for p, tmp, pb in zip(ps, tmps, pbs)]
        same = merged
    return ps


def _rwkv_kernel(rw_ref, w0_ref, a0_ref, kk_ref, ka_ref, rk_ref,
                 lnw_ref, lnb_ref, wup_ref, aup_ref, gup_ref, ones_ref, tri_ref,
                 o_ref, s_scr, y_scr, *, width):
    t = pl.program_id(1)
    n_seq, tc, rw_w = rw_ref.shape
    n_rows = n_seq * tc
    n_chunks = tc // CHUNK
    n_groups = width // GROUP_W

    @pl.when(t == 0)
    def _():
        s_scr[...] = jnp.zeros_like(s_scr)

    ones_blk = ones_ref[...]

    def segsum(x):
        hi, lo = _split2(x)
        return _dot(hi, ones_blk) + _dot(lo, ones_blk)

    p = rw_ref[...].reshape(n_rows, rw_w)
    r = p[:, 0:width]
    k = p[:, width:2 * width]
    v = p[:, 2 * width:3 * width]
    o = 3 * width
    w_lat = p[:, o:o + LANES]
    a_lat = p[:, o + LANES:o + 2 * LANES]
    g_lat = p[:, o + 2 * LANES:o + 4 * LANES]

    z = w0_ref[...] + _dot(jnp.tanh(w_lat).astype(BF16), wup_ref[...])
    g_log = _sigmoid(z) * (-(2.718281828459045 ** -0.5))
    lr = _sigmoid(a0_ref[...] + _dot(a_lat.astype(BF16), aup_ref[...]))
    gate = _dot(_sigmoid(g_lat).astype(BF16), gup_ref[...])
    kk = k * kk_ref[...]
    kk = kk * lax.rsqrt(jnp.maximum(segsum(kk * kk), 1e-24))
    k = k * (1.0 + (lr - 1.0) * ka_ref[...])
    bonus = segsum(r * k * rk_ref[...]) * v

    tri = tri_ref[...]
    G = jnp.concatenate([_dot_exact_lhs01(tri, g_log[s * tc:(s + 1) * tc]) for s in range(n_seq)], axis=0)
    e_pos = jnp.exp(G)
    e_neg = jnp.exp(-G)
    a_t = (-kk * jnp.exp(G - g_log)).astype(BF16)
    b_t = (kk * lr * e_neg).astype(BF16)
    r_t = (r * e_pos).astype(BF16)
    k_t = (k * e_neg).astype(BF16)
    v_t = v.astype(BF16)

    lane_head = lax.broadcasted_iota(jnp.int32, (CHUNK, GROUP_W), 1) // HEAD_DIM
    head_masks = [lane_head == h for h in range(GROUP_HEADS)]
    rt = lax.broadcasted_iota(jnp.int32, (GROUP_W, GROUP_W), 0)
    ct = lax.broadcasted_iota(jnp.int32, (GROUP_W, GROUP_W), 1)
    strict = (rt % CHUNK) > (ct % CHUNK)
    same_head = (rt // HEAD_DIM) == (ct // HEAD_DIM)
    diag = rt == ct
    incl_w = (lax.broadcasted_iota(jnp.int32, (CHUNK, GROUP_W), 0)
              >= lax.broadcasted_iota(jnp.int32, (CHUNK, GROUP_W), 1) % CHUNK)
    rows = lambda s, ci: slice((s * n_chunks + ci) * CHUNK, (s * n_chunks + ci + 1) * CHUNK)
    cols = lambda g: slice(g * GROUP_W, (g + 1) * GROUP_W)
    wide = lambda x, u: x[rows(u[0], u[1]), cols(u[2])]

    def stack(x):
        zero = jnp.zeros_like(x)
        return jnp.concatenate([jnp.where(mk, x, zero) for mk in head_masks], axis=0)

    def fold(x):
        return (x[0:CHUNK] + x[CHUNK:2 * CHUNK]) + (x[2 * CHUNK:3 * CHUNK] + x[3 * CHUNK:])

    units = [(s, ci, g) for ci in range(n_chunks) for s in range(n_seq) for g in range(n_groups)]
    a_s = {u: stack(wide(a_t, u)) for u in units}
    b_s = {u: stack(wide(b_t, u)) for u in units}
    k_s = {u: stack(wide(k_t, u)) for u in units}
    v_s = {u: stack(wide(v_t, u)) for u in units}
    l_all = {u: _dot_nt(jnp.concatenate([a_s[u], wide(r_t, u)], axis=0),
                        jnp.concatenate([b_s[u], k_s[u]], axis=0)) for u in units}
    l_ab = {u: jnp.where(strict, l_all[u][0:GROUP_W, 0:GROUP_W], 0.0) for u in units}
    l_ak = {u: jnp.where(strict, l_all[u][0:GROUP_W, GROUP_W:], 0.0).astype(BF16) for u in units}
    l_rb = {u: jnp.where(incl_w, l_all[u][GROUP_W:, 0:GROUP_W], 0.0).astype(BF16) for u in units}
    l_rk = {u: jnp.where(incl_w, l_all[u][GROUP_W:, GROUP_W:], 0.0).astype(BF16) for u in units}
    av = {u: _dot(jnp.concatenate([l_ak[u], l_rk[u]], axis=0), v_s[u]) for u in units}
    akv = {u: av[u][0:GROUP_W].astype(BF16) for u in units}
    y_rk = {u: av[u][GROUP_W:] for u in units}
    t_inv = _unit_lower_inverse([l_ab[u] for u in units])
    wu = {u: _dot(fold(t_inv[n]).astype(BF16), jnp.concatenate([a_s[u], akv[u]], axis=1))
          for n, u in enumerate(units)}
    last = lambda u: (u[0] * n_chunks + u[1] + 1) * CHUNK - 1
    gc = {u: e_pos[last(u):last(u) + 1, cols(u[2])] for u in units}
    gcol = {u: jnp.sum(jnp.where(diag, jnp.broadcast_to(gc[u], (GROUP_W, GROUP_W)), 0.0),
                       axis=1, keepdims=True) for u in units}
    bk_h = {u: jnp.concatenate([(wide(b_t, u).astype(F32) * gc[u]).astype(BF16),
                                (wide(k_t, u).astype(F32) * gc[u]).astype(BF16)], axis=0) for u in units}

    chains = [(s, g) for s in range(n_seq) for g in range(n_groups)]
    state = {c: s_scr[c[0] * n_groups + c[1]] for c in chains}
    for ci in range(n_chunks):
        level = [(s, ci, g) for s, g in chains]
        zr = {u: _dot(jnp.concatenate([wu[u][:, 0:GROUP_W].astype(BF16), wide(r_t, u)], axis=0),
                      state[u[0], u[2]].astype(BF16)) for u in level}
        z_b = {u: (zr[u][0:CHUNK] + wu[u][:, GROUP_W:]).astype(BF16) for u in level}
        upd = {u: _dot_tn(bk_h[u], jnp.concatenate([z_b[u], wide(v_t, u)], axis=0)) for u in level}
        for u in level:
            state[u[0], u[2]] = state[u[0], u[2]] * gcol[u] + jnp.where(same_head, upd[u], 0.0)
        for u in level:
            y_scr[rows(u[0], ci), cols(u[2])] = (zr[u][CHUNK:] + _dot(l_rb[u], stack(z_b[u]))) + y_rk[u]
    for c in chains:
        s_scr[c[0] * n_groups + c[1]] = state[c]

    y = y_scr[...]
    d = y - segsum(y) * (1.0 / HEAD_DIM)
    var = segsum(d * d) * (1.0 / HEAD_DIM)
    yn = d * lax.rsqrt(var + LNX_EPS) * lnw_ref[...] + lnb_ref[...]
    o_ref[...] = ((yn + bonus) * gate).astype(o_ref.dtype).reshape(n_seq, tc, width)


def _rwkv(rw, w0, a0, k_k, k_a, r_k, ln_w, ln_b, w_up, a_up, g_up, ones_blk, tri_blk, *,
          width, tc, n_seq):
    B, T, rw_w = rw.shape
    assert B % n_seq == 0 and T % tc == 0 and tc % CHUNK == 0 and width % GROUP_W == 0
    n_groups = width // GROUP_W
    kern = functools.partial(_rwkv_kernel, width=width)
    small = [w0, a0, k_k, k_a, r_k, ln_w, ln_b, w_up, a_up, g_up, ones_blk, tri_blk]
    return pl.pallas_call(
        kern,
        grid=(B // n_seq, T // tc),
        in_specs=[
            pl.BlockSpec((n_seq, tc, rw_w), lambda b, t: (b, t, 0)),
        ] + [_const_spec(a.shape) for a in small],
        out_specs=pl.BlockSpec((n_seq, tc, width), lambda b, t: (b, t, 0)),
        out_shape=jax.ShapeDtypeStruct((B, T, width), BF16),
        scratch_shapes=[
            pltpu.VMEM((n_seq * n_groups, GROUP_W, GROUP_W), F32),
            pltpu.VMEM((n_seq * tc, width), F32),
        ],
        compiler_params=pltpu.CompilerParams(
            dimension_semantics=("arbitrary", "arbitrary"), vmem_limit_bytes=VMEM_LIMIT),
        name="rwkv",
    )(rw, *small)


def _outffn_kernel(x_ref, of_ref, or_ref, wof_ref, wor_ref, gpost_ref, gfpre_ref, gfpost_ref,
                   wg_ref, wu_ref, wd_ref, o_ref, *, n_sub):
    tm = x_ref.shape[0]
    sub = tm // n_sub
    parts = [slice(i * sub, (i + 1) * sub) for i in range(n_sub)]
    mix = [_dot(of_ref[sl, :], wof_ref[...]) + _dot(or_ref[sl, :], wor_ref[...]) for sl in parts]
    h = [x_ref[sl, :] + _rms(m) * gpost_ref[...] for sl, m in zip(parts, mix)]
    z = [(_rms(hh) * gfpre_ref[...]).astype(BF16) for hh in h]
    gate = [_dot(zz, wg_ref[...]) for zz in z]
    up = [_dot(zz, wu_ref[...]) for zz in z]
    act = [(g * _sigmoid(g) * u).astype(BF16) for g, u in zip(gate, up)]
    f = [_dot(a, wd_ref[...]) for a in act]
    for sl, hh, ff in zip(parts, h, f):
        o_ref[sl, :] = hh + _rms(ff) * gfpost_ref[...]


def _outffn(x2, o_fox, o_rw, wof, wor, g_post, g_fpre, g_fpost, wg, wu, wd, *, tm, n_sub):
    N, D = x2.shape
    assert N % tm == 0 and tm % n_sub == 0
    consts = [wof, wor, g_post, g_fpre, g_fpost, wg, wu, wd]
    return pl.pallas_call(
        functools.partial(_outffn_kernel, n_sub=n_sub),
        grid=(N // tm,),
        in_specs=[
            pl.BlockSpec((tm, D), lambda i: (i, 0)),
            pl.BlockSpec((tm, o_fox.shape[1]), lambda i: (i, 0)),
            pl.BlockSpec((tm, o_rw.shape[1]), lambda i: (i, 0)),
        ] + [_const_spec(a.shape) for a in consts],
        out_specs=pl.BlockSpec((tm, D), lambda i: (i, 0)),
        out_shape=jax.ShapeDtypeStruct((N, D), F32),
        compiler_params=pltpu.CompilerParams(
            dimension_semantics=("arbitrary",), vmem_limit_bytes=VMEM_LIMIT),
        name="outffn",
    )(x2, o_fox, o_rw, *consts)


def _pad_cols(a, n):
    return jnp.pad(a, ((0, 0), (0, n - a.shape[1])))


def _pad_rows(a, n):
    return jnp.pad(a, ((0, n - a.shape[0]), (0, 0)))


def _block_layer(h, attn_norm_pre, attn_norm_post, w_in, fox_forget_bias, shift_mu, rwkv_w0,
                 rwkv_w_up, rwkv_a0, rwkv_a_up, rwkv_g_up, rwkv_k_k, rwkv_k_a, rwkv_r_k,
                 rwkv_ln_w, rwkv_ln_b, w_out, ffn_norm_pre, ffn_norm_post, ffn_w_gate, ffn_w_up,
                 ffn_w_down):
    B, T, D = h.shape
    fw = D // 2
    rwid = D // 2
    nh = fw // HEAD_DIM
    hw = nh * LANES

    fq, fk, fv = w_in[:, 0:fw], w_in[:, fw:2 * fw], w_in[:, 2 * fw:3 * fw]
    ffw = w_in[:, 3 * fw:3 * fw + nh]
    rcol = 3 * fw + nh

    def rw_layout(a, spare):
        o = 3 * rwid
        return jnp.concatenate([
            a[:, 0:o],
            _pad_cols(jnp.concatenate([a[:, o:o + DECAY_LORA], spare], axis=1), LANES),
            _pad_cols(a[:, o + DECAY_LORA:o + DECAY_LORA + AAA_LORA], LANES),
            _pad_cols(a[:, o + DECAY_LORA + AAA_LORA:], 2 * LANES)], axis=1)

    w_rw = rw_layout(w_in[:, rcol:], ffw)
    rw_w = w_rw.shape[1]
    wp = jnp.concatenate([fq, fk, fv, w_rw], axis=1).astype(BF16)
    mu = rw_layout(shift_mu[None, :], jnp.zeros((1, nh), F32))
    fbias = _pad_cols(jnp.pad(fox_forget_bias[None, :], ((0, 0), (FORGET_LANE, 0))), LANES)

    tm1 = 512
    ii = jnp.arange(tm1)
    tri = (ii[:, None] >= ii[None, :]).astype(BF16)
    bias_lane = lambda hd, p: hd * LANES + (HEAD_DIM if hd % 2 == 0 else 0) + 3 + p
    sel = np.zeros((LANES, hw), np.float32)
    qconst = np.zeros((1, hw), np.float32)
    for hd in range(nh):
        for p in range(3):
            sel[FORGET_LANE + hd + 8 * p, bias_lane(hd, p)] = 1.0
            qconst[0, bias_lane(hd, p)] = 1.0
    sel = jnp.asarray(sel, BF16)
    qconst = jnp.asarray(qconst)

    q_aug, k_aug, v, rw = _inproj(h, attn_norm_pre[None, :], wp, fbias, tri, sel, qconst, mu,
                                  n_heads=nh, rw_w=rw_w, f_col=3 * rwid, tm=tm1)

    o_fox = _fox(q_aug, k_aug, v, tq=512)

    tc = 128
    hh = jnp.arange(rwid) // HEAD_DIM
    ones_blk = (hh[:, None] == hh[None, :]).astype(BF16)
    jj = jnp.arange(tc)
    tri_blk = ((jj[:, None] >= jj[None, :]) & (jj[:, None] // CHUNK == jj[None, :] // CHUNK)).astype(BF16)
    row = lambda a: a.reshape(1, -1)
    o_rw = _rwkv(rw, row(rwkv_w0), row(rwkv_a0), row(rwkv_k_k), row(rwkv_k_a), row(rwkv_r_k),
                 row(rwkv_ln_w), row(rwkv_ln_b),
                 _pad_rows(rwkv_w_up, LANES).astype(BF16), _pad_rows(rwkv_a_up, LANES).astype(BF16),
                 _pad_rows(rwkv_g_up, 2 * LANES).astype(BF16), ones_blk, tri_blk,
                 width=rwid, tc=tc, n_seq=4)

    out = _outffn(h.reshape(B * T, D), o_fox.reshape(B * T, fw), o_rw.reshape(B * T, rwid),
                  w_out[0:fw].astype(BF16), w_out[fw:].astype(BF16),
                  row(attn_norm_post), row(ffn_norm_pre), row(ffn_norm_post),
                  ffn_w_gate.astype(BF16), ffn_w_up.astype(BF16), ffn_w_down.astype(BF16),
                  tm=512, n_sub=2)
    return out.reshape(B, T, D)


def kernel(x, attn_norm_pre, attn_norm_post, w_in, fox_forget_bias, shift_mu, rwkv_w0, rwkv_w_up,
           rwkv_a0, rwkv_a_up, rwkv_g_up, rwkv_k_k, rwkv_k_a, rwkv_r_k, rwkv_ln_w, rwkv_ln_b, w_out,
           ffn_norm_pre, ffn_norm_post, ffn_w_gate, ffn_w_up, ffn_w_down):
    h = x
    for l in range(attn_norm_pre.shape[0]):
        h = _block_layer(h, attn_norm_pre[l], attn_norm_post[l], w_in[l], fox_forget_bias[l],
                         shift_mu[l], rwkv_w0[l], rwkv_w_up[l], rwkv_a0[l], rwkv_a_up[l],
                         rwkv_g_up[l], rwkv_k_k[l], rwkv_k_a[l], rwkv_r_k[l], rwkv_ln_w[l],
                         rwkv_ln_b[l], w_out[l], ffn_norm_pre[l], ffn_norm_post[l], ffn_w_gate[l],
                         ffn_w_up[l], ffn_w_down[l])
    return h
```

```python
import functools

import jax
import jax.numpy as jnp
import numpy as np
from jax import lax
from jax.experimental import pallas as pl
from jax.experimental.pallas import tpu as pltpu

F32 = jnp.float32
BF16 = jnp.bfloat16

HEAD_DIM = 64
LANES = 128
NORM_EPS = 1e-6
LNX_EPS = 64e-5
LOG2E = 1.4426950408889634
DECAY_LORA = 64
AAA_LORA = 64
GATE_LORA = 160
CHUNK = 64
GROUP_HEADS = 4
GROUP_W = GROUP_HEADS * HEAD_DIM
FORGET_LANE = DECAY_LORA
VMEM_LIMIT = 56 * 1024 * 1024


def _dot(a, b):
    return jnp.dot(a, b, preferred_element_type=F32)


def _dot_nt(a, b):
    return lax.dot_general(a, b, (((1,), (1,)), ((), ())), preferred_element_type=F32)


def _dot_tn(a, b):
    return lax.dot_general(a, b, (((0,), (0,)), ((), ())), preferred_element_type=F32)


def _split2(x):
    hi = x.astype(BF16)
    lo = (x - hi.astype(F32)).astype(BF16)
    return hi, lo


def _split3(x):
    hi = x.astype(BF16)
    r = x - hi.astype(F32)
    mid = r.astype(BF16)
    lo = (r - mid.astype(F32)).astype(BF16)
    return hi, mid, lo


def _dot_exact_lhs01(ones01, x):
    hi, mid, lo = _split3(x)
    return _dot(ones01, hi) + (_dot(ones01, mid) + _dot(ones01, lo))


def _rms(x):
    return x * lax.rsqrt(jnp.mean(x * x, axis=-1, keepdims=True) + NORM_EPS)


def _softplus(x):
    return jnp.maximum(x, 0.0) + jnp.log1p(jnp.exp(-jnp.abs(x)))


def _sigmoid(x):
    return 1.0 / (1.0 + jnp.exp(-x))


def _const_spec(shape):
    nd = len(shape)
    return pl.BlockSpec(shape, lambda *_: (0,) * nd, pipeline_mode=pl.Buffered(1))


def _inproj_kernel(x_ref, g_ref, w_ref, fb_ref, tri_ref, sel_ref, qc_ref, mu_ref,
                   q_out, k_out, v_out, rw_out, carry_ref, rwlast_ref, *, n_heads, rw_w, f_col):
    t = pl.program_id(1)

    @pl.when(t == 0)
    def _():
        carry_ref[...] = jnp.zeros_like(carry_ref)
        rwlast_ref[...] = jnp.zeros_like(rwlast_ref)

    fw = n_heads * HEAD_DIM
    tm = x_ref.shape[1]
    u = (_rms(x_ref[0]) * g_ref[...]).astype(BF16)
    lane = lax.broadcasted_iota(jnp.int32, (tm, LANES), 1)
    lower = lane < HEAD_DIM

    rw = _dot(u, w_ref[:, 3 * fw:3 * fw + rw_w])
    row0 = lax.broadcasted_iota(jnp.int32, (tm, 1), 0) == 0
    rw_prev = jnp.where(row0, rwlast_ref[...], pltpu.roll(rw, 1, 0))
    rwlast_ref[...] = rw[tm - 1:tm, :]
    rw_out[0] = rw + mu_ref[...] * (rw_prev - rw)

    fo = f_col
    in_f = (lane >= FORGET_LANE) & (lane < FORGET_LANE + n_heads)

    def pack3(x):
        hi = x.astype(BF16).astype(F32)
        r1 = x - hi
        mid = r1.astype(BF16).astype(F32)
        lo = r1 - mid
        keep = lambda a: jnp.where(in_f, a, 0.0)
        return (keep(hi) + pltpu.roll(keep(mid), 8, 1) + pltpu.roll(keep(lo), 16, 1)).astype(BF16)

    def unpack3(x):
        return x + pltpu.roll(x, LANES - 8, 1) + pltpu.roll(x, LANES - 16, 1)

    logf = -_softplus(-(rw[:, fo:fo + LANES] + fb_ref[...]))
    c = unpack3(_dot(tri_ref[...], pack3(logf))) + carry_ref[...]
    carry_ref[...] = c[tm - 1:tm, :]

    qk = _dot(u, w_ref[:, 0:2 * fw])
    k_bias = _dot(pack3(c * (-LOG2E)), sel_ref[...])
    scale = HEAD_DIM ** -0.5 * LOG2E
    for hp in range(n_heads // 2):
        qb = qk[:, hp * LANES:(hp + 1) * LANES] * scale
        kb = qk[:, fw + hp * LANES:fw + (hp + 1) * LANES]
        for h, keep in ((2 * hp, lower), (2 * hp + 1, jnp.logical_not(lower))):
            q_out[0, h] = jnp.where(keep, qb, qc_ref[:, h * LANES:(h + 1) * LANES]).astype(BF16)
            k_out[0, h] = jnp.where(keep, kb, k_bias[:, h * LANES:(h + 1) * LANES]).astype(BF16)

    v = _dot(u, w_ref[:, 2 * fw:3 * fw])
    one_even = jnp.where(lane == HEAD_DIM, 1.0, 0.0)
    one_odd = jnp.where(lane == 0, 1.0, 0.0)
    for hp in range(n_heads // 2):
        blk = v[:, hp * LANES:(hp + 1) * LANES]
        v_out[0, 2 * hp] = jnp.where(lower, blk, one_even).astype(BF16)
        v_out[0, 2 * hp + 1] = jnp.where(lower, one_odd, blk).astype(BF16)


def _inproj(x, g_pre, wp, fbias, tri, sel, qconst, mu, *, n_heads, rw_w, f_col, tm):
    B, T, D = x.shape
    assert T % tm == 0 and n_heads % 2 == 0 and tm == tri.shape[0]
    kern = functools.partial(_inproj_kernel, n_heads=n_heads, rw_w=rw_w, f_col=f_col)
    return pl.pallas_call(
        kern,
        grid=(B, T // tm),
        in_specs=[
            pl.BlockSpec((1, tm, D), lambda b, t: (b, t, 0)),
            _const_spec(g_pre.shape),
            _const_spec(wp.shape),
            _const_spec(fbias.shape),
            _const_spec(tri.shape),
            _const_spec(sel.shape),
            _const_spec(qconst.shape),
            _const_spec(mu.shape),
        ],
        out_specs=[
            pl.BlockSpec((1, n_heads, tm, LANES), lambda b, t: (b, 0, t, 0)),
            pl.BlockSpec((1, n_heads, tm, LANES), lambda b, t: (b, 0, t, 0)),
            pl.BlockSpec((1, n_heads, tm, LANES), lambda b, t: (b, 0, t, 0)),
            pl.BlockSpec((1, tm, rw_w), lambda b, t: (b, t, 0)),
        ],
        out_shape=[
            jax.ShapeDtypeStruct((B, n_heads, T, LANES), BF16),
            jax.ShapeDtypeStruct((B, n_heads, T, LANES), BF16),
            jax.ShapeDtypeStruct((B, n_heads, T, LANES), BF16),
            jax.ShapeDtypeStruct((B, T, rw_w), F32),
        ],
        scratch_shapes=[pltpu.VMEM((1, LANES), F32),
                        pltpu.VMEM((1, rw_w), F32)],
        compiler_params=pltpu.CompilerParams(
            dimension_semantics=("arbitrary", "arbitrary"), vmem_limit_bytes=VMEM_LIMIT),
        name="inproj",
    )(x, g_pre, wp, fbias, tri, sel, qconst, mu)


def _fox_kernel(q_ref, k_ref, v_ref, o_ref, m_scr, acc_scr, *, tq):
    qi = pl.program_id(2)
    heads = range(2)
    half = tq // 2

    def step(jobs):
        todo = [(n, h) for n in range(len(jobs)) for h in heads]
        qrows = [pl.ds(q0, nq) for q0, nq, _, _, _ in jobs]
        krows = [pl.ds(pl.multiple_of(k0, tk), tk) for _, _, k0, tk, _ in jobs]
        s = {(n, h): _dot_nt(q_ref[0, h, qrows[n], :], k_ref[0, h, krows[n], :]) for n, h in todo}
        for n, (_, nq, _, tk, masked) in enumerate(jobs):
            if masked:
                row = lax.broadcasted_iota(jnp.int32, (nq, tk), 0)
                col = lax.broadcasted_iota(jnp.int32, (nq, tk), 1)
                for h in heads:
                    s[n, h] = jnp.where(col <= row + (tk - nq), s[n, h], -jnp.inf)
        m_prev = {(n, h): m_scr[h, qrows[n], :] for n, h in todo}
        m_next = {nh: jnp.maximum(m_prev[nh], jnp.max(s[nh], axis=1, keepdims=True)) for nh in todo}
        p = {(n, h): jnp.exp2(s[n, h] - jnp.concatenate([m_next[n, h]] * (jobs[n][3] // LANES), axis=1)
                              ).astype(BF16) for n, h in todo}
        for n, h in todo:
            alpha = jnp.exp2(m_prev[n, h] - m_next[n, h])
            acc_scr[h, qrows[n], :] = (alpha * acc_scr[h, qrows[n], :]
                                       + _dot(p[n, h], v_ref[0, h, krows[n], :]))
            m_scr[h, qrows[n], :] = m_next[n, h]

    for h in heads:
        m_scr[h] = jnp.full(m_scr.shape[1:], -jnp.inf, F32)
        acc_scr[h] = jnp.zeros(acc_scr.shape[1:], F32)

    def body(j, carry):
        step([(0, tq, j * tq, tq, False)])
        return carry

    lax.fori_loop(0, qi, body, 0)
    step([(0, half, qi * tq, half, True), (half, half, qi * tq, tq, True)])

    lane = lax.broadcasted_iota(jnp.int32, acc_scr.shape[1:], 1)
    acc0 = acc_scr[0]
    acc1 = acc_scr[1]
    o = jnp.where(lane < HEAD_DIM, acc0 / acc0[:, HEAD_DIM:HEAD_DIM + 1], acc1 / acc1[:, 0:1])
    o_ref[0] = o.astype(o_ref.dtype)


def _fox(q_aug, k_aug, v_aug, *, tq):
    B, H, T, _ = q_aug.shape
    assert T % tq == 0 and H % 2 == 0
    kern = functools.partial(_fox_kernel, tq=tq)
    return pl.pallas_call(
        kern,
        grid=(B, H // 2, T // tq),
        in_specs=[
            pl.BlockSpec((1, 2, tq, LANES), lambda b, hp, qi: (b, hp, qi, 0)),
            pl.BlockSpec((1, 2, T, LANES), lambda b, hp, qi: (b, hp, 0, 0)),
            pl.BlockSpec((1, 2, T, LANES), lambda b, hp, qi: (b, hp, 0, 0)),
        ],
        out_specs=pl.BlockSpec((1, tq, LANES), lambda b, hp, qi: (b, qi, hp)),
        out_shape=jax.ShapeDtypeStruct((B, T, H * HEAD_DIM), BF16),
        scratch_shapes=[pltpu.VMEM((2, tq, LANES), F32)] * 2,
        compiler_params=pltpu.CompilerParams(
            dimension_semantics=("arbitrary", "arbitrary", "arbitrary"),
            vmem_limit_bytes=VMEM_LIMIT),
        name="fox",
    )(q_aug, k_aug, v_aug)


def _unit_lower_inverse(Ls):
    n = Ls[0].shape[0]
    r = lax.broadcasted_iota(jnp.int32, (n, n), 0)
    c = lax.broadcasted_iota(jnp.int32, (n, n), 1)
    same = (r >> 1) == (c >> 1)
    eye = jnp.where(r == c, 1.0, 0.0)
    ps = [eye + jnp.where(same, L, 0.0) for L in Ls]
    for s in range(2, 7):
        merged = (r >> s) == (c >> s)
        sel = merged & jnp.logical_not(same)
        offs = [jnp.where(sel, L, 0.0).astype(BF16) for L in Ls]
        pbs = [p.astype(BF16) for p in ps]
        tmps = [_dot(pb, off).astype(BF16) for pb, off in zip(pbs, offs)]
        ps = [p + _dot(tmp, pb) for p, tmp, pb in zip(ps, tmps, pbs)]
        same = merged
    return ps


def _rwkv_kernel(rw_ref, w0_ref, a0_ref, kk_ref, ka_ref, rk_ref,
                 lnw_ref, lnb_ref, wup_ref, aup_ref, gup_ref, ones_ref, tri_ref,
                 o_ref, s_scr, y_scr, *, width):
    t = pl.program_id(1)
    n_seq, tc, rw_w = rw_ref.shape
    n_rows = n_seq * tc
    n_chunks = tc // CHUNK
    n_groups = width // GROUP_W

    @pl.when(t == 0)
    def _():
        s_scr[...] = jnp.zeros_like(s_scr)

    ones_blk = ones_ref[...]

    def segsum(x):
        return _dot(x.astype(BF16), ones_blk)

    p = rw_ref[...].reshape(n_rows, rw_w)
    r = p[:, 0:width]
    k = p[:, width:2 * width]
    v = p[:, 2 * width:3 * width]
    o = 3 * width
    w_lat = p[:, o:o + LANES]
    a_lat = p[:, o + LANES:o + 2 * LANES]
    g_lat = p[:, o + 2 * LANES:o + 4 * LANES]

    z = w0_ref[...] + _dot(jnp.tanh(w_lat).astype(BF16), wup_ref[...])
    g_log = _sigmoid(z) * (-(2.718281828459045 ** -0.5))
    lr = _sigmoid(a0_ref[...] + _dot(a_lat.astype(BF16), aup_ref[...]))
    gate = _dot(_sigmoid(g_lat).astype(BF16), gup_ref[...])
    kk = k * kk_ref[...]
    kk = kk * lax.rsqrt(jnp.maximum(segsum(kk * kk), 1e-24))
    k = k * (1.0 + (lr - 1.0) * ka_ref[...])
    bonus = segsum(r * k * rk_ref[...]) * v

    tri = tri_ref[...]
    G = jnp.concatenate([_dot_exact_lhs01(tri, g_log[s * tc:(s + 1) * tc]) for s in range(n_seq)], axis=0)
    e_pos = jnp.exp(G)
    e_neg = jnp.exp(-G)
    a_t = (-kk * jnp.exp(G - g_log)).astype(BF16)
    b_t = (kk * lr * e_neg).astype(BF16)
    r_t = (r * e_pos).astype(BF16)
    k_t = (k * e_neg).astype(BF16)
    v_t = v.astype(BF16)

    lane_head = lax.broadcasted_iota(jnp.int32, (CHUNK, GROUP_W), 1) // HEAD_DIM
    head_masks = [lane_head == h for h in range(GROUP_HEADS)]
    rt = lax.broadcasted_iota(jnp.int32, (GROUP_W, GROUP_W), 0)
    ct = lax.broadcasted_iota(jnp.int32, (GROUP_W, GROUP_W), 1)
    strict = (rt % CHUNK) > (ct % CHUNK)
    same_head = (rt // HEAD_DIM) == (ct // HEAD_DIM)
    diag = rt == ct
    incl_w = (lax.broadcasted_iota(jnp.int32, (CHUNK, GROUP_W), 0)
              >= lax.broadcasted_iota(jnp.int32, (CHUNK, GROUP_W), 1) % CHUNK)
    rows = lambda s, ci: slice((s * n_chunks + ci) * CHUNK, (s * n_chunks + ci + 1) * CHUNK)
    cols = lambda g: slice(g * GROUP_W, (g + 1) * GROUP_W)
    wide = lambda x, u: x[rows(u[0], u[1]), cols(u[2])]

    def stack(x):
        zero = jnp.zeros_like(x)
        return jnp.concatenate([jnp.where(mk, x, zero) for mk in head_masks], axis=0)

    def fold(x):
        return (x[0:CHUNK] + x[CHUNK:2 * CHUNK]) + (x[2 * CHUNK:3 * CHUNK] + x[3 * CHUNK:])

    units = [(s, ci, g) for ci in range(n_chunks) for s in range(n_seq) for g in range(n_groups)]
    a_s = {u: stack(wide(a_t, u)) for u in units}
    b_s = {u: stack(wide(b_t, u)) for u in units}
    k_s = {u: stack(wide(k_t, u)) for u in units}
    v_s = {u: stack(wide(v_t, u)) for u in units}
    l_all = {u: _dot_nt(jnp.concatenate([a_s[u], wide(r_t, u)], axis=0),
                        jnp.concatenate([b_s[u], k_s[u]], axis=0)) for u in units}
    l_ab = {u: jnp.where(strict, l_all[u][0:GROUP_W, 0:GROUP_W], 0.0) for u in units}
    l_ak = {u: jnp.where(strict, l_all[u][0:GROUP_W, GROUP_W:], 0.0).astype(BF16) for u in units}
    l_rb = {u: jnp.where(incl_w, l_all[u][GROUP_W:, 0:GROUP_W], 0.0).astype(BF16) for u in units}
    l_rk = {u: jnp.where(incl_w, l_all[u][GROUP_W:, GROUP_W:], 0.0).astype(BF16) for u in units}
    av = {u: _dot(jnp.concatenate([l_ak[u], l_rk[u]], axis=0), v_s[u]) for u in units}
    akv = {u: av[u][0:GROUP_W].astype(BF16) for u in units}
    y_rk = {u: av[u][GROUP_W:] for u in units}
    t_inv = _unit_lower_inverse([l_ab[u] for u in units])
    wu = {u: _dot(fold(t_inv[n]).astype(BF16), jnp.concatenate([a_s[u], akv[u]], axis=1))
          for n, u in enumerate(units)}
    last = lambda u: (u[0] * n_chunks + u[1] + 1) * CHUNK - 1
    gc = {u: e_pos[last(u):last(u) + 1, cols(u[2])] for u in units}
    gcol = {u: jnp.sum(jnp.where(diag, jnp.broadcast_to(gc[u], (GROUP_W, GROUP_W)), 0.0),
                       axis=1, keepdims=True) for u in units}
    bk_h = {u: jnp.concatenate([(wide(b_t, u).astype(F32) * gc[u]).astype(BF16),
                                (wide(k_t, u).astype(F32) * gc[u]).astype(BF16)], axis=0) for u in units}

    chains = [(s, g) for s in range(n_seq) for g in range(n_groups)]
    state = {c: s_scr[c[0] * n_groups + c[1]] for c in chains}
    for ci in range(n_chunks):
        level = [(s, ci, g) for s, g in chains]
        zr = {u: _dot(jnp.concatenate([wu[u][:, 0:GROUP_W].astype(BF16), wide(r_t, u)], axis=0),
                      state[u[0], u[2]].astype(BF16)) for u in level}
        z_b = {u: (zr[u][0:CHUNK] + wu[u][:, GROUP_W:]).astype(BF16) for u in level}
        upd = {u: _dot_tn(bk_h[u], jnp.concatenate([z_b[u], wide(v_t, u)], axis=0)) for u in level}
        for u in level:
            state[u[0], u[2]] = state[u[0], u[2]] * gcol[u] + jnp.where(same_head, upd[u], 0.0)
        for u in level:
            y_scr[rows(u[0], ci), cols(u[2])] = (zr[u][CHUNK:] + _dot(l_rb[u], stack(z_b[u]))) + y_rk[u]
    for c in chains:
        s_scr[c[0] * n_groups + c[1]] = state[c]

    y = y_scr[...]
    d = y - segsum(y) * (1.0 / HEAD_DIM)
    var = segsum(d * d) * (1.0 / HEAD_DIM)
    yn = d * lax.rsqrt(var + LNX_EPS) * lnw_ref[...] + lnb_ref[...]
    o_ref[...] = ((yn + bonus) * gate).astype(o_ref.dtype).reshape(n_seq, tc, width)


def _rwkv(rw, w0, a0, k_k, k_a, r_k, ln_w, ln_b, w_up, a_up, g_up, ones_blk, tri_blk, *,
          width, tc, n_seq):
    B, T, rw_w = rw.shape
    assert B % n_seq == 0 and T % tc == 0 and tc % CHUNK == 0 and width % GROUP_W == 0
    n_groups = width // GROUP_W
    kern = functools.partial(_rwkv_kernel, width=width)
    small = [w0, a0, k_k, k_a, r_k, ln_w, ln_b, w_up, a_up, g_up, ones_blk, tri_blk]
    return pl.pallas_call(
        kern,
        grid=(B // n_seq, T // tc),
        in_specs=[
            pl.BlockSpec((n_seq, tc, rw_w), lambda b, t: (b, t, 0)),
        ] + [_const_spec(a.shape) for a in small],
        out_specs=pl.BlockSpec((n_seq, tc, width), lambda b, t: (b, t, 0)),
        out_shape=jax.ShapeDtypeStruct((B, T, width), BF16),
        scratch_shapes=[
            pltpu.VMEM((n_seq * n_groups, GROUP_W, GROUP_W), F32),
            pltpu.VMEM((n_seq * tc, width), F32),
        ],
        compiler_params=pltpu.CompilerParams(
            dimension_semantics=("arbitrary", "arbitrary"), vmem_limit_bytes=VMEM_LIMIT),
        name="rwkv",
    )(rw, *small)


def _outffn_kernel(x_ref, of_ref, or_ref, wof_ref, wor_ref, gpost_ref, gfpre_ref, gfpost_ref,
                   wg_ref, wu_ref, wd_ref, o_ref, *, n_sub):
    tm = x_ref.shape[0]
    sub = tm // n_sub
    parts = [slice(i * sub, (i + 1) * sub) for i in range(n_sub)]
    mix = [_dot(of_ref[sl, :], wof_ref[...]) + _dot(or_ref[sl, :], wor_ref[...]) for sl in parts]
    h = [x_ref[sl, :] + _rms(m) * gpost_ref[...] for sl, m in zip(parts, mix)]
    z = [(_rms(hh) * gfpre_ref[...]).astype(BF16) for hh in h]
    gate = [_dot(zz, wg_ref[...]) for zz in z]
    up = [_dot(zz, wu_ref[...]) for zz in z]
    act = [(g * _sigmoid(g) * u).astype(BF16) for g, u in zip(gate, up)]
    f = [_dot(a, wd_ref[...]) for a in act]
    for sl, hh, ff in zip(parts, h, f):
        o_ref[sl, :] = hh + _rms(ff) * gfpost_ref[...]


def _outffn(x2, o_fox, o_rw, wof, wor, g_post, g_fpre, g_fpost, wg, wu, wd, *, tm, n_sub):
    N, D = x2.shape
    assert N % tm == 0 and tm % n_sub == 0
    consts = [wof, wor, g_post, g_fpre, g_fpost, wg, wu, wd]
    return pl.pallas_call(
        functools.partial(_outffn_kernel, n_sub=n_sub),
        grid=(N // tm,),
        in_specs=[
            pl.BlockSpec((tm, D), lambda i: (i, 0)),
            pl.BlockSpec((tm, o_fox.shape[1]), lambda i: (i, 0)),
            pl.BlockSpec((tm, o_rw.shape[1]), lambda i: (i, 0)),
        ] + [_const_spec(a.shape) for a in consts],
        out_specs=pl.BlockSpec((tm, D), lambda i: (i, 0)),
        out_shape=jax.ShapeDtypeStruct((N, D), F32),
        compiler_params=pltpu.CompilerParams(
            dimension_semantics=("arbitrary",), vmem_limit_bytes=VMEM_LIMIT),
        name="outffn",
    )(x2, o_fox, o_rw, *consts)


def _pad_cols(a, n):
    return jnp.pad(a, ((0, 0), (0, n - a.shape[1])))


def _pad_rows(a, n):
    return jnp.pad(a, ((0, n - a.shape[0]), (0, 0)))


def _block_layer(h, attn_norm_pre, attn_norm_post, w_in, fox_forget_bias, shift_mu, rwkv_w0,
                 rwkv_w_up, rwkv_a0, rwkv_a_up, rwkv_g_up, rwkv_k_k, rwkv_k_a, rwkv_r_k,
                 rwkv_ln_w, rwkv_ln_b, w_out, ffn_norm_pre, ffn_norm_post, ffn_w_gate, ffn_w_up,
                 ffn_w_down):
    B, T, D = h.shape
    fw = D // 2
    rwid = D // 2
    nh = fw // HEAD_DIM
    hw = nh * LANES

    fq, fk, fv = w_in[:, 0:fw], w_in[:, fw:2 * fw], w_in[:, 2 * fw:3 * fw]
    ffw = w_in[:, 3 * fw:3 * fw + nh]
    rcol = 3 * fw + nh

    def rw_layout(a, spare):
        o = 3 * rwid
        return jnp.concatenate([
            a[:, 0:o],
            _pad_cols(jnp.concatenate([a[:, o:o + DECAY_LORA], spare], axis=1), LANES),
            _pad_cols(a[:, o + DECAY_LORA:o + DECAY_LORA + AAA_LORA], LANES),
            _pad_cols(a[:, o + DECAY_LORA + AAA_LORA:], 2 * LANES)], axis=1)

    w_rw = rw_layout(w_in[:, rcol:], ffw)
    rw_w = w_rw.shape[1]
    wp = jnp.concatenate([fq, fk, fv, w_rw], axis=1).astype(BF16)
    mu = rw_layout(shift_mu[None, :], jnp.zeros((1, nh), F32))
    fbias = _pad_cols(jnp.pad(fox_forget_bias[None, :], ((0, 0), (FORGET_LANE, 0))), LANES)

    tm1 = 512
    ii = jnp.arange(tm1)
    tri = (ii[:, None] >= ii[None, :]).astype(BF16)
    bias_lane = lambda hd, p: hd * LANES + (HEAD_DIM if hd % 2 == 0 else 0) + 3 + p
    sel = np.zeros((LANES, hw), np.float32)
    qconst = np.zeros((1, hw), np.float32)
    for hd in range(nh):
        for p in range(3):
            sel[FORGET_LANE + hd + 8 * p, bias_lane(hd, p)] = 1.0
            qconst[0, bias_lane(hd, p)] = 1.0
    sel = jnp.asarray(sel, BF16)
    qconst = jnp.asarray(qconst)

    q_aug, k_aug, v, rw = _inproj(h, attn_norm_pre[None, :], wp, fbias, tri, sel, qconst, mu,
                                  n_heads=nh, rw_w=rw_w, f_col=3 * rwid, tm=tm1)

    o_fox = _fox(q_aug, k_aug, v, tq=1024)

    tc = 128
    hh = jnp.arange(rwid) // HEAD_DIM
    ones_blk = (hh[:, None] == hh[None, :]).astype(BF16)
    jj = jnp.arange(tc)
    tri_blk = ((jj[:, None] >= jj[None, :]) & (jj[:, None] // CHUNK == jj[None, :] // CHUNK)).astype(BF16)
    row = lambda a: a.reshape(1, -1)
    o_rw = _rwkv(rw, row(rwkv_w0), row(rwkv_a0), row(rwkv_k_k), row(rwkv_k_a), row(rwkv_r_k),
                 row(rwkv_ln_w), row(rwkv_ln_b),
                 _pad_rows(rwkv_w_up, LANES).astype(BF16), _pad_rows(rwkv_a_up, LANES).astype(BF16),
                 _pad_rows(rwkv_g_up, 2 * LANES).astype(BF16), ones_blk, tri_blk,
                 width=rwid, tc=tc, n_seq=4)

    out = _outffn(h.reshape(B * T, D), o_fox.reshape(B * T, fw), o_rw.reshape(B * T, rwid),
                  w_out[0:fw].astype(BF16), w_out[fw:].astype(BF16),
                  row(attn_norm_post), row(ffn_norm_pre), row(ffn_norm_post),
                  ffn_w_gate.astype(BF16), ffn_w_up.astype(BF16), ffn_w_down.astype(BF16),
                  tm=512, n_sub=2)
    return out.reshape(B, T, D)


def kernel(x, attn_norm_pre, attn_norm_post, w_in, fox_forget_bias, shift_mu, rwkv_w0, rwkv_w_up,
           rwkv_a0, rwkv_a_up, rwkv_g_up, rwkv_k_k, rwkv_k_a, rwkv_r_k, rwkv_ln_w, rwkv_ln_b, w_out,
           ffn_norm_pre, ffn_norm_post, ffn_w_gate, ffn_w_up, ffn_w_down):
    h = x
    for l in range(attn_norm_pre.shape[0]):
        h = _block_layer(h, attn_norm_pre[l], attn_norm_post[l], w_in[l], fox_forget_bias[l],
                         shift_mu[l], rwkv_w0[l], rwkv_w_up[l], rwkv_a0[l], rwkv_a_up[l],
                         rwkv_g_up[l], rwkv_k_k[l], rwkv_k_a[l], rwkv_r_k[l], rwkv_ln_w[l],
                         rwkv_ln_b[l], w_out[l], ffn_norm_pre[l], ffn_norm_post[l], ffn_w_gate[l],
                         ffn_w_up[l], ffn_w_down[l])
    return h
```

```python
import functools

import jax
import jax.numpy as jnp
import numpy as np
from jax import lax
from jax.experimental import pallas as pl
from jax.experimental.pallas import tpu as pltpu

F32 = jnp.float32
BF16 = jnp.bfloat16

HEAD_DIM = 64
LANES = 128
NORM_EPS = 1e-6
LNX_EPS = 64e-5
LOG2E = 1.4426950408889634
DECAY_LORA = 64
AAA_LORA = 64
GATE_LORA = 160
CHUNK = 64
GROUP_HEADS = 4
GROUP_W = GROUP_HEADS * HEAD_DIM
FORGET_LANE = DECAY_LORA
VMEM_LIMIT = 56 * 1024 * 1024


def _dot(a, b):
    return jnp.dot(a, b, preferred_element_type=F32)


def _dot_nt(a, b):
    return lax.dot_general(a, b, (((1,), (1,)), ((), ())), preferred_element_type=F32)


def _dot_tn(a, b):
    return lax.dot_general(a, b, (((0,), (0,)), ((), ())), preferred_element_type=F32)


def _split2(x):
    hi = x.astype(BF16)
    lo = (x - hi.astype(F32)).astype(BF16)
    return hi, lo


def _split3(x):
    hi = x.astype(BF16)
    r = x - hi.astype(F32)
    mid = r.astype(BF16)
    lo = (r - mid.astype(F32)).astype(BF16)
    return hi, mid, lo


def _dot_exact_lhs01(ones01, x):
    hi, mid, lo = _split3(x)
    return _dot(ones01, hi) + (_dot(ones01, mid) + _dot(ones01, lo))


def _rms(x):
    return x * lax.rsqrt(jnp.mean(x * x, axis=-1, keepdims=True) + NORM_EPS)


def _softplus(x):
    return jnp.maximum(x, 0.0) + jnp.log1p(jnp.exp(-jnp.abs(x)))


def _sigmoid(x):
    return 1.0 / (1.0 + jnp.exp(-x))


def _const_spec(shape):
    nd = len(shape)
    return pl.BlockSpec(shape, lambda *_: (0,) * nd, pipeline_mode=pl.Buffered(1))


def _inproj_kernel(x_ref, g_ref, w_ref, fb_ref, tri_ref, sel_ref, qc_ref, mu_ref,
                   q_out, k_out, v_out, rw_out, carry_ref, rwlast_ref, *, n_heads, rw_w, f_col):
    t = pl.program_id(1)

    @pl.when(t == 0)
    def _():
        carry_ref[...] = jnp.zeros_like(carry_ref)
        rwlast_ref[...] = jnp.zeros_like(rwlast_ref)

    fw = n_heads * HEAD_DIM
    tm = x_ref.shape[1]
    u = (_rms(x_ref[0]) * g_ref[...]).astype(BF16)
    lane = lax.broadcasted_iota(jnp.int32, (tm, LANES), 1)
    lower = lane < HEAD_DIM

    rw = _dot(u, w_ref[:, 3 * fw:3 * fw + rw_w])
    row0 = lax.broadcasted_iota(jnp.int32, (tm, 1), 0) == 0
    rw_prev = jnp.where(row0, rwlast_ref[...], pltpu.roll(rw, 1, 0))
    rwlast_ref[...] = rw[tm - 1:tm, :]
    rw_out[0] = rw + mu_ref[...] * (rw_prev - rw)

    fo = f_col
    in_f = (lane >= FORGET_LANE) & (lane < FORGET_LANE + n_heads)

    def pack3(x):
        hi = x.astype(BF16).astype(F32)
        r1 = x - hi
        mid = r1.astype(BF16).astype(F32)
        lo = r1 - mid
        keep = lambda a: jnp.where(in_f, a, 0.0)
        return (keep(hi) + pltpu.roll(keep(mid), 8, 1) + pltpu.roll(keep(lo), 16, 1)).astype(BF16)

    def unpack3(x):
        return x + pltpu.roll(x, LANES - 8, 1) + pltpu.roll(x, LANES - 16, 1)

    logf = -_softplus(-(rw[:, fo:fo + LANES] + fb_ref[...]))
    c = unpack3(_dot(tri_ref[...], pack3(logf))) + carry_ref[...]
    carry_ref[...] = c[tm - 1:tm, :]

    qk = _dot(u, w_ref[:, 0:2 * fw])
    k_bias = _dot(pack3(c * (-LOG2E)), sel_ref[...])
    scale = HEAD_DIM ** -0.5 * LOG2E
    for hp in range(n_heads // 2):
        qb = qk[:, hp * LANES:(hp + 1) * LANES] * scale
        kb = qk[:, fw + hp * LANES:fw + (hp + 1) * LANES]
        for h, keep in ((2 * hp, lower), (2 * hp + 1, jnp.logical_not(lower))):
            q_out[0, h] = jnp.where(keep, qb, qc_ref[:, h * LANES:(h + 1) * LANES]).astype(BF16)
            k_out[0, h] = jnp.where(keep, kb, k_bias[:, h * LANES:(h + 1) * LANES]).astype(BF16)

    v = _dot(u, w_ref[:, 2 * fw:3 * fw])
    one_even = jnp.where(lane == HEAD_DIM, 1.0, 0.0)
    one_odd = jnp.where(lane == 0, 1.0, 0.0)
    for hp in range(n_heads // 2):
        blk = v[:, hp * LANES:(hp + 1) * LANES]
        v_out[0, 2 * hp] = jnp.where(lower, blk, one_even).astype(BF16)
        v_out[0, 2 * hp + 1] = jnp.where(lower, one_odd, blk).astype(BF16)


def _inproj(x, g_pre, wp, fbias, tri, sel, qconst, mu, *, n_heads, rw_w, f_col, tm):
    B, T, D = x.shape
    assert T % tm == 0 and n_heads % 2 == 0 and tm == tri.shape[0]
    kern = functools.partial(_inproj_kernel, n_heads=n_heads, rw_w=rw_w, f_col=f_col)
    return pl.pallas_call(
        kern,
        grid=(B, T // tm),
        in_specs=[
            pl.BlockSpec((1, tm, D), lambda b, t: (b, t, 0)),
            _const_spec(g_pre.shape),
            _const_spec(wp.shape),
            _const_spec(fbias.shape),
            _const_spec(tri.shape),
            _const_spec(sel.shape),
            _const_spec(qconst.shape),
            _const_spec(mu.shape),
        ],
        out_specs=[
            pl.BlockSpec((1, n_heads, tm, LANES), lambda b, t: (b, 0, t, 0)),
            pl.BlockSpec((1, n_heads, tm, LANES), lambda b, t: (b, 0, t, 0)),
            pl.BlockSpec((1, n_heads, tm, LANES), lambda b, t: (b, 0, t, 0)),
            pl.BlockSpec((1, tm, rw_w), lambda b, t: (b, t, 0)),
        ],
        out_shape=[
            jax.ShapeDtypeStruct((B, n_heads, T, LANES), BF16),
            jax.ShapeDtypeStruct((B, n_heads, T, LANES), BF16),
            jax.ShapeDtypeStruct((B, n_heads, T, LANES), BF16),
            jax.ShapeDtypeStruct((B, T, rw_w), F32),
        ],
        scratch_shapes=[pltpu.VMEM((1, LANES), F32),
                        pltpu.VMEM((1, rw_w), F32)],
        compiler_params=pltpu.CompilerParams(
            dimension_semantics=("arbitrary", "arbitrary"), vmem_limit_bytes=VMEM_LIMIT),
        name="inproj",
    )(x, g_pre, wp, fbias, tri, sel, qconst, mu)


def _fox_kernel(q_ref, k_ref, v_ref, o_ref, m_scr, acc_scr, *, tq):
    qi = pl.program_id(2)
    heads = range(2)
    half = tq // 2

    def step(jobs):
        todo = [(n, h) for n in range(len(jobs)) for h in heads]
        qrows = [pl.ds(q0, nq) for q0, nq, _, _, _ in jobs]
        krows = [pl.ds(pl.multiple_of(k0, tk), tk) for _, _, k0, tk, _ in jobs]
        s = {(n, h): _dot_nt(q_ref[0, h, qrows[n], :], k_ref[0, h, krows[n], :]) for n, h in todo}
        for n, (_, nq, _, tk, masked) in enumerate(jobs):
            if masked:
                row = lax.broadcasted_iota(jnp.int32, (nq, tk), 0)
                col = lax.broadcasted_iota(jnp.int32, (nq, tk), 1)
                for h in heads:
                    s[n, h] = jnp.where(col <= row + (tk - nq), s[n, h], -jnp.inf)
        m_prev = {(n, h): m_scr[h, qrows[n], :] for n, h in todo}
        m_next = {nh: jnp.maximum(m_prev[nh], jnp.max(s[nh], axis=1, keepdims=True)) for nh in todo}
        p = {(n, h): jnp.exp2(s[n, h] - jnp.concatenate([m_next[n, h]] * (jobs[n][3] // LANES), axis=1)
                              ).astype(BF16) for n, h in todo}
        for n, h in todo:
            alpha = jnp.exp2(m_prev[n, h] - m_next[n, h])
            acc_scr[h, qrows[n], :] = (alpha * acc_scr[h, qrows[n], :]
                                       + _dot(p[n, h], v_ref[0, h, krows[n], :]))
            m_scr[h, qrows[n], :] = m_next[n, h]

    for h in heads:
        m_scr[h] = jnp.full(m_scr.shape[1:], -jnp.inf, F32)
        acc_scr[h] = jnp.zeros(acc_scr.shape[1:], F32)

    def body(j, carry):
        step([(0, tq, j * tq, tq, False)])
        return carry

    lax.fori_loop(0, qi, body, 0)
    step([(0, half, qi * tq, half, True), (half, half, qi * tq, tq, True)])

    lane = lax.broadcasted_iota(jnp.int32, acc_scr.shape[1:], 1)
    acc0 = acc_scr[0]
    acc1 = acc_scr[1]
    o = jnp.where(lane < HEAD_DIM, acc0 / acc0[:, HEAD_DIM:HEAD_DIM + 1], acc1 / acc1[:, 0:1])
    o_ref[0] = o.astype(o_ref.dtype)


def _fox(q_aug, k_aug, v_aug, *, tq):
    B, H, T, _ = q_aug.shape
    assert T % tq == 0 and H % 2 == 0
    kern = functools.partial(_fox_kernel, tq=tq)
    return pl.pallas_call(
        kern,
        grid=(B, H // 2, T // tq),
        in_specs=[
            pl.BlockSpec((1, 2, tq, LANES), lambda b, hp, qi: (b, hp, qi, 0)),
            pl.BlockSpec((1, 2, T, LANES), lambda b, hp, qi: (b, hp, 0, 0)),
            pl.BlockSpec((1, 2, T, LANES), lambda b, hp, qi: (b, hp, 0, 0)),
        ],
        out_specs=pl.BlockSpec((1, tq, LANES), lambda b, hp, qi: (b, qi, hp)),
        out_shape=jax.ShapeDtypeStruct((B, T, H * HEAD_DIM), BF16),
        scratch_shapes=[pltpu.VMEM((2, tq, LANES), F32)] * 2,
        compiler_params=pltpu.CompilerParams(
            dimension_semantics=("arbitrary", "arbitrary", "arbitrary"),
            vmem_limit_bytes=VMEM_LIMIT),
        name="fox",
    )(q_aug, k_aug, v_aug)


def _unit_lower_inverse(Ls, stack):
    shape = Ls[0].shape
    t = lax.broadcasted_iota(jnp.int32, shape, 0)
    j = lax.broadcasted_iota(jnp.int32, shape, 1) % shape[0]
    same = (t >> 1) == (j >> 1)
    eye = jnp.where(t == j, 1.0, 0.0)
    ps = [eye + jnp.where(same, L, 0.0) for L in Ls]
    for s in range(2, 7):
        merged = (t >> s) == (j >> s)
        sel = merged & jnp.logical_not(same)
        offs = [stack(jnp.where(sel, L, 0.0).astype(BF16)) for L in Ls]
        pbs = [p.astype(BF16) for p in ps]
        tmps = [_dot(pb, off).astype(BF16) for pb, off in zip(pbs, offs)]
        ps = [p + _dot(tmp, stack(pb)) for p, tmp, pb in zip(ps, tmps, pbs)]
        same = merged
    return ps


def _rwkv_kernel(rw_ref, w0_ref, a0_ref, kk_ref, ka_ref, rk_ref,
                 lnw_ref, lnb_ref, wup_ref, aup_ref, gup_ref, ones_ref, tri_ref,
                 o_ref, s_scr, y_scr, *, width):
    t = pl.program_id(1)
    n_seq, tc, rw_w = rw_ref.shape
    n_rows = n_seq * tc
    n_chunks = tc // CHUNK
    n_groups = width // GROUP_W

    @pl.when(t == 0)
    def _():
        s_scr[...] = jnp.zeros_like(s_scr)

    ones_blk = ones_ref[...]

    def segsum(x):
        return _dot(x.astype(BF16), ones_blk)

    p = rw_ref[...].reshape(n_rows, rw_w)
    r = p[:, 0:width]
    k = p[:, width:2 * width]
    v = p[:, 2 * width:3 * width]
    o = 3 * width
    w_lat = p[:, o:o + LANES]
    a_lat = p[:, o + LANES:o + 2 * LANES]
    g_lat = p[:, o + 2 * LANES:o + 4 * LANES]

    z = w0_ref[...] + _dot(jnp.tanh(w_lat).astype(BF16), wup_ref[...])
    g_log = _sigmoid(z) * (-(2.718281828459045 ** -0.5))
    lr = _sigmoid(a0_ref[...] + _dot(a_lat.astype(BF16), aup_ref[...]))
    gate = _dot(_sigmoid(g_lat).astype(BF16), gup_ref[...])
    kk = k * kk_ref[...]
    kk = kk * lax.rsqrt(jnp.maximum(segsum(kk * kk), 1e-24))
    k = k * (1.0 + (lr - 1.0) * ka_ref[...])
    bonus = segsum(r * k * rk_ref[...]) * v

    tri = tri_ref[...]
    G = jnp.concatenate([_dot_exact_lhs01(tri, g_log[s * tc:(s + 1) * tc]) for s in range(n_seq)], axis=0)
    e_pos = jnp.exp(G)
    e_neg = jnp.exp(-G)
    a_t = (-kk * jnp.exp(G - g_log)).astype(BF16)
    b_t = (kk * lr * e_neg).astype(BF16)
    r_t = (r * e_pos).astype(BF16)
    k_t = (k * e_neg).astype(BF16)
    v_t = v.astype(BF16)

    lane_head = lax.broadcasted_iota(jnp.int32, (CHUNK, GROUP_W), 1) // HEAD_DIM
    head_masks = [lane_head == h for h in range(GROUP_HEADS)]
    rt = lax.broadcasted_iota(jnp.int32, (GROUP_W, GROUP_W), 0)
    ct = lax.broadcasted_iota(jnp.int32, (GROUP_W, GROUP_W), 1)
    same_head = (rt // HEAD_DIM) == (ct // HEAD_DIM)
    diag = rt == ct
    tw = lax.broadcasted_iota(jnp.int32, (CHUNK, GROUP_W), 0)
    jw = lax.broadcasted_iota(jnp.int32, (CHUNK, GROUP_W), 1) % CHUNK
    strict_w = tw > jw
    incl_w = tw >= jw
    rows = lambda s, ci: slice((s * n_chunks + ci) * CHUNK, (s * n_chunks + ci + 1) * CHUNK)
    cols = lambda g: slice(g * GROUP_W, (g + 1) * GROUP_W)
    wide = lambda x, u: x[rows(u[0], u[1]), cols(u[2])]

    def stack(x):
        zero = jnp.zeros_like(x)
        return jnp.concatenate([jnp.where(mk, x, zero) for mk in head_masks], axis=0)

    units = [(s, ci, g) for ci in range(n_chunks) for s in range(n_seq) for g in range(n_groups)]
    a_s = {u: stack(wide(a_t, u)) for u in units}
    b_s = {u: stack(wide(b_t, u)) for u in units}
    k_s = {u: stack(wide(k_t, u)) for u in units}
    v_s = {u: stack(wide(v_t, u)) for u in units}
    l_all = {u: _dot_nt(jnp.concatenate([wide(a_t, u), wide(r_t, u)], axis=0),
                        jnp.concatenate([b_s[u], k_s[u]], axis=0)) for u in units}
    l_ab = {u: jnp.where(strict_w, l_all[u][0:CHUNK, 0:GROUP_W], 0.0) for u in units}
    l_ak = {u: jnp.where(strict_w, l_all[u][0:CHUNK, GROUP_W:], 0.0).astype(BF16) for u in units}
    l_rb = {u: jnp.where(incl_w, l_all[u][CHUNK:, 0:GROUP_W], 0.0).astype(BF16) for u in units}
    l_rk = {u: jnp.where(incl_w, l_all[u][CHUNK:, GROUP_W:], 0.0).astype(BF16) for u in units}
    av = {u: _dot(jnp.concatenate([l_ak[u], l_rk[u]], axis=0), v_s[u]) for u in units}
    akv = {u: stack(av[u][0:CHUNK].astype(BF16)) for u in units}
    y_rk = {u: av[u][CHUNK:] for u in units}
    t_inv = _unit_lower_inverse([l_ab[u] for u in units], stack)
    wu = {u: _dot(t_inv[n].astype(BF16), jnp.concatenate([a_s[u], akv[u]], axis=1))
          for n, u in enumerate(units)}
    last = lambda u: (u[0] * n_chunks + u[1] + 1) * CHUNK - 1
    gc = {u: e_pos[last(u):last(u) + 1, cols(u[2])] for u in units}
    gcol = {u: jnp.sum(jnp.where(diag, jnp.broadcast_to(gc[u], (GROUP_W, GROUP_W)), 0.0),
                       axis=1, keepdims=True) for u in units}
    bk_h = {u: jnp.concatenate([(wide(b_t, u).astype(F32) * gc[u]).astype(BF16),
                                (wide(k_t, u).astype(F32) * gc[u]).astype(BF16)], axis=0) for u in units}

    chains = [(s, g) for s in range(n_seq) for g in range(n_groups)]
    state = {c: s_scr[c[0] * n_groups + c[1]] for c in chains}
    for ci in range(n_chunks):
        level = [(s, ci, g) for s, g in chains]
        zr = {u: _dot(jnp.concatenate([wu[u][:, 0:GROUP_W].astype(BF16), wide(r_t, u)], axis=0),
                      state[u[0], u[2]].astype(BF16)) for u in level}
        z_b = {u: (zr[u][0:CHUNK] + wu[u][:, GROUP_W:]).astype(BF16) for u in level}
        upd = {u: _dot_tn(bk_h[u], jnp.concatenate([z_b[u], wide(v_t, u)], axis=0)) for u in level}
        for u in level:
            state[u[0], u[2]] = state[u[0], u[2]] * gcol[u] + jnp.where(same_head, upd[u], 0.0)
        for u in level:
            y_scr[rows(u[0], ci), cols(u[2])] = (zr[u][CHUNK:] + _dot(l_rb[u], stack(z_b[u]))) + y_rk[u]
    for c in chains:
        s_scr[c[0] * n_groups + c[1]] = state[c]

    y = y_scr[...]
    d = y - segsum(y) * (1.0 / HEAD_DIM)
    var = segsum(d * d) * (1.0 / HEAD_DIM)
    yn = d * lax.rsqrt(var + LNX_EPS) * lnw_ref[...] + lnb_ref[...]
    o_ref[...] = ((yn + bonus) * gate).astype(o_ref.dtype).reshape(n_seq, tc, width)


def _rwkv(rw, w0, a0, k_k, k_a, r_k, ln_w, ln_b, w_up, a_up, g_up, ones_blk, tri_blk, *,
          width, tc, n_seq):
    B, T, rw_w = rw.shape
    assert B % n_seq == 0 and T % tc == 0 and tc % CHUNK == 0 and width % GROUP_W == 0
    n_groups = width // GROUP_W
    kern = functools.partial(_rwkv_kernel, width=width)
    small = [w0, a0, k_k, k_a, r_k, ln_w, ln_b, w_up, a_up, g_up, ones_blk, tri_blk]
    return pl.pallas_call(
        kern,
        grid=(B // n_seq, T // tc),
        in_specs=[
            pl.BlockSpec((n_seq, tc, rw_w), lambda b, t: (b, t, 0)),
        ] + [_const_spec(a.shape) for a in small],
        out_specs=pl.BlockSpec((n_seq, tc, width), lambda b, t: (b, t, 0)),
        out_shape=jax.ShapeDtypeStruct((B, T, width), BF16),
        scratch_shapes=[
            pltpu.VMEM((n_seq * n_groups, GROUP_W, GROUP_W), F32),
            pltpu.VMEM((n_seq * tc, width), F32),
        ],
        compiler_params=pltpu.CompilerParams(
            dimension_semantics=("arbitrary", "arbitrary"), vmem_limit_bytes=VMEM_LIMIT),
        name="rwkv",
    )(rw, *small)


def _outffn_kernel(x_ref, of_ref, or_ref, wof_ref, wor_ref, gpost_ref, gfpre_ref, gfpost_ref,
                   wg_ref, wu_ref, wd_ref, o_ref, *, n_sub):
    tm = x_ref.shape[0]
    sub = tm // n_sub
    parts = [slice(i * sub, (i + 1) * sub) for i in range(n_sub)]
    mix = [_dot(of_ref[sl, :], wof_ref[...]) + _dot(or_ref[sl, :], wor_ref[...]) for sl in parts]
    h = [x_ref[sl, :] + _rms(m) * gpost_ref[...] for sl, m in zip(parts, mix)]
    z = [(_rms(hh) * gfpre_ref[...]).astype(BF16) for hh in h]
    gate = [_dot(zz, wg_ref[...]) for zz in z]
    up = [_dot(zz, wu_ref[...]) for zz in z]
    act = [(g * _sigmoid(g) * u).astype(BF16) for g, u in zip(gate, up)]
    f = [_dot(a, wd_ref[...]) for a in act]
    for sl, hh, ff in zip(parts, h, f):
        o_ref[sl, :] = hh + _rms(ff) * gfpost_ref[...]


def _outffn(x2, o_fox, o_rw, wof, wor, g_post, g_fpre, g_fpost, wg, wu, wd, *, tm, n_sub):
    N, D = x2.shape
    assert N % tm == 0 and tm % n_sub == 0
    consts = [wof, wor, g_post, g_fpre, g_fpost, wg, wu, wd]
    return pl.pallas_call(
        functools.partial(_outffn_kernel, n_sub=n_sub),
        grid=(N // tm,),
        in_specs=[
            pl.BlockSpec((tm, D), lambda i: (i, 0)),
            pl.BlockSpec((tm, o_fox.shape[1]), lambda i: (i, 0)),
            pl.BlockSpec((tm, o_rw.shape[1]), lambda i: (i, 0)),
        ] + [_const_spec(a.shape) for a in consts],
        out_specs=pl.BlockSpec((tm, D), lambda i: (i, 0)),
        out_shape=jax.ShapeDtypeStruct((N, D), F32),
        compiler_params=pltpu.CompilerParams(
            dimension_semantics=("arbitrary",), vmem_limit_bytes=VMEM_LIMIT),
        name="outffn",
    )(x2, o_fox, o_rw, *consts)


def _pad_cols(a, n):
    return jnp.pad(a, ((0, 0), (0, n - a.shape[1])))


def _pad_rows(a, n):
    return jnp.pad(a, ((0, n - a.shape[0]), (0, 0)))


def _block_layer(h, attn_norm_pre, attn_norm_post, w_in, fox_forget_bias, shift_mu, rwkv_w0,
                 rwkv_w_up, rwkv_a0, rwkv_a_up, rwkv_g_up, rwkv_k_k, rwkv_k_a, rwkv_r_k,
                 rwkv_ln_w, rwkv_ln_b, w_out, ffn_norm_pre, ffn_norm_post, ffn_w_gate, ffn_w_up,
                 ffn_w_down):
    B, T, D = h.shape
    fw = D // 2
    rwid = D // 2
    nh = fw // HEAD_DIM
    hw = nh * LANES

    fq, fk, fv = w_in[:, 0:fw], w_in[:, fw:2 * fw], w_in[:, 2 * fw:3 * fw]
    ffw = w_in[:, 3 * fw:3 * fw + nh]
    rcol = 3 * fw + nh

    def rw_layout(a, spare):
        o = 3 * rwid
        return jnp.concatenate([
            a[:, 0:o],
            _pad_cols(jnp.concatenate([a[:, o:o + DECAY_LORA], spare], axis=1), LANES),
            _pad_cols(a[:, o + DECAY_LORA:o + DECAY_LORA + AAA_LORA], LANES),
            _pad_cols(a[:, o + DECAY_LORA + AAA_LORA:], 2 * LANES)], axis=1)

    w_rw = rw_layout(w_in[:, rcol:], ffw)
    rw_w = w_rw.shape[1]
    wp = jnp.concatenate([fq, fk, fv, w_rw], axis=1).astype(BF16)
    mu = rw_layout(shift_mu[None, :], jnp.zeros((1, nh), F32))
    fbias = _pad_cols(jnp.pad(fox_forget_bias[None, :], ((0, 0), (FORGET_LANE, 0))), LANES)

    tm1 = 512
    ii = jnp.arange(tm1)
    tri = (ii[:, None] >= ii[None, :]).astype(BF16)
    bias_lane = lambda hd, p: hd * LANES + (HEAD_DIM if hd % 2 == 0 else 0) + 3 + p
    sel = np.zeros((LANES, hw), np.float32)
    qconst = np.zeros((1, hw), np.float32)
    for hd in range(nh):
        for p in range(3):
            sel[FORGET_LANE + hd + 8 * p, bias_lane(hd, p)] = 1.0
            qconst[0, bias_lane(hd, p)] = 1.0
    sel = jnp.asarray(sel, BF16)
    qconst = jnp.asarray(qconst)

    q_aug, k_aug, v, rw = _inproj(h, attn_norm_pre[None, :], wp, fbias, tri, sel, qconst, mu,
                                  n_heads=nh, rw_w=rw_w, f_col=3 * rwid, tm=tm1)

    o_fox = _fox(q_aug, k_aug, v, tq=1024)

    tc = 128
    hh = jnp.arange(rwid) // HEAD_DIM
    ones_blk = (hh[:, None] == hh[None, :]).astype(BF16)
    jj = jnp.arange(tc)
    tri_blk = ((jj[:, None] >= jj[None, :]) & (jj[:, None] // CHUNK == jj[None, :] // CHUNK)).astype(BF16)
    row = lambda a: a.reshape(1, -1)
    o_rw = _rwkv(rw, row(rwkv_w0), row(rwkv_a0), row(rwkv_k_k), row(rwkv_k_a), row(rwkv_r_k),
                 row(rwkv_ln_w), row(rwkv_ln_b),
                 _pad_rows(rwkv_w_up, LANES).astype(BF16), _pad_rows(rwkv_a_up, LANES).astype(BF16),
                 _pad_rows(rwkv_g_up, 2 * LANES).astype(BF16), ones_blk, tri_blk,
                 width=rwid, tc=tc, n_seq=4)

    out = _outffn(h.reshape(B * T, D), o_fox.reshape(B * T, fw), o_rw.reshape(B * T, rwid),
                  w_out[0:fw].astype(BF16), w_out[fw:].astype(BF16),
                  row(attn_norm_post), row(ffn_norm_pre), row(ffn_norm_post),
                  ffn_w_gate.astype(BF16), ffn_w_up.astype(BF16), ffn_w_down.astype(BF16),
                  tm=512, n_sub=2)
    return out.reshape(B, T, D)


def kernel(x, attn_norm_pre, attn_norm_post, w_in, fox_forget_bias, shift_mu, rwkv_w0, rwkv_w_up,
           rwkv_a0, rwkv_a_up, rwkv_g_up, rwkv_k_k, rwkv_k_a, rwkv_r_k, rwkv_ln_w, rwkv_ln_b, w_out,
           ffn_norm_pre, ffn_norm_post, ffn_w_gate, ffn_w_up, ffn_w_down):
    h = x
    for l in range(attn_norm_pre.shape[0]):
        h = _block_layer(h, attn_norm_pre[l], attn_norm_post[l], w_in[l], fox_forget_bias[l],
                         shift_mu[l], rwkv_w0[l], rwkv_w_up[l], rwkv_a0[l], rwkv_a_up[l],
                         rwkv_g_up[l], rwkv_k_k[l], rwkv_k_a[l], rwkv_r_k[l], rwkv_ln_w[l],
                         rwkv_ln_b[l], w_out[l], ffn_norm_pre[l], ffn_norm_post[l], ffn_w_gate[l],
                         ffn_w_up[l], ffn_w_down[l])
    return h
```

```python
import functools

import jax
import jax.numpy as jnp
import numpy as np
from jax import lax
from jax.experimental import pallas as pl
from jax.experimental.pallas import tpu as pltpu

F32 = jnp.float32
BF16 = jnp.bfloat16

HEAD_DIM = 64
LANES = 128
NORM_EPS = 1e-6
LNX_EPS = 64e-5
LOG2E = 1.4426950408889634
DECAY_LORA = 64
AAA_LORA = 64
GATE_LORA = 160
CHUNK = 64
GROUP_HEADS = 4
GROUP_W = GROUP_HEADS * HEAD_DIM
FORGET_LANE = DECAY_LORA
VMEM_LIMIT = 56 * 1024 * 1024


def _dot(a, b):
    return jnp.dot(a, b, preferred_element_type=F32)


def _dot_nt(a, b):
    return lax.dot_general(a, b, (((1,), (1,)), ((), ())), preferred_element_type=F32)


def _dot_tn(a, b):
    return lax.dot_general(a, b, (((0,), (0,)), ((), ())), preferred_element_type=F32)


def _split2(x):
    hi = x.astype(BF16)
    lo = (x - hi.astype(F32)).astype(BF16)
    return hi, lo


def _split3(x):
    hi = x.astype(BF16)
    r = x - hi.astype(F32)
    mid = r.astype(BF16)
    lo = (r - mid.astype(F32)).astype(BF16)
    return hi, mid, lo


def _dot_exact_lhs01(ones01, x):
    hi, mid, lo = _split3(x)
    return _dot(ones01, hi) + (_dot(ones01, mid) + _dot(ones01, lo))


def _rms(x):
    return x * lax.rsqrt(jnp.mean(x * x, axis=-1, keepdims=True) + NORM_EPS)


def _softplus(x):
    return jnp.maximum(x, 0.0) + jnp.log1p(jnp.exp(-jnp.abs(x)))


def _sigmoid(x):
    return 0.5 * jnp.tanh(0.5 * x) + 0.5


def _const_spec(shape):
    nd = len(shape)
    return pl.BlockSpec(shape, lambda *_: (0,) * nd, pipeline_mode=pl.Buffered(1))


def _inproj_kernel(x_ref, g_ref, w_ref, fb_ref, tri_ref, sel_ref, qc_ref, mu_ref,
                   q_out, k_out, v_out, rw_out, carry_ref, rwlast_ref, *, n_heads, rw_w, f_col):
    t = pl.program_id(1)

    @pl.when(t == 0)
    def _():
        carry_ref[...] = jnp.zeros_like(carry_ref)
        rwlast_ref[...] = jnp.zeros_like(rwlast_ref)

    fw = n_heads * HEAD_DIM
    tm = x_ref.shape[1]
    u = (_rms(x_ref[0]) * g_ref[...]).astype(BF16)
    lane = lax.broadcasted_iota(jnp.int32, (tm, LANES), 1)
    lower = lane < HEAD_DIM

    rw = _dot(u, w_ref[:, 3 * fw:3 * fw + rw_w])
    row0 = lax.broadcasted_iota(jnp.int32, (tm, 1), 0) == 0
    rw_prev = jnp.where(row0, rwlast_ref[...], pltpu.roll(rw, 1, 0))
    rwlast_ref[...] = rw[tm - 1:tm, :]
    rw_out[0] = rw + mu_ref[...] * (rw_prev - rw)

    fo = f_col
    in_f = (lane >= FORGET_LANE) & (lane < FORGET_LANE + n_heads)

    def pack3(x):
        hi = x.astype(BF16).astype(F32)
        r1 = x - hi
        mid = r1.astype(BF16).astype(F32)
        lo = r1 - mid
        keep = lambda a: jnp.where(in_f, a, 0.0)
        return (keep(hi) + pltpu.roll(keep(mid), 8, 1) + pltpu.roll(keep(lo), 16, 1)).astype(BF16)

    def unpack3(x):
        return x + pltpu.roll(x, LANES - 8, 1) + pltpu.roll(x, LANES - 16, 1)

    logf = -_softplus(-(rw[:, fo:fo + LANES] + fb_ref[...]))
    c = unpack3(_dot(tri_ref[...], pack3(logf))) + carry_ref[...]
    carry_ref[...] = c[tm - 1:tm, :]

    qk = _dot(u, w_ref[:, 0:2 * fw])
    k_bias = _dot(pack3(c * (-LOG2E)), sel_ref[...])
    scale = HEAD_DIM ** -0.5 * LOG2E
    for hp in range(n_heads // 2):
        qb = qk[:, hp * LANES:(hp + 1) * LANES] * scale
        kb = qk[:, fw + hp * LANES:fw + (hp + 1) * LANES]
        for h, keep in ((2 * hp, lower), (2 * hp + 1, jnp.logical_not(lower))):
            q_out[0, h] = jnp.where(keep, qb, qc_ref[:, h * LANES:(h + 1) * LANES]).astype(BF16)
            k_out[0, h] = jnp.where(keep, kb, k_bias[:, h * LANES:(h + 1) * LANES]).astype(BF16)

    v = _dot(u, w_ref[:, 2 * fw:3 * fw])
    one_even = jnp.where(lane == HEAD_DIM, 1.0, 0.0)
    one_odd = jnp.where(lane == 0, 1.0, 0.0)
    for hp in range(n_heads // 2):
        blk = v[:, hp * LANES:(hp + 1) * LANES]
        v_out[0, 2 * hp] = jnp.where(lower, blk, one_even).astype(BF16)
        v_out[0, 2 * hp + 1] = jnp.where(lower, one_odd, blk).astype(BF16)


def _inproj(x, g_pre, wp, fbias, tri, sel, qconst, mu, *, n_heads, rw_w, f_col, tm):
    B, T, D = x.shape
    assert T % tm == 0 and n_heads % 2 == 0 and tm == tri.shape[0]
    kern = functools.partial(_inproj_kernel, n_heads=n_heads, rw_w=rw_w, f_col=f_col)
    return pl.pallas_call(
        kern,
        grid=(B, T // tm),
        in_specs=[
            pl.BlockSpec((1, tm, D), lambda b, t: (b, t, 0)),
            _const_spec(g_pre.shape),
            _const_spec(wp.shape),
            _const_spec(fbias.shape),
            _const_spec(tri.shape),
            _const_spec(sel.shape),
            _const_spec(qconst.shape),
            _const_spec(mu.shape),
        ],
        out_specs=[
            pl.BlockSpec((1, n_heads, tm, LANES), lambda b, t: (b, 0, t, 0)),
            pl.BlockSpec((1, n_heads, tm, LANES), lambda b, t: (b, 0, t, 0)),
            pl.BlockSpec((1, n_heads, tm, LANES), lambda b, t: (b, 0, t, 0)),
            pl.BlockSpec((1, tm, rw_w), lambda b, t: (b, t, 0)),
        ],
        out_shape=[
            jax.ShapeDtypeStruct((B, n_heads, T, LANES), BF16),
            jax.ShapeDtypeStruct((B, n_heads, T, LANES), BF16),
            jax.ShapeDtypeStruct((B, n_heads, T, LANES), BF16),
            jax.ShapeDtypeStruct((B, T, rw_w), F32),
        ],
        scratch_shapes=[pltpu.VMEM((1, LANES), F32),
                        pltpu.VMEM((1, rw_w), F32)],
        compiler_params=pltpu.CompilerParams(
            dimension_semantics=("arbitrary", "arbitrary"), vmem_limit_bytes=VMEM_LIMIT),
        name="inproj",
    )(x, g_pre, wp, fbias, tri, sel, qconst, mu)


def _fox_kernel(q_ref, k_ref, v_ref, o_ref, m_scr, acc_scr, *, tq):
    qi = pl.program_id(2)
    heads = range(2)
    half = tq // 2

    def step(jobs):
        todo = [(n, h) for n in range(len(jobs)) for h in heads]
        qrows = [pl.ds(q0, nq) for q0, nq, _, _, _ in jobs]
        krows = [pl.ds(pl.multiple_of(k0, tq), tk) for _, _, k0, tk, _ in jobs]
        s = {(n, h): _dot_nt(q_ref[0, h, qrows[n], :], k_ref[0, h, krows[n], :]) for n, h in todo}
        for n, (_, nq, _, tk, masked) in enumerate(jobs):
            if masked:
                row = lax.broadcasted_iota(jnp.int32, (nq, nq), 0)
                col = lax.broadcasted_iota(jnp.int32, (nq, nq), 1)
                for h in heads:
                    edge = jnp.where(col <= row, s[n, h][:, tk - nq:], -jnp.inf)
                    s[n, h] = edge if tk == nq else jnp.concatenate([s[n, h][:, :tk - nq], edge], axis=1)
        m_prev = {(n, h): m_scr[h, qrows[n], :] for n, h in todo}
        m_next = {nh: jnp.maximum(m_prev[nh], jnp.max(s[nh], axis=1, keepdims=True)) for nh in todo}
        p = {(n, h): jnp.exp2(s[n, h] - jnp.concatenate([m_next[n, h]] * (jobs[n][3] // LANES), axis=1)
                              ).astype(BF16) for n, h in todo}
        for n, h in todo:
            alpha = jnp.exp2(m_prev[n, h] - m_next[n, h])
            acc_scr[h, qrows[n], :] = (alpha * acc_scr[h, qrows[n], :]
                                       + _dot(p[n, h], v_ref[0, h, krows[n], :]))
            m_scr[h, qrows[n], :] = m_next[n, h]

    for h in heads:
        m_scr[h] = jnp.full(m_scr.shape[1:], -jnp.inf, F32)
        acc_scr[h] = jnp.zeros(acc_scr.shape[1:], F32)

    def body(j, carry):
        step([(0, tq, j * tq, tq, False)])
        return carry

    lax.fori_loop(0, qi, body, 0)
    quarter = tq // 4
    step([(n * quarter, quarter, qi * tq, (n + 1) * quarter, True) for n in range(4)])

    lane = lax.broadcasted_iota(jnp.int32, acc_scr.shape[1:], 1)
    acc0 = acc_scr[0]
    acc1 = acc_scr[1]
    o = jnp.where(lane < HEAD_DIM, acc0 / acc0[:, HEAD_DIM:HEAD_DIM + 1], acc1 / acc1[:, 0:1])
    o_ref[0] = o.astype(o_ref.dtype)


def _fox(q_aug, k_aug, v_aug, *, tq):
    B, H, T, _ = q_aug.shape
    assert T % tq == 0 and H % 2 == 0
    kern = functools.partial(_fox_kernel, tq=tq)
    return pl.pallas_call(
        kern,
        grid=(B, H // 2, T // tq),
        in_specs=[
            pl.BlockSpec((1, 2, tq, LANES), lambda b, hp, qi: (b, hp, qi, 0)),
            pl.BlockSpec((1, 2, T, LANES), lambda b, hp, qi: (b, hp, 0, 0)),
            pl.BlockSpec((1, 2, T, LANES), lambda b, hp, qi: (b, hp, 0, 0)),
        ],
        out_specs=pl.BlockSpec((1, tq, LANES), lambda b, hp, qi: (b, qi, hp)),
        out_shape=jax.ShapeDtypeStruct((B, T, H * HEAD_DIM), BF16),
        scratch_shapes=[pltpu.VMEM((2, tq, LANES), F32)] * 2,
        compiler_params=pltpu.CompilerParams(
            dimension_semantics=("arbitrary", "arbitrary", "arbitrary"),
            vmem_limit_bytes=VMEM_LIMIT),
        name="fox",
    )(q_aug, k_aug, v_aug)


def _unit_lower_inverse(Ls, stack):
    shape = Ls[0].shape
    t = lax.broadcasted_iota(jnp.int32, shape, 0)
    j = lax.broadcasted_iota(jnp.int32, shape, 1) % shape[0]
    same = (t >> 1) == (j >> 1)
    eye = jnp.where(t == j, 1.0, 0.0)
    ps = [eye + jnp.where(same, L, 0.0) for L in Ls]
    for s in range(2, 7):
        merged = (t >> s) == (j >> s)
        sel = merged & jnp.logical_not(same)
        offs = [stack(jnp.where(sel, L, 0.0).astype(BF16)) for L in Ls]
        pbs = [p.astype(BF16) for p in ps]
        tmps = [_dot(pb, off).astype(BF16) for pb, off in zip(pbs, offs)]
        ps = [p + _dot(tmp, stack(pb)) for p, tmp, pb in zip(ps, tmps, pbs)]
        same = merged
    return ps


def _rwkv_kernel(rw_ref, w0_ref, a0_ref, kk_ref, ka_ref, rk_ref,
                 lnw_ref, lnb_ref, wup_ref, aup_ref, gup_ref, ones_ref, tri_ref,
                 o_ref, s_scr, y_scr, *, width):
    t = pl.program_id(1)
    n_seq, tc, rw_w = rw_ref.shape
    n_rows = n_seq * tc
    n_chunks = tc // CHUNK
    n_groups = width // GROUP_W

    @pl.when(t == 0)
    def _():
        s_scr[...] = jnp.zeros_like(s_scr)

    ones_blk = ones_ref[...]

    def segsum(x):
        return _dot(x.astype(BF16), ones_blk)

    p = rw_ref[...].reshape(n_rows, rw_w)
    r = p[:, 0:width]
    k = p[:, width:2 * width]
    v = p[:, 2 * width:3 * width]
    o = 3 * width
    w_lat = p[:, o:o + LANES]
    a_lat = p[:, o + LANES:o + 2 * LANES]
    g_lat = p[:, o + 2 * LANES:o + 4 * LANES]

    z = w0_ref[...] + _dot(jnp.tanh(w_lat).astype(BF16), wup_ref[...])
    g_log = _sigmoid(z) * (-(2.718281828459045 ** -0.5))
    lr = _sigmoid(a0_ref[...] + _dot(a_lat.astype(BF16), aup_ref[...]))
    gate = _dot(_sigmoid(g_lat).astype(BF16), gup_ref[...])
    kk = k * kk_ref[...]
    kk = kk * lax.rsqrt(jnp.maximum(segsum(kk * kk), 1e-24))
    k = k * (1.0 + (lr - 1.0) * ka_ref[...])
    bonus = segsum(r * k * rk_ref[...]) * v

    tri = tri_ref[...]
    G = jnp.concatenate([_dot_exact_lhs01(tri, g_log[s * tc:(s + 1) * tc]) for s in range(n_seq)], axis=0)
    e_pos = jnp.exp(G)
    e_neg = jnp.exp(-G)
    a_t = (-kk * jnp.exp(G - g_log)).astype(BF16)
    b_t = (kk * lr * e_neg).astype(BF16)
    r_t = (r * e_pos).astype(BF16)
    k_t = (k * e_neg).astype(BF16)
    v_t = v.astype(BF16)

    lane_head = lax.broadcasted_iota(jnp.int32, (CHUNK, GROUP_W), 1) // HEAD_DIM
    head_masks = [lane_head == h for h in range(GROUP_HEADS)]
    rt = lax.broadcasted_iota(jnp.int32, (GROUP_W, GROUP_W), 0)
    ct = lax.broadcasted_iota(jnp.int32, (GROUP_W, GROUP_W), 1)
    same_head = (rt // HEAD_DIM) == (ct // HEAD_DIM)
    diag = rt == ct
    tw = lax.broadcasted_iota(jnp.int32, (CHUNK, GROUP_W), 0)
    jw = lax.broadcasted_iota(jnp.int32, (CHUNK, GROUP_W), 1) % CHUNK
    strict_w = tw > jw
    incl_w = tw >= jw
    rows = lambda s, ci: slice((s * n_chunks + ci) * CHUNK, (s * n_chunks + ci + 1) * CHUNK)
    cols = lambda g: slice(g * GROUP_W, (g + 1) * GROUP_W)
    wide = lambda x, u: x[rows(u[0], u[1]), cols(u[2])]

    def stack(x):
        zero = jnp.zeros_like(x)
        return jnp.concatenate([jnp.where(mk, x, zero) for mk in head_masks], axis=0)

    units = [(s, ci, g) for ci in range(n_chunks) for s in range(n_seq) for g in range(n_groups)]
    a_s = {u: stack(wide(a_t, u)) for u in units}
    b_s = {u: stack(wide(b_t, u)) for u in units}
    k_s = {u: stack(wide(k_t, u)) for u in units}
    v_s = {u: stack(wide(v_t, u)) for u in units}
    l_all = {u: _dot_nt(jnp.concatenate([wide(a_t, u), wide(r_t, u)], axis=0),
                        jnp.concatenate([b_s[u], k_s[u]], axis=0)) for u in units}
    l_ab = {u: jnp.where(strict_w, l_all[u][0:CHUNK, 0:GROUP_W], 0.0) for u in units}
    l_ak = {u: jnp.where(strict_w, l_all[u][0:CHUNK, GROUP_W:], 0.0).astype(BF16) for u in units}
    l_rb = {u: jnp.where(incl_w, l_all[u][CHUNK:, 0:GROUP_W], 0.0).astype(BF16) for u in units}
    l_rk = {u: jnp.where(incl_w, l_all[u][CHUNK:, GROUP_W:], 0.0).astype(BF16) for u in units}
    av = {u: _dot(jnp.concatenate([l_ak[u], l_rk[u]], axis=0), v_s[u]) for u in units}
    akv = {u: stack(av[u][0:CHUNK].astype(BF16)) for u in units}
    y_rk = {u: av[u][CHUNK:] for u in units}
    t_inv = _unit_lower_inverse([l_ab[u] for u in units], stack)
    wu = {u: _dot(t_inv[n].astype(BF16), jnp.concatenate([a_s[u], akv[u]], axis=1))
          for n, u in enumerate(units)}
    last = lambda u: (u[0] * n_chunks + u[1] + 1) * CHUNK - 1
    gc = {u: e_pos[last(u):last(u) + 1, cols(u[2])] for u in units}
    gcol = {u: jnp.sum(jnp.where(diag, jnp.broadcast_to(gc[u], (GROUP_W, GROUP_W)), 0.0),
                       axis=1, keepdims=True) for u in units}
    bk_h = {u: jnp.concatenate([(wide(b_t, u).astype(F32) * gc[u]).astype(BF16),
                                (wide(k_t, u).astype(F32) * gc[u]).astype(BF16)], axis=0) for u in units}

    chains = [(s, g) for s in range(n_seq) for g in range(n_groups)]
    state = {c: s_scr[c[0] * n_groups + c[1]] for c in chains}
    for ci in range(n_chunks):
        level = [(s, ci, g) for s, g in chains]
        zr = {u: _dot(jnp.concatenate([wu[u][:, 0:GROUP_W].astype(BF16), wide(r_t, u)], axis=0),
                      state[u[0], u[2]].astype(BF16)) for u in level}
        z_b = {u: (zr[u][0:CHUNK] + wu[u][:, GROUP_W:]).astype(BF16) for u in level}
        upd = {u: _dot_tn(bk_h[u], jnp.concatenate([z_b[u], wide(v_t, u)], axis=0)) for u in level}
        for u in level:
            state[u[0], u[2]] = state[u[0], u[2]] * gcol[u] + jnp.where(same_head, upd[u], 0.0)
        for u in level:
            y_scr[rows(u[0], ci), cols(u[2])] = (zr[u][CHUNK:] + _dot(l_rb[u], stack(z_b[u]))) + y_rk[u]
    for c in chains:
        s_scr[c[0] * n_groups + c[1]] = state[c]

    y = y_scr[...]
    d = y - segsum(y) * (1.0 / HEAD_DIM)
    var = segsum(d * d) * (1.0 / HEAD_DIM)
    yn = d * lax.rsqrt(var + LNX_EPS) * lnw_ref[...] + lnb_ref[...]
    o_ref[...] = ((yn + bonus) * gate).astype(o_ref.dtype).reshape(n_seq, tc, width)


def _rwkv(rw, w0, a0, k_k, k_a, r_k, ln_w, ln_b, w_up, a_up, g_up, ones_blk, tri_blk, *,
          width, tc, n_seq):
    B, T, rw_w = rw.shape
    assert B % n_seq == 0 and T % tc == 0 and tc % CHUNK == 0 and width % GROUP_W == 0
    n_groups = width // GROUP_W
    kern = functools.partial(_rwkv_kernel, width=width)
    small = [w0, a0, k_k, k_a, r_k, ln_w, ln_b, w_up, a_up, g_up, ones_blk, tri_blk]
    return pl.pallas_call(
        kern,
        grid=(B // n_seq, T // tc),
        in_specs=[
            pl.BlockSpec((n_seq, tc, rw_w), lambda b, t: (b, t, 0)),
        ] + [_const_spec(a.shape) for a in small],
        out_specs=pl.BlockSpec((n_seq, tc, width), lambda b, t: (b, t, 0)),
        out_shape=jax.ShapeDtypeStruct((B, T, width), BF16),
        scratch_shapes=[
            pltpu.VMEM((n_seq * n_groups, GROUP_W, GROUP_W), F32),
            pltpu.VMEM((n_seq * tc, width), F32),
        ],
        compiler_params=pltpu.CompilerParams(
            dimension_semantics=("arbitrary", "arbitrary"), vmem_limit_bytes=VMEM_LIMIT),
        name="rwkv",
    )(rw, *small)


def _outffn_kernel(x_ref, of_ref, or_ref, wof_ref, wor_ref, gpost_ref, gfpre_ref, gfpost_ref,
                   wg_ref, wu_ref, wd_ref, o_ref, *, n_sub):
    tm = x_ref.shape[0]
    sub = tm // n_sub
    parts = [slice(i * sub, (i + 1) * sub) for i in range(n_sub)]
    mix = [_dot(of_ref[sl, :], wof_ref[...]) + _dot(or_ref[sl, :], wor_ref[...]) for sl in parts]
    h = [x_ref[sl, :] + _rms(m) * gpost_ref[...] for sl, m in zip(parts, mix)]
    z = [(_rms(hh) * gfpre_ref[...]).astype(BF16) for hh in h]
    gate = [_dot(zz, wg_ref[...]) for zz in z]
    up = [_dot(zz, wu_ref[...]) for zz in z]
    act = [(g * _sigmoid(g) * u).astype(BF16) for g, u in zip(gate, up)]
    f = [_dot(a, wd_ref[...]) for a in act]
    for sl, hh, ff in zip(parts, h, f):
        o_ref[sl, :] = hh + _rms(ff) * gfpost_ref[...]


def _outffn(x2, o_fox, o_rw, wof, wor, g_post, g_fpre, g_fpost, wg, wu, wd, *, tm, n_sub):
    N, D = x2.shape
    assert N % tm == 0 and tm % n_sub == 0
    consts = [wof, wor, g_post, g_fpre, g_fpost, wg, wu, wd]
    return pl.pallas_call(
        functools.partial(_outffn_kernel, n_sub=n_sub),
        grid=(N // tm,),
        in_specs=[
            pl.BlockSpec((tm, D), lambda i: (i, 0)),
            pl.BlockSpec((tm, o_fox.shape[1]), lambda i: (i, 0)),
            pl.BlockSpec((tm, o_rw.shape[1]), lambda i: (i, 0)),
        ] + [_const_spec(a.shape) for a in consts],
        out_specs=pl.BlockSpec((tm, D), lambda i: (i, 0)),
        out_shape=jax.ShapeDtypeStruct((N, D), F32),
        compiler_params=pltpu.CompilerParams(
            dimension_semantics=("arbitrary",), vmem_limit_bytes=VMEM_LIMIT),
        name="outffn",
    )(x2, o_fox, o_rw, *consts)


def _pad_cols(a, n):
    return jnp.pad(a, ((0, 0), (0, n - a.shape[1])))


def _pad_rows(a, n):
    return jnp.pad(a, ((0, n - a.shape[0]), (0, 0)))


def _block_layer(h, attn_norm_pre, attn_norm_post, w_in, fox_forget_bias, shift_mu, rwkv_w0,
                 rwkv_w_up, rwkv_a0, rwkv_a_up, rwkv_g_up, rwkv_k_k, rwkv_k_a, rwkv_r_k,
                 rwkv_ln_w, rwkv_ln_b, w_out, ffn_norm_pre, ffn_norm_post, ffn_w_gate, ffn_w_up,
                 ffn_w_down):
    B, T, D = h.shape
    fw = D // 2
    rwid = D // 2
    nh = fw // HEAD_DIM
    hw = nh * LANES

    fq, fk, fv = w_in[:, 0:fw], w_in[:, fw:2 * fw], w_in[:, 2 * fw:3 * fw]
    ffw = w_in[:, 3 * fw:3 * fw + nh]
    rcol = 3 * fw + nh

    def rw_layout(a, spare):
        o = 3 * rwid
        return jnp.concatenate([
            a[:, 0:o],
            _pad_cols(jnp.concatenate([a[:, o:o + DECAY_LORA], spare], axis=1), LANES),
            _pad_cols(a[:, o + DECAY_LORA:o + DECAY_LORA + AAA_LORA], LANES),
            _pad_cols(a[:, o + DECAY_LORA + AAA_LORA:], 2 * LANES)], axis=1)

    w_rw = rw_layout(w_in[:, rcol:], ffw)
    rw_w = w_rw.shape[1]
    wp = jnp.concatenate([fq, fk, fv, w_rw], axis=1).astype(BF16)
    mu = rw_layout(shift_mu[None, :], jnp.zeros((1, nh), F32))
    fbias = _pad_cols(jnp.pad(fox_forget_bias[None, :], ((0, 0), (FORGET_LANE, 0))), LANES)

    tm1 = 512
    ii = jnp.arange(tm1)
    tri = (ii[:, None] >= ii[None, :]).astype(BF16)
    bias_lane = lambda hd, p: hd * LANES + (HEAD_DIM if hd % 2 == 0 else 0) + 3 + p
    sel = np.zeros((LANES, hw), np.float32)
    qconst = np.zeros((1, hw), np.float32)
    for hd in range(nh):
        for p in range(3):
            sel[FORGET_LANE + hd + 8 * p, bias_lane(hd, p)] = 1.0
            qconst[0, bias_lane(hd, p)] = 1.0
    sel = jnp.asarray(sel, BF16)
    qconst = jnp.asarray(qconst)

    q_aug, k_aug, v, rw = _inproj(h, attn_norm_pre[None, :], wp, fbias, tri, sel, qconst, mu,
                                  n_heads=nh, rw_w=rw_w, f_col=3 * rwid, tm=tm1)

    o_fox = _fox(q_aug, k_aug, v, tq=1024)

    tc = 128
    hh = jnp.arange(rwid) // HEAD_DIM
    ones_blk = (hh[:, None] == hh[None, :]).astype(BF16)
    jj = jnp.arange(tc)
    tri_blk = ((jj[:, None] >= jj[None, :]) & (jj[:, None] // CHUNK == jj[None, :] // CHUNK)).astype(BF16)
    row = lambda a: a.reshape(1, -1)
    o_rw = _rwkv(rw, row(rwkv_w0), row(rwkv_a0), row(rwkv_k_k), row(rwkv_k_a), row(rwkv_r_k),
                 row(rwkv_ln_w), row(rwkv_ln_b),
                 _pad_rows(rwkv_w_up, LANES).astype(BF16), _pad_rows(rwkv_a_up, LANES).astype(BF16),
                 _pad_rows(rwkv_g_up, 2 * LANES).astype(BF16), ones_blk, tri_blk,
                 width=rwid, tc=tc, n_seq=4)

    out = _outffn(h.reshape(B * T, D), o_fox.reshape(B * T, fw), o_rw.reshape(B * T, rwid),
                  w_out[0:fw].astype(BF16), w_out[fw:].astype(BF16),
                  row(attn_norm_post), row(ffn_norm_pre), row(ffn_norm_post),
                  ffn_w_gate.astype(BF16), ffn_w_up.astype(BF16), ffn_w_down.astype(BF16),
                  tm=512, n_sub=2)
    return out.reshape(B, T, D)


def kernel(x, attn_norm_pre, attn_norm_post, w_in, fox_forget_bias, shift_mu, rwkv_w0, rwkv_w_up,
           rwkv_a0, rwkv_a_up, rwkv_g_up, rwkv_k_k, rwkv_k_a, rwkv_r_k, rwkv_ln_w, rwkv_ln_b, w_out,
           ffn_norm_pre, ffn_norm_post, ffn_w_gate, ffn_w_up, ffn_w_down):
    h = x
    for l in range(attn_norm_pre.shape[0]):
        h = _block_layer(h, attn_norm_pre[l], attn_norm_post[l], w_in[l], fox_forget_bias[l],
                         shift_mu[l], rwkv_w0[l], rwkv_w_up[l], rwkv_a0[l], rwkv_a_up[l],
                         rwkv_g_up[l], rwkv_k_k[l], rwkv_k_a[l], rwkv_r_k[l], rwkv_ln_w[l],
                         rwkv_ln_b[l], w_out[l], ffn_norm_pre[l], ffn_norm_post[l], ffn_w_gate[l],
                         ffn_w_up[l], ffn_w_down[l])
    return h
```

```python
import functools

import jax
import jax.numpy as jnp
import numpy as np
from jax import lax
from jax.experimental import pallas as pl
from jax.experimental.pallas import tpu as pltpu

F32 = jnp.float32
BF16 = jnp.bfloat16

HEAD_DIM = 64
LANES = 128
NORM_EPS = 1e-6
LNX_EPS = 64e-5
LOG2E = 1.4426950408889634
DECAY_LORA = 64
AAA_LORA = 64
GATE_LORA = 160
CHUNK = 64
GROUP_HEADS = 4
GROUP_W = GROUP_HEADS * HEAD_DIM
FORGET_LANE = DECAY_LORA
VMEM_LIMIT = 56 * 1024 * 1024


def _dot(a, b):
    return jnp.dot(a, b, preferred_element_type=F32)


def _dot_nt(a, b):
    return lax.dot_general(a, b, (((1,), (1,)), ((), ())), preferred_element_type=F32)


def _dot_tn(a, b):
    return lax.dot_general(a, b, (((0,), (0,)), ((), ())), preferred_element_type=F32)


def _split2(x):
    hi = x.astype(BF16)
    lo = (x - hi.astype(F32)).astype(BF16)
    return hi, lo


def _split3(x):
    hi = x.astype(BF16)
    r = x - hi.astype(F32)
    mid = r.astype(BF16)
    lo = (r - mid.astype(F32)).astype(BF16)
    return hi, mid, lo


def _dot_exact_lhs01(ones01, x):
    hi, mid, lo = _split3(x)
    return _dot(ones01, hi) + (_dot(ones01, mid) + _dot(ones01, lo))


def _rms(x):
    return x * lax.rsqrt(jnp.mean(x * x, axis=-1, keepdims=True) + NORM_EPS)


def _softplus(x):
    return jnp.maximum(x, 0.0) + jnp.log1p(jnp.exp(-jnp.abs(x)))


def _sigmoid(x):
    return 0.5 * jnp.tanh(0.5 * x) + 0.5


def _const_spec(shape):
    nd = len(shape)
    return pl.BlockSpec(shape, lambda *_: (0,) * nd, pipeline_mode=pl.Buffered(1))


def _inproj_kernel(x_ref, g_ref, w_ref, fb_ref, tri_ref, sel_ref, qc_ref, mu_ref,
                   q_out, k_out, v_out, rw_out, carry_ref, rwlast_ref, *, n_heads, rw_w, f_col):
    t = pl.program_id(1)

    @pl.when(t == 0)
    def _():
        carry_ref[...] = jnp.zeros_like(carry_ref)
        rwlast_ref[...] = jnp.zeros_like(rwlast_ref)

    fw = n_heads * HEAD_DIM
    tm = x_ref.shape[1]
    u = (_rms(x_ref[0]) * g_ref[...]).astype(BF16)
    lane = lax.broadcasted_iota(jnp.int32, (tm, LANES), 1)
    lower = lane < HEAD_DIM

    rw = _dot(u, w_ref[:, 3 * fw:3 * fw + rw_w])
    row0 = lax.broadcasted_iota(jnp.int32, (tm, 1), 0) == 0
    rw_prev = jnp.where(row0, rwlast_ref[...], pltpu.roll(rw, 1, 0))
    rwlast_ref[...] = rw[tm - 1:tm, :]
    rw_out[0] = rw + mu_ref[...] * (rw_prev - rw)

    fo = f_col
    in_f = (lane >= FORGET_LANE) & (lane < FORGET_LANE + n_heads)

    def pack3(x):
        hi = x.astype(BF16).astype(F32)
        r1 = x - hi
        mid = r1.astype(BF16).astype(F32)
        lo = r1 - mid
        keep = lambda a: jnp.where(in_f, a, 0.0)
        return (keep(hi) + pltpu.roll(keep(mid), 8, 1) + pltpu.roll(keep(lo), 16, 1)).astype(BF16)

    def unpack3(x):
        return x + pltpu.roll(x, LANES - 8, 1) + pltpu.roll(x, LANES - 16, 1)

    logf = -_softplus(-(rw[:, fo:fo + LANES] + fb_ref[...]))
    c = unpack3(_dot(tri_ref[...], pack3(logf))) + carry_ref[...]
    carry_ref[...] = c[tm - 1:tm, :]

    qk = _dot(u, w_ref[:, 0:2 * fw])
    k_bias = _dot(pack3(c * (-LOG2E)), sel_ref[...])
    scale = HEAD_DIM ** -0.5 * LOG2E
    for hp in range(n_heads // 2):
        qb = qk[:, hp * LANES:(hp + 1) * LANES] * scale
        kb = qk[:, fw + hp * LANES:fw + (hp + 1) * LANES]
        for h, keep in ((2 * hp, lower), (2 * hp + 1, jnp.logical_not(lower))):
            q_out[0, h] = jnp.where(keep, qb, qc_ref[:, h * LANES:(h + 1) * LANES]).astype(BF16)
            k_out[0, h] = jnp.where(keep, kb, k_bias[:, h * LANES:(h + 1) * LANES]).astype(BF16)

    v = _dot(u, w_ref[:, 2 * fw:3 * fw])
    one_even = jnp.where(lane == HEAD_DIM, 1.0, 0.0)
    one_odd = jnp.where(lane == 0, 1.0, 0.0)
    for hp in range(n_heads // 2):
        blk = v[:, hp * LANES:(hp + 1) * LANES]
        v_out[0, 2 * hp] = jnp.where(lower, blk, one_even).astype(BF16)
        v_out[0, 2 * hp + 1] = jnp.where(lower, one_odd, blk).astype(BF16)


def _inproj(x, g_pre, wp, fbias, tri, sel, qconst, mu, *, n_heads, rw_w, f_col, tm):
    B, T, D = x.shape
    assert T % tm == 0 and n_heads % 2 == 0 and tm == tri.shape[0]
    kern = functools.partial(_inproj_kernel, n_heads=n_heads, rw_w=rw_w, f_col=f_col)
    return pl.pallas_call(
        kern,
        grid=(B, T // tm),
        in_specs=[
            pl.BlockSpec((1, tm, D), lambda b, t: (b, t, 0)),
            _const_spec(g_pre.shape),
            _const_spec(wp.shape),
            _const_spec(fbias.shape),
            _const_spec(tri.shape),
            _const_spec(sel.shape),
            _const_spec(qconst.shape),
            _const_spec(mu.shape),
        ],
        out_specs=[
            pl.BlockSpec((1, n_heads, tm, LANES), lambda b, t: (b, 0, t, 0)),
            pl.BlockSpec((1, n_heads, tm, LANES), lambda b, t: (b, 0, t, 0)),
            pl.BlockSpec((1, n_heads, tm, LANES), lambda b, t: (b, 0, t, 0)),
            pl.BlockSpec((1, tm, rw_w), lambda b, t: (b, t, 0)),
        ],
        out_shape=[
            jax.ShapeDtypeStruct((B, n_heads, T, LANES), BF16),
            jax.ShapeDtypeStruct((B, n_heads, T, LANES), BF16),
            jax.ShapeDtypeStruct((B, n_heads, T, LANES), BF16),
            jax.ShapeDtypeStruct((B, T, rw_w), F32),
        ],
        scratch_shapes=[pltpu.VMEM((1, LANES), F32),
                        pltpu.VMEM((1, rw_w), F32)],
        compiler_params=pltpu.CompilerParams(
            dimension_semantics=("arbitrary", "arbitrary"), vmem_limit_bytes=VMEM_LIMIT),
        name="inproj",
    )(x, g_pre, wp, fbias, tri, sel, qconst, mu)


def _fox_kernel(q_ref, k_ref, v_ref, o_ref, m_scr, acc_scr, *, tq):
    qi = pl.program_id(2)
    heads = range(2)
    half = tq // 2

    def step(jobs):
        todo = [(n, h) for n in range(len(jobs)) for h in heads]
        qrows = [pl.ds(q0, nq) for q0, nq, _, _, _ in jobs]
        krows = [pl.ds(pl.multiple_of(k0, tq), tk) for _, _, k0, tk, _ in jobs]
        s = {(n, h): _dot_nt(q_ref[0, h, qrows[n], :], k_ref[0, h, krows[n], :]) for n, h in todo}
        for n, (_, nq, _, tk, masked) in enumerate(jobs):
            if masked:
                row = lax.broadcasted_iota(jnp.int32, (nq, nq), 0)
                col = lax.broadcasted_iota(jnp.int32, (nq, nq), 1)
                for h in heads:
                    edge = jnp.where(col <= row, s[n, h][:, tk - nq:], -jnp.inf)
                    s[n, h] = edge if tk == nq else jnp.concatenate([s[n, h][:, :tk - nq], edge], axis=1)
        m_prev = {(n, h): m_scr[h, qrows[n], :] for n, h in todo}
        m_next = {nh: jnp.maximum(m_prev[nh], jnp.max(s[nh], axis=1, keepdims=True)) for nh in todo}
        p = {(n, h): jnp.exp2(s[n, h] - jnp.concatenate([m_next[n, h]] * (jobs[n][3] // LANES), axis=1)
                              ).astype(BF16) for n, h in todo}
        for n, h in todo:
            alpha = jnp.exp2(m_prev[n, h] - m_next[n, h])
            acc_scr[h, qrows[n], :] = (alpha * acc_scr[h, qrows[n], :]
                                       + _dot(p[n, h], v_ref[0, h, krows[n], :]))
            m_scr[h, qrows[n], :] = m_next[n, h]

    for h in heads:
        m_scr[h] = jnp.full(m_scr.shape[1:], -jnp.inf, F32)
        acc_scr[h] = jnp.zeros(acc_scr.shape[1:], F32)

    def body(j, carry):
        step([(0, tq, j * tq, tq, False)])
        return carry

    lax.fori_loop(0, qi, body, 0)
    quarter = tq // 4
    step([(n * quarter, quarter, qi * tq, (n + 1) * quarter, True) for n in range(4)])

    lane = lax.broadcasted_iota(jnp.int32, acc_scr.shape[1:], 1)
    acc0 = acc_scr[0]
    acc1 = acc_scr[1]
    o = jnp.where(lane < HEAD_DIM, acc0 / acc0[:, HEAD_DIM:HEAD_DIM + 1], acc1 / acc1[:, 0:1])
    o_ref[0] = o.astype(o_ref.dtype)


def _fox(q_aug, k_aug, v_aug, *, tq):
    B, H, T, _ = q_aug.shape
    assert T % tq == 0 and H % 2 == 0
    kern = functools.partial(_fox_kernel, tq=tq)
    return pl.pallas_call(
        kern,
        grid=(B, H // 2, T // tq),
        in_specs=[
            pl.BlockSpec((1, 2, tq, LANES), lambda b, hp, qi: (b, hp, qi, 0)),
            pl.BlockSpec((1, 2, T, LANES), lambda b, hp, qi: (b, hp, 0, 0)),
            pl.BlockSpec((1, 2, T, LANES), lambda b, hp, qi: (b, hp, 0, 0)),
        ],
        out_specs=pl.BlockSpec((1, tq, LANES), lambda b, hp, qi: (b, qi, hp)),
        out_shape=jax.ShapeDtypeStruct((B, T, H * HEAD_DIM), BF16),
        scratch_shapes=[pltpu.VMEM((2, tq, LANES), F32)] * 2,
        compiler_params=pltpu.CompilerParams(
            dimension_semantics=("arbitrary", "arbitrary", "arbitrary"),
            vmem_limit_bytes=VMEM_LIMIT),
        name="fox",
    )(q_aug, k_aug, v_aug)


def _unit_lower_inverse(Ls, stack):
    shape = Ls[0].shape
    t = lax.broadcasted_iota(jnp.int32, shape, 0)
    j = lax.broadcasted_iota(jnp.int32, shape, 1) % shape[0]
    same = (t >> 1) == (j >> 1)
    eye = jnp.where(t == j, 1.0, 0.0)
    ps = [eye + jnp.where(same, L, 0.0) for L in Ls]
    for s in range(2, 7):
        merged = (t >> s) == (j >> s)
        sel = merged & jnp.logical_not(same)
        offs = [stack(jnp.where(sel, L, 0.0).astype(BF16)) for L in Ls]
        pbs = [p.astype(BF16) for p in ps]
        tmps = [_dot(pb, off).astype(BF16) for pb, off in zip(pbs, offs)]
        ps = [p + _dot(tmp, stack(pb)) for p, tmp, pb in zip(ps, tmps, pbs)]
        same = merged
    return ps


def _rwkv_kernel(rw_ref, w0_ref, a0_ref, kk_ref, ka_ref, rk_ref,
                 lnw_ref, lnb_ref, wup_ref, aup_ref, gup_ref, ones_ref, tri_ref,
                 o_ref, s_scr, y_scr, *, width):
    t = pl.program_id(1)
    n_seq, tc, rw_w = rw_ref.shape
    n_rows = n_seq * tc
    n_chunks = tc // CHUNK
    n_groups = width // GROUP_W

    @pl.when(t == 0)
    def _():
        s_scr[...] = jnp.zeros_like(s_scr)

    ones_blk = ones_ref[...]

    def segsum(x):
        return _dot(x.astype(BF16), ones_blk)

    p = rw_ref[...].reshape(n_rows, rw_w)
    r = p[:, 0:width]
    k = p[:, width:2 * width]
    v = p[:, 2 * width:3 * width]
    o = 3 * width
    w_lat = p[:, o:o + LANES]
    a_lat = p[:, o + LANES:o + 2 * LANES]
    g_lat = p[:, o + 2 * LANES:o + 4 * LANES]

    z = w0_ref[...] + _dot(jnp.tanh(w_lat).astype(BF16), wup_ref[...])
    g_log = _sigmoid(z) * (-(2.718281828459045 ** -0.5))
    lr = _sigmoid(a0_ref[...] + _dot(a_lat.astype(BF16), aup_ref[...]))
    gate = _dot(_sigmoid(g_lat).astype(BF16), gup_ref[...])
    kk = k * kk_ref[...]
    kk = kk * lax.rsqrt(jnp.maximum(segsum(kk * kk), 1e-24))
    k = k * (1.0 + (lr - 1.0) * ka_ref[...])
    bonus = segsum(r * k * rk_ref[...]) * v

    tri = tri_ref[...]
    G = jnp.concatenate([_dot_exact_lhs01(tri, g_log[s * tc:(s + 1) * tc]) for s in range(n_seq)], axis=0)
    e_pos = jnp.exp(G)
    e_neg = jnp.exp(-G)
    a_t = (-kk * jnp.exp(G - g_log)).astype(BF16)
    b_t = (kk * lr * e_neg).astype(BF16)
    r_t = (r * e_pos).astype(BF16)
    k_t = (k * e_neg).astype(BF16)
    v_t = v.astype(BF16)

    lane_head = lax.broadcasted_iota(jnp.int32, (CHUNK, GROUP_W), 1) // HEAD_DIM
    head_masks = [lane_head == h for h in range(GROUP_HEADS)]
    rt = lax.broadcasted_iota(jnp.int32, (GROUP_W, GROUP_W), 0)
    ct = lax.broadcasted_iota(jnp.int32, (GROUP_W, GROUP_W), 1)
    same_head = (rt // HEAD_DIM) == (ct // HEAD_DIM)
    diag = rt == ct
    tw = lax.broadcasted_iota(jnp.int32, (CHUNK, GROUP_W), 0)
    jw = lax.broadcasted_iota(jnp.int32, (CHUNK, GROUP_W), 1) % CHUNK
    strict_w = tw > jw
    incl_w = tw >= jw
    rows = lambda s, ci: slice((s * n_chunks + ci) * CHUNK, (s * n_chunks + ci + 1) * CHUNK)
    cols = lambda g: slice(g * GROUP_W, (g + 1) * GROUP_W)
    wide = lambda x, u: x[rows(u[0], u[1]), cols(u[2])]

    def stack(x):
        zero = jnp.zeros_like(x)
        return jnp.concatenate([jnp.where(mk, x, zero) for mk in head_masks], axis=0)

    units = [(s, ci, g) for ci in range(n_chunks) for s in range(n_seq) for g in range(n_groups)]
    b_s = {u: stack(wide(b_t, u)) for u in units}
    k_s = {u: stack(wide(k_t, u)) for u in units}
    v_s = {u: stack(wide(v_t, u)) for u in units}
    l_all = {u: _dot_nt(jnp.concatenate([wide(a_t, u), wide(r_t, u)], axis=0),
                        jnp.concatenate([b_s[u], k_s[u]], axis=0)) for u in units}
    l_ab = {u: jnp.where(strict_w, l_all[u][0:CHUNK, 0:GROUP_W], 0.0) for u in units}
    l_ak = {u: jnp.where(strict_w, l_all[u][0:CHUNK, GROUP_W:], 0.0).astype(BF16) for u in units}
    l_rb = {u: jnp.where(incl_w, l_all[u][CHUNK:, 0:GROUP_W], 0.0).astype(BF16) for u in units}
    l_rk = {u: jnp.where(incl_w, l_all[u][CHUNK:, GROUP_W:], 0.0).astype(BF16) for u in units}
    av = {u: _dot(jnp.concatenate([l_ak[u], l_rk[u]], axis=0), v_s[u]) for u in units}
    akv = {u: av[u][0:CHUNK] for u in units}
    y_rk = {u: av[u][CHUNK:] for u in units}
    t_inv = {u: t.astype(BF16) for u, t in
             zip(units, _unit_lower_inverse([l_ab[u] for u in units], stack))}
    last = lambda u: (u[0] * n_chunks + u[1] + 1) * CHUNK - 1
    gc = {u: e_pos[last(u):last(u) + 1, cols(u[2])] for u in units}
    gcol = {u: jnp.sum(jnp.where(diag, jnp.broadcast_to(gc[u], (GROUP_W, GROUP_W)), 0.0),
                       axis=1, keepdims=True) for u in units}
    bk_h = {u: jnp.concatenate([(wide(b_t, u).astype(F32) * gc[u]).astype(BF16),
                                (wide(k_t, u).astype(F32) * gc[u]).astype(BF16)], axis=0) for u in units}

    chains = [(s, g) for s in range(n_seq) for g in range(n_groups)]
    state = {c: s_scr[c[0] * n_groups + c[1]] for c in chains}
    for ci in range(n_chunks):
        level = [(s, ci, g) for s, g in chains]
        zr = {u: _dot(jnp.concatenate([wide(a_t, u), wide(r_t, u)], axis=0),
                      state[u[0], u[2]].astype(BF16)) for u in level}
        z_b = {u: _dot(t_inv[u], stack((zr[u][0:CHUNK] + akv[u]).astype(BF16))).astype(BF16)
               for u in level}
        upd = {u: _dot_tn(bk_h[u], jnp.concatenate([z_b[u], wide(v_t, u)], axis=0)) for u in level}
        for u in level:
            state[u[0], u[2]] = state[u[0], u[2]] * gcol[u] + jnp.where(same_head, upd[u], 0.0)
        for u in level:
            y_scr[rows(u[0], ci), cols(u[2])] = (zr[u][CHUNK:] + _dot(l_rb[u], stack(z_b[u]))) + y_rk[u]
    for c in chains:
        s_scr[c[0] * n_groups + c[1]] = state[c]

    y = y_scr[...]
    d = y - segsum(y) * (1.0 / HEAD_DIM)
    var = segsum(d * d) * (1.0 / HEAD_DIM)
    yn = d * lax.rsqrt(var + LNX_EPS) * lnw_ref[...] + lnb_ref[...]
    o_ref[...] = ((yn + bonus) * gate).astype(o_ref.dtype).reshape(n_seq, tc, width)


def _rwkv(rw, w0, a0, k_k, k_a, r_k, ln_w, ln_b, w_up, a_up, g_up, ones_blk, tri_blk, *,
          width, tc, n_seq):
    B, T, rw_w = rw.shape
    assert B % n_seq == 0 and T % tc == 0 and tc % CHUNK == 0 and width % GROUP_W == 0
    n_groups = width // GROUP_W
    kern = functools.partial(_rwkv_kernel, width=width)
    small = [w0, a0, k_k, k_a, r_k, ln_w, ln_b, w_up, a_up, g_up, ones_blk, tri_blk]
    return pl.pallas_call(
        kern,
        grid=(B // n_seq, T // tc),
        in_specs=[
            pl.BlockSpec((n_seq, tc, rw_w), lambda b, t: (b, t, 0)),
        ] + [_const_spec(a.shape) for a in small],
        out_specs=pl.BlockSpec((n_seq, tc, width), lambda b, t: (b, t, 0)),
        out_shape=jax.ShapeDtypeStruct((B, T, width), BF16),
        scratch_shapes=[
            pltpu.VMEM((n_seq * n_groups, GROUP_W, GROUP_W), F32),
            pltpu.VMEM((n_seq * tc, width), F32),
        ],
        compiler_params=pltpu.CompilerParams(
            dimension_semantics=("arbitrary", "arbitrary"), vmem_limit_bytes=VMEM_LIMIT),
        name="rwkv",
    )(rw, *small)


def _outffn_kernel(x_ref, of_ref, or_ref, wof_ref, wor_ref, gpost_ref, gfpre_ref, gfpost_ref,
                   wg_ref, wu_ref, wd_ref, o_ref, *, n_sub):
    tm = x_ref.shape[0]
    sub = tm // n_sub
    parts = [slice(i * sub, (i + 1) * sub) for i in range(n_sub)]
    mix = [_dot(of_ref[sl, :], wof_ref[...]) + _dot(or_ref[sl, :], wor_ref[...]) for sl in parts]
    h = [x_ref[sl, :] + _rms(m) * gpost_ref[...] for sl, m in zip(parts, mix)]
    z = [(_rms(hh) * gfpre_ref[...]).astype(BF16) for hh in h]
    gate = [_dot(zz, wg_ref[...]) for zz in z]
    up = [_dot(zz, wu_ref[...]) for zz in z]
    act = [(g * _sigmoid(g) * u).astype(BF16) for g, u in zip(gate, up)]
    f = [_dot(a, wd_ref[...]) for a in act]
    for sl, hh, ff in zip(parts, h, f):
        o_ref[sl, :] = hh + _rms(ff) * gfpost_ref[...]


def _outffn(x2, o_fox, o_rw, wof, wor, g_post, g_fpre, g_fpost, wg, wu, wd, *, tm, n_sub):
    N, D = x2.shape
    assert N % tm == 0 and tm % n_sub == 0
    consts = [wof, wor, g_post, g_fpre, g_fpost, wg, wu, wd]
    return pl.pallas_call(
        functools.partial(_outffn_kernel, n_sub=n_sub),
        grid=(N // tm,),
        in_specs=[
            pl.BlockSpec((tm, D), lambda i: (i, 0)),
            pl.BlockSpec((tm, o_fox.shape[1]), lambda i: (i, 0)),
            pl.BlockSpec((tm, o_rw.shape[1]), lambda i: (i, 0)),
        ] + [_const_spec(a.shape) for a in consts],
        out_specs=pl.BlockSpec((tm, D), lambda i: (i, 0)),
        out_shape=jax.ShapeDtypeStruct((N, D), F32),
        compiler_params=pltpu.CompilerParams(
            dimension_semantics=("arbitrary",), vmem_limit_bytes=VMEM_LIMIT),
        name="outffn",
    )(x2, o_fox, o_rw, *consts)


def _pad_cols(a, n):
    return jnp.pad(a, ((0, 0), (0, n - a.shape[1])))


def _pad_rows(a, n):
    return jnp.pad(a, ((0, n - a.shape[0]), (0, 0)))


def _block_layer(h, attn_norm_pre, attn_norm_post, w_in, fox_forget_bias, shift_mu, rwkv_w0,
                 rwkv_w_up, rwkv_a0, rwkv_a_up, rwkv_g_up, rwkv_k_k, rwkv_k_a, rwkv_r_k,
                 rwkv_ln_w, rwkv_ln_b, w_out, ffn_norm_pre, ffn_norm_post, ffn_w_gate, ffn_w_up,
                 ffn_w_down):
    B, T, D = h.shape
    fw = D // 2
    rwid = D // 2
    nh = fw // HEAD_DIM
    hw = nh * LANES

    fq, fk, fv = w_in[:, 0:fw], w_in[:, fw:2 * fw], w_in[:, 2 * fw:3 * fw]
    ffw = w_in[:, 3 * fw:3 * fw + nh]
    rcol = 3 * fw + nh

    def rw_layout(a, spare):
        o = 3 * rwid
        return jnp.concatenate([
            a[:, 0:o],
            _pad_cols(jnp.concatenate([a[:, o:o + DECAY_LORA], spare], axis=1), LANES),
            _pad_cols(a[:, o + DECAY_LORA:o + DECAY_LORA + AAA_LORA], LANES),
            _pad_cols(a[:, o + DECAY_LORA + AAA_LORA:], 2 * LANES)], axis=1)

    w_rw = rw_layout(w_in[:, rcol:], ffw)
    rw_w = w_rw.shape[1]
    wp = jnp.concatenate([fq, fk, fv, w_rw], axis=1).astype(BF16)
    mu = rw_layout(shift_mu[None, :], jnp.zeros((1, nh), F32))
    fbias = _pad_cols(jnp.pad(fox_forget_bias[None, :], ((0, 0), (FORGET_LANE, 0))), LANES)

    tm1 = 512
    ii = jnp.arange(tm1)
    tri = (ii[:, None] >= ii[None, :]).astype(BF16)
    bias_lane = lambda hd, p: hd * LANES + (HEAD_DIM if hd % 2 == 0 else 0) + 3 + p
    sel = np.zeros((LANES, hw), np.float32)
    qconst = np.zeros((1, hw), np.float32)
    for hd in range(nh):
        for p in range(3):
            sel[FORGET_LANE + hd + 8 * p, bias_lane(hd, p)] = 1.0
            qconst[0, bias_lane(hd, p)] = 1.0
    sel = jnp.asarray(sel, BF16)
    qconst = jnp.asarray(qconst)

    q_aug, k_aug, v, rw = _inproj(h, attn_norm_pre[None, :], wp, fbias, tri, sel, qconst, mu,
                                  n_heads=nh, rw_w=rw_w, f_col=3 * rwid, tm=tm1)

    o_fox = _fox(q_aug, k_aug, v, tq=1024)

    tc = 128
    hh = jnp.arange(rwid) // HEAD_DIM
    ones_blk = (hh[:, None] == hh[None, :]).astype(BF16)
    jj = jnp.arange(tc)
    tri_blk = ((jj[:, None] >= jj[None, :]) & (jj[:, None] // CHUNK == jj[None, :] // CHUNK)).astype(BF16)
    row = lambda a: a.reshape(1, -1)
    o_rw = _rwkv(rw, row(rwkv_w0), row(rwkv_a0), row(rwkv_k_k), row(rwkv_k_a), row(rwkv_r_k),
                 row(rwkv_ln_w), row(rwkv_ln_b),
                 _pad_rows(rwkv_w_up, LANES).astype(BF16), _pad_rows(rwkv_a_up, LANES).astype(BF16),
                 _pad_rows(rwkv_g_up, 2 * LANES).astype(BF16), ones_blk, tri_blk,
                 width=rwid, tc=tc, n_seq=4)

    out = _outffn(h.reshape(B * T, D), o_fox.reshape(B * T, fw), o_rw.reshape(B * T, rwid),
                  w_out[0:fw].astype(BF16), w_out[fw:].astype(BF16),
                  row(attn_norm_post), row(ffn_norm_pre), row(ffn_norm_post),
                  ffn_w_gate.astype(BF16), ffn_w_up.astype(BF16), ffn_w_down.astype(BF16),
                  tm=512, n_sub=2)
    return out.reshape(B, T, D)


def kernel(x, attn_norm_pre, attn_norm_post, w_in, fox_forget_bias, shift_mu, rwkv_w0, rwkv_w_up,
           rwkv_a0, rwkv_a_up, rwkv_g_up, rwkv_k_k, rwkv_k_a, rwkv_r_k, rwkv_ln_w, rwkv_ln_b, w_out,
           ffn_norm_pre, ffn_norm_post, ffn_w_gate, ffn_w_up, ffn_w_down):
    h = x
    for l in range(attn_norm_pre.shape[0]):
        h = _block_layer(h, attn_norm_pre[l], attn_norm_post[l], w_in[l], fox_forget_bias[l],
                         shift_mu[l], rwkv_w0[l], rwkv_w_up[l], rwkv_a0[l], rwkv_a_up[l],
                         rwkv_g_up[l], rwkv_k_k[l], rwkv_k_a[l], rwkv_r_k[l], rwkv_ln_w[l],
                         rwkv_ln_b[l], w_out[l], ffn_norm_pre[l], ffn_norm_post[l], ffn_w_gate[l],
                         ffn_w_up[l], ffn_w_down[l])
    return h
```

```python
import functools

import jax
import jax.numpy as jnp
import numpy as np
from jax import lax
from jax.experimental import pallas as pl
from jax.experimental.pallas import tpu as pltpu

F32 = jnp.float32
BF16 = jnp.bfloat16

HEAD_DIM = 64
LANES = 128
NORM_EPS = 1e-6
LNX_EPS = 64e-5
LOG2E = 1.4426950408889634
DECAY_LORA = 64
AAA_LORA = 64
GATE_LORA = 160
CHUNK = 64
GROUP_HEADS = 4
GROUP_W = GROUP_HEADS * HEAD_DIM
FORGET_LANE = DECAY_LORA
VMEM_LIMIT = 56 * 1024 * 1024


def _dot(a, b):
    return jnp.dot(a, b, preferred_element_type=F32)


def _dot_nt(a, b):
    return lax.dot_general(a, b, (((1,), (1,)), ((), ())), preferred_element_type=F32)


def _dot_tn(a, b):
    return lax.dot_general(a, b, (((0,), (0,)), ((), ())), preferred_element_type=F32)


def _split2(x):
    hi = x.astype(BF16)
    lo = (x - hi.astype(F32)).astype(BF16)
    return hi, lo


def _split3(x):
    hi = x.astype(BF16)
    r = x - hi.astype(F32)
    mid = r.astype(BF16)
    lo = (r - mid.astype(F32)).astype(BF16)
    return hi, mid, lo


def _dot_exact_lhs01(ones01, x):
    hi, mid, lo = _split3(x)
    return _dot(ones01, hi) + (_dot(ones01, mid) + _dot(ones01, lo))


def _rms(x):
    return x * lax.rsqrt(jnp.mean(x * x, axis=-1, keepdims=True) + NORM_EPS)


def _softplus(x):
    return jnp.maximum(x, 0.0) + jnp.log1p(jnp.exp(-jnp.abs(x)))


def _sigmoid(x):
    return 0.5 * jnp.tanh(0.5 * x) + 0.5


def _const_spec(shape):
    nd = len(shape)
    return pl.BlockSpec(shape, lambda *_: (0,) * nd, pipeline_mode=pl.Buffered(1))


def _inproj_kernel(x_ref, g_ref, w_ref, fb_ref, tri_ref, sel_ref, qc_ref, mu_ref,
                   w0_ref, a0_ref, kk_ref, ka_ref, rk_ref, wup_ref, aup_ref, gup_ref, ones_ref, trib_ref,
                   q_out, k_out, v_out, ops_out, bg_out, gc_out, carry_ref, rwlast_ref,
                   *, n_heads, rw_w, f_col):
    t = pl.program_id(1)

    @pl.when(t == 0)
    def _():
        carry_ref[...] = jnp.zeros_like(carry_ref)
        rwlast_ref[...] = jnp.zeros_like(rwlast_ref)

    fw = n_heads * HEAD_DIM
    tm = x_ref.shape[1]
    u = (_rms(x_ref[0]) * g_ref[...]).astype(BF16)
    lane = lax.broadcasted_iota(jnp.int32, (tm, LANES), 1)
    lower = lane < HEAD_DIM

    rw = _dot(u, w_ref[:, 3 * fw:3 * fw + rw_w])
    qk = _dot(u, w_ref[:, 0:2 * fw])
    v_fox = _dot(u, w_ref[:, 2 * fw:3 * fw])
    row0 = lax.broadcasted_iota(jnp.int32, (tm, 1), 0) == 0
    rw_prev = jnp.where(row0, rwlast_ref[...], pltpu.roll(rw, 1, 0))
    rwlast_ref[...] = rw[tm - 1:tm, :]
    p = rw + mu_ref[...] * (rw_prev - rw)

    width = fw
    ones_blk = ones_ref[...]
    segsum = lambda x: _dot(x.astype(BF16), ones_blk)
    r = p[:, 0:width]
    k = p[:, width:2 * width]
    v = p[:, 2 * width:3 * width]
    o = 3 * width
    w_lat = p[:, o:o + LANES]
    a_lat = p[:, o + LANES:o + 2 * LANES]
    g_lat = p[:, o + 2 * LANES:o + 4 * LANES]
    z = w0_ref[...] + _dot(jnp.tanh(w_lat).astype(BF16), wup_ref[...])
    g_log = _sigmoid(z) * (-(2.718281828459045 ** -0.5))
    lr = _sigmoid(a0_ref[...] + _dot(a_lat.astype(BF16), aup_ref[...]))
    gate = _dot(_sigmoid(g_lat).astype(BF16), gup_ref[...])
    kk = k * kk_ref[...]
    kk = kk * lax.rsqrt(jnp.maximum(segsum(kk * kk), 1e-24))
    k = k * (1.0 + (lr - 1.0) * ka_ref[...])
    bg_out[0, :, 0:width] = segsum(r * k * rk_ref[...]) * v
    bg_out[0, :, width:] = gate
    nb = trib_ref.shape[0]
    G = jnp.concatenate([_dot_exact_lhs01(trib_ref[...], g_log[i:i + nb]) for i in range(0, tm, nb)], axis=0)
    e_pos = jnp.exp(G)
    e_neg = jnp.exp(-G)
    ops_out[0, :, 0:width] = (-kk * jnp.exp(G - g_log)).astype(BF16)
    ops_out[0, :, width:2 * width] = (kk * lr * e_neg).astype(BF16)
    ops_out[0, :, 2 * width:3 * width] = (r * e_pos).astype(BF16)
    ops_out[0, :, 3 * width:4 * width] = (k * e_neg).astype(BF16)
    ops_out[0, :, 4 * width:] = v.astype(BF16)
    for ci in range(tm // CHUNK):
        gc_out[0, ci] = e_pos[(ci + 1) * CHUNK - 1:(ci + 1) * CHUNK, :]

    fo = f_col
    in_f = (lane >= FORGET_LANE) & (lane < FORGET_LANE + n_heads)

    def pack3(x):
        hi = x.astype(BF16).astype(F32)
        r1 = x - hi
        mid = r1.astype(BF16).astype(F32)
        lo = r1 - mid
        keep = lambda a: jnp.where(in_f, a, 0.0)
        return (keep(hi) + pltpu.roll(keep(mid), 8, 1) + pltpu.roll(keep(lo), 16, 1)).astype(BF16)

    def unpack3(x):
        return x + pltpu.roll(x, LANES - 8, 1) + pltpu.roll(x, LANES - 16, 1)

    logf = -_softplus(-(rw[:, fo:fo + LANES] + fb_ref[...]))
    c = unpack3(_dot(tri_ref[...], pack3(logf))) + carry_ref[...]
    carry_ref[...] = c[tm - 1:tm, :]

    k_bias = _dot(pack3(c * (-LOG2E)), sel_ref[...])
    scale = HEAD_DIM ** -0.5 * LOG2E
    for hp in range(n_heads // 2):
        qb = qk[:, hp * LANES:(hp + 1) * LANES] * scale
        kb = qk[:, fw + hp * LANES:fw + (hp + 1) * LANES]
        for h, keep in ((2 * hp, lower), (2 * hp + 1, jnp.logical_not(lower))):
            q_out[0, h] = jnp.where(keep, qb, qc_ref[:, h * LANES:(h + 1) * LANES]).astype(BF16)
            k_out[0, h] = jnp.where(keep, kb, k_bias[:, h * LANES:(h + 1) * LANES]).astype(BF16)

    one_even = jnp.where(lane == HEAD_DIM, 1.0, 0.0)
    one_odd = jnp.where(lane == 0, 1.0, 0.0)
    for hp in range(n_heads // 2):
        blk = v_fox[:, hp * LANES:(hp + 1) * LANES]
        v_out[0, 2 * hp] = jnp.where(lower, blk, one_even).astype(BF16)
        v_out[0, 2 * hp + 1] = jnp.where(lower, one_odd, blk).astype(BF16)


def _inproj(x, g_pre, wp, fbias, tri, sel, qconst, mu, rwkv_params, *, n_heads, rw_w, f_col, tm):
    B, T, D = x.shape
    assert T % tm == 0 and n_heads % 2 == 0 and tm == tri.shape[0] and tm % CHUNK == 0
    width = n_heads * HEAD_DIM
    kern = functools.partial(_inproj_kernel, n_heads=n_heads, rw_w=rw_w, f_col=f_col)
    consts = [g_pre, wp, fbias, tri, sel, qconst, mu] + list(rwkv_params)
    return pl.pallas_call(
        kern,
        grid=(B, T // tm),
        in_specs=[pl.BlockSpec((1, tm, D), lambda b, t: (b, t, 0))] + [_const_spec(a.shape) for a in consts],
        out_specs=[
            pl.BlockSpec((1, n_heads, tm, LANES), lambda b, t: (b, 0, t, 0)),
            pl.BlockSpec((1, n_heads, tm, LANES), lambda b, t: (b, 0, t, 0)),
            pl.BlockSpec((1, n_heads, tm, LANES), lambda b, t: (b, 0, t, 0)),
            pl.BlockSpec((1, tm, 5 * width), lambda b, t: (b, t, 0)),
            pl.BlockSpec((1, tm, 2 * width), lambda b, t: (b, t, 0)),
            pl.BlockSpec((1, tm // CHUNK, 1, width), lambda b, t: (b, t, 0, 0)),
        ],
        out_shape=[
            jax.ShapeDtypeStruct((B, n_heads, T, LANES), BF16),
            jax.ShapeDtypeStruct((B, n_heads, T, LANES), BF16),
            jax.ShapeDtypeStruct((B, n_heads, T, LANES), BF16),
            jax.ShapeDtypeStruct((B, T, 5 * width), BF16),
            jax.ShapeDtypeStruct((B, T, 2 * width), F32),
            jax.ShapeDtypeStruct((B, T // CHUNK, 1, width), F32),
        ],
        scratch_shapes=[pltpu.VMEM((1, LANES), F32),
                        pltpu.VMEM((1, rw_w), F32)],
        compiler_params=pltpu.CompilerParams(
            dimension_semantics=("arbitrary", "arbitrary"), vmem_limit_bytes=VMEM_LIMIT),
        name="inproj",
    )(x, *consts)


def _fox_kernel(q_ref, k_ref, v_ref, o_ref, m_scr, acc_scr, *, tq):
    qi = pl.program_id(2)
    heads = range(2)
    half = tq // 2

    def step(jobs):
        todo = [(n, h) for n in range(len(jobs)) for h in heads]
        qrows = [pl.ds(q0, nq) for q0, nq, _, _, _ in jobs]
        krows = [pl.ds(pl.multiple_of(k0, tq), tk) for _, _, k0, tk, _ in jobs]
        s = {(n, h): _dot_nt(q_ref[0, h, qrows[n], :], k_ref[0, h, krows[n], :]) for n, h in todo}
        for n, (_, nq, _, tk, masked) in enumerate(jobs):
            if masked:
                row = lax.broadcasted_iota(jnp.int32, (nq, nq), 0)
                col = lax.broadcasted_iota(jnp.int32, (nq, nq), 1)
                for h in heads:
                    edge = jnp.where(col <= row, s[n, h][:, tk - nq:], -jnp.inf)
                    s[n, h] = edge if tk == nq else jnp.concatenate([s[n, h][:, :tk - nq], edge], axis=1)
        m_prev = {(n, h): m_scr[h, qrows[n], :] for n, h in todo}
        m_next = {nh: jnp.maximum(m_prev[nh], jnp.max(s[nh], axis=1, keepdims=True)) for nh in todo}
        p = {(n, h): jnp.exp2(s[n, h] - jnp.concatenate([m_next[n, h]] * (jobs[n][3] // LANES), axis=1)
                              ).astype(BF16) for n, h in todo}
        for n, h in todo:
            alpha = jnp.exp2(m_prev[n, h] - m_next[n, h])
            acc_scr[h, qrows[n], :] = (alpha * acc_scr[h, qrows[n], :]
                                       + _dot(p[n, h], v_ref[0, h, krows[n], :]))
            m_scr[h, qrows[n], :] = m_next[n, h]

    for h in heads:
        m_scr[h] = jnp.full(m_scr.shape[1:], -jnp.inf, F32)
        acc_scr[h] = jnp.zeros(acc_scr.shape[1:], F32)

    def body(j, carry):
        step([(0, tq, j * tq, tq, False)])
        return carry

    lax.fori_loop(0, qi, body, 0)
    quarter = tq // 4
    step([(n * quarter, quarter, qi * tq, (n + 1) * quarter, True) for n in range(4)])

    lane = lax.broadcasted_iota(jnp.int32, acc_scr.shape[1:], 1)
    acc0 = acc_scr[0]
    acc1 = acc_scr[1]
    o = jnp.where(lane < HEAD_DIM, acc0 / acc0[:, HEAD_DIM:HEAD_DIM + 1], acc1 / acc1[:, 0:1])
    o_ref[0] = o.astype(o_ref.dtype)


def _fox(q_aug, k_aug, v_aug, *, tq):
    B, H, T, _ = q_aug.shape
    assert T % tq == 0 and H % 2 == 0
    kern = functools.partial(_fox_kernel, tq=tq)
    return pl.pallas_call(
        kern,
        grid=(B, H // 2, T // tq),
        in_specs=[
            pl.BlockSpec((1, 2, tq, LANES), lambda b, hp, qi: (b, hp, qi, 0)),
            pl.BlockSpec((1, 2, T, LANES), lambda b, hp, qi: (b, hp, 0, 0)),
            pl.BlockSpec((1, 2, T, LANES), lambda b, hp, qi: (b, hp, 0, 0)),
        ],
        out_specs=pl.BlockSpec((1, tq, LANES), lambda b, hp, qi: (b, qi, hp)),
        out_shape=jax.ShapeDtypeStruct((B, T, H * HEAD_DIM), BF16),
        scratch_shapes=[pltpu.VMEM((2, tq, LANES), F32)] * 2,
        compiler_params=pltpu.CompilerParams(
            dimension_semantics=("arbitrary", "arbitrary", "arbitrary"),
            vmem_limit_bytes=VMEM_LIMIT),
        name="fox",
    )(q_aug, k_aug, v_aug)


def _unit_lower_inverse(Ls, stack):
    shape = Ls[0].shape
    t = lax.broadcasted_iota(jnp.int32, shape, 0)
    j = lax.broadcasted_iota(jnp.int32, shape, 1) % shape[0]
    same = (t >> 1) == (j >> 1)
    eye = jnp.where(t == j, 1.0, 0.0)
    ps = [eye + jnp.where(same, L, 0.0) for L in Ls]
    for s in range(2, 7):
        merged = (t >> s) == (j >> s)
        sel = merged & jnp.logical_not(same)
        offs = [stack(jnp.where(sel, L, 0.0).astype(BF16)) for L in Ls]
        pbs = [p.astype(BF16) for p in ps]
        tmps = [_dot(pb, off).astype(BF16) for pb, off in zip(pbs, offs)]
        ps = [p + _dot(tmp, stack(pb)) for p, tmp, pb in zip(ps, tmps, pbs)]
        same = merged
    return ps


def _rwkv_kernel(ops_ref, bg_ref, gc_ref, lnw_ref, lnb_ref, ones_ref,
                 o_ref, s_scr, y_scr, *, width):
    t = pl.program_id(1)
    n_seq, tc, _ = ops_ref.shape
    n_rows = n_seq * tc
    n_chunks = tc // CHUNK
    n_groups = width // GROUP_W

    @pl.when(t == 0)
    def _():
        s_scr[...] = jnp.zeros_like(s_scr)

    ones_blk = ones_ref[...]

    def segsum(x):
        return _dot(x.astype(BF16), ones_blk)

    ops = ops_ref[...].reshape(n_rows, 5 * width)
    a_t = ops[:, 0:width]
    b_t = ops[:, width:2 * width]
    r_t = ops[:, 2 * width:3 * width]
    k_t = ops[:, 3 * width:4 * width]
    v_t = ops[:, 4 * width:]

    lane_head = lax.broadcasted_iota(jnp.int32, (CHUNK, GROUP_W), 1) // HEAD_DIM
    head_masks = [lane_head == h for h in range(GROUP_HEADS)]
    rt = lax.broadcasted_iota(jnp.int32, (GROUP_W, GROUP_W), 0)
    ct = lax.broadcasted_iota(jnp.int32, (GROUP_W, GROUP_W), 1)
    same_head = (rt // HEAD_DIM) == (ct // HEAD_DIM)
    diag = rt == ct
    tw = lax.broadcasted_iota(jnp.int32, (CHUNK, GROUP_W), 0)
    jw = lax.broadcasted_iota(jnp.int32, (CHUNK, GROUP_W), 1) % CHUNK
    strict_w = tw > jw
    incl_w = tw >= jw
    rows = lambda s, ci: slice((s * n_chunks + ci) * CHUNK, (s * n_chunks + ci + 1) * CHUNK)
    cols = lambda g: slice(g * GROUP_W, (g + 1) * GROUP_W)
    wide = lambda x, u: x[rows(u[0], u[1]), cols(u[2])]

    def stack(x):
        zero = jnp.zeros_like(x)
        return jnp.concatenate([jnp.where(mk, x, zero) for mk in head_masks], axis=0)

    units = [(s, ci, g) for ci in range(n_chunks) for s in range(n_seq) for g in range(n_groups)]
    b_s = {u: stack(wide(b_t, u)) for u in units}
    k_s = {u: stack(wide(k_t, u)) for u in units}
    v_s = {u: stack(wide(v_t, u)) for u in units}
    l_all = {u: _dot_nt(jnp.concatenate([wide(a_t, u), wide(r_t, u)], axis=0),
                        jnp.concatenate([b_s[u], k_s[u]], axis=0)) for u in units}
    l_ab = {u: jnp.where(strict_w, l_all[u][0:CHUNK, 0:GROUP_W], 0.0) for u in units}
    l_ak = {u: jnp.where(strict_w, l_all[u][0:CHUNK, GROUP_W:], 0.0).astype(BF16) for u in units}
    l_rb = {u: jnp.where(incl_w, l_all[u][CHUNK:, 0:GROUP_W], 0.0).astype(BF16) for u in units}
    l_rk = {u: jnp.where(incl_w, l_all[u][CHUNK:, GROUP_W:], 0.0).astype(BF16) for u in units}
    av = {u: _dot(jnp.concatenate([l_ak[u], l_rk[u]], axis=0), v_s[u]) for u in units}
    akv = {u: av[u][0:CHUNK] for u in units}
    y_rk = {u: av[u][CHUNK:] for u in units}
    t_inv = {u: t.astype(BF16) for u, t in
             zip(units, _unit_lower_inverse([l_ab[u] for u in units], stack))}
    gc = {u: gc_ref[u[0], u[1], :, cols(u[2])] for u in units}
    gcol = {u: jnp.sum(jnp.where(diag, jnp.broadcast_to(gc[u], (GROUP_W, GROUP_W)), 0.0),
                       axis=1, keepdims=True) for u in units}
    bk_h = {u: jnp.concatenate([(wide(b_t, u).astype(F32) * gc[u]).astype(BF16),
                                (wide(k_t, u).astype(F32) * gc[u]).astype(BF16)], axis=0) for u in units}

    chains = [(s, g) for s in range(n_seq) for g in range(n_groups)]
    state = {c: s_scr[c[0] * n_groups + c[1]] for c in chains}
    for ci in range(n_chunks):
        level = [(s, ci, g) for s, g in chains]
        zr = {u: _dot(jnp.concatenate([wide(a_t, u), wide(r_t, u)], axis=0),
                      state[u[0], u[2]].astype(BF16)) for u in level}
        z_b = {u: _dot(t_inv[u], stack((zr[u][0:CHUNK] + akv[u]).astype(BF16))).astype(BF16)
               for u in level}
        upd = {u: _dot_tn(bk_h[u], jnp.concatenate([z_b[u], wide(v_t, u)], axis=0)) for u in level}
        for u in level:
            state[u[0], u[2]] = state[u[0], u[2]] * gcol[u] + jnp.where(same_head, upd[u], 0.0)
        for u in level:
            y_scr[rows(u[0], ci), cols(u[2])] = (zr[u][CHUNK:] + _dot(l_rb[u], stack(z_b[u]))) + y_rk[u]
    for c in chains:
        s_scr[c[0] * n_groups + c[1]] = state[c]

    y = y_scr[...]
    d = y - segsum(y) * (1.0 / HEAD_DIM)
    var = segsum(d * d) * (1.0 / HEAD_DIM)
    yn = d * lax.rsqrt(var + LNX_EPS) * lnw_ref[...] + lnb_ref[...]
    bg = bg_ref[...].reshape(n_rows, 2 * width)
    o_ref[...] = ((yn + bg[:, 0:width]) * bg[:, width:]).astype(o_ref.dtype).reshape(n_seq, tc, width)


def _rwkv(ops, bg, gc, ln_w, ln_b, ones_blk, *, width, tc, n_seq):
    B, T, _ = ops.shape
    assert B % n_seq == 0 and T % tc == 0 and tc % CHUNK == 0 and width % GROUP_W == 0
    n_groups = width // GROUP_W
    kern = functools.partial(_rwkv_kernel, width=width)
    small = [ln_w, ln_b, ones_blk]
    return pl.pallas_call(
        kern,
        grid=(B // n_seq, T // tc),
        in_specs=[
            pl.BlockSpec((n_seq, tc, 5 * width), lambda b, t: (b, t, 0)),
            pl.BlockSpec((n_seq, tc, 2 * width), lambda b, t: (b, t, 0)),
            pl.BlockSpec((n_seq, tc // CHUNK, 1, width), lambda b, t: (b, t, 0, 0)),
        ] + [_const_spec(a.shape) for a in small],
        out_specs=pl.BlockSpec((n_seq, tc, width), lambda b, t: (b, t, 0)),
        out_shape=jax.ShapeDtypeStruct((B, T, width), BF16),
        scratch_shapes=[
            pltpu.VMEM((n_seq * n_groups, GROUP_W, GROUP_W), F32),
            pltpu.VMEM((n_seq * tc, width), F32),
        ],
        compiler_params=pltpu.CompilerParams(
            dimension_semantics=("arbitrary", "arbitrary"), vmem_limit_bytes=VMEM_LIMIT),
        name="rwkv",
    )(ops, bg, gc, *small)


def _outffn_kernel(x_ref, of_ref, or_ref, wof_ref, wor_ref, gpost_ref, gfpre_ref, gfpost_ref,
                   wg_ref, wu_ref, wd_ref, o_ref, *, n_sub):
    tm = x_ref.shape[0]
    sub = tm // n_sub
    parts = [slice(i * sub, (i + 1) * sub) for i in range(n_sub)]
    mix = [_dot(of_ref[sl, :], wof_ref[...]) + _dot(or_ref[sl, :], wor_ref[...]) for sl in parts]
    h = [x_ref[sl, :] + _rms(m) * gpost_ref[...] for sl, m in zip(parts, mix)]
    z = [(_rms(hh) * gfpre_ref[...]).astype(BF16) for hh in h]
    gate = [_dot(zz, wg_ref[...]) for zz in z]
    up = [_dot(zz, wu_ref[...]) for zz in z]
    act = [(g * _sigmoid(g) * u).astype(BF16) for g, u in zip(gate, up)]
    f = [_dot(a, wd_ref[...]) for a in act]
    for sl, hh, ff in zip(parts, h, f):
        o_ref[sl, :] = hh + _rms(ff) * gfpost_ref[...]


def _outffn(x2, o_fox, o_rw, wof, wor, g_post, g_fpre, g_fpost, wg, wu, wd, *, tm, n_sub):
    N, D = x2.shape
    assert N % tm == 0 and tm % n_sub == 0
    consts = [wof, wor, g_post, g_fpre, g_fpost, wg, wu, wd]
    return pl.pallas_call(
        functools.partial(_outffn_kernel, n_sub=n_sub),
        grid=(N // tm,),
        in_specs=[
            pl.BlockSpec((tm, D), lambda i: (i, 0)),
            pl.BlockSpec((tm, o_fox.shape[1]), lambda i: (i, 0)),
            pl.BlockSpec((tm, o_rw.shape[1]), lambda i: (i, 0)),
        ] + [_const_spec(a.shape) for a in consts],
        out_specs=pl.BlockSpec((tm, D), lambda i: (i, 0)),
        out_shape=jax.ShapeDtypeStruct((N, D), F32),
        compiler_params=pltpu.CompilerParams(
            dimension_semantics=("arbitrary",), vmem_limit_bytes=VMEM_LIMIT),
        name="outffn",
    )(x2, o_fox, o_rw, *consts)


def _pad_cols(a, n):
    return jnp.pad(a, ((0, 0), (0, n - a.shape[1])))


def _pad_rows(a, n):
    return jnp.pad(a, ((0, n - a.shape[0]), (0, 0)))


def _block_layer(h, attn_norm_pre, attn_norm_post, w_in, fox_forget_bias, shift_mu, rwkv_w0,
                 rwkv_w_up, rwkv_a0, rwkv_a_up, rwkv_g_up, rwkv_k_k, rwkv_k_a, rwkv_r_k,
                 rwkv_ln_w, rwkv_ln_b, w_out, ffn_norm_pre, ffn_norm_post, ffn_w_gate, ffn_w_up,
                 ffn_w_down):
    B, T, D = h.shape
    fw = D // 2
    rwid = D // 2
    nh = fw // HEAD_DIM
    hw = nh * LANES

    fq, fk, fv = w_in[:, 0:fw], w_in[:, fw:2 * fw], w_in[:, 2 * fw:3 * fw]
    ffw = w_in[:, 3 * fw:3 * fw + nh]
    rcol = 3 * fw + nh

    def rw_layout(a, spare):
        o = 3 * rwid
        return jnp.concatenate([
            a[:, 0:o],
            _pad_cols(jnp.concatenate([a[:, o:o + DECAY_LORA], spare], axis=1), LANES),
            _pad_cols(a[:, o + DECAY_LORA:o + DECAY_LORA + AAA_LORA], LANES),
            _pad_cols(a[:, o + DECAY_LORA + AAA_LORA:], 2 * LANES)], axis=1)

    w_rw = rw_layout(w_in[:, rcol:], ffw)
    rw_w = w_rw.shape[1]
    wp = jnp.concatenate([fq, fk, fv, w_rw], axis=1).astype(BF16)
    mu = rw_layout(shift_mu[None, :], jnp.zeros((1, nh), F32))
    fbias = _pad_cols(jnp.pad(fox_forget_bias[None, :], ((0, 0), (FORGET_LANE, 0))), LANES)

    tm1 = 512
    ii = jnp.arange(tm1)
    tri = (ii[:, None] >= ii[None, :]).astype(BF16)
    bias_lane = lambda hd, p: hd * LANES + (HEAD_DIM if hd % 2 == 0 else 0) + 3 + p
    sel = np.zeros((LANES, hw), np.float32)
    qconst = np.zeros((1, hw), np.float32)
    for hd in range(nh):
        for p in range(3):
            sel[FORGET_LANE + hd + 8 * p, bias_lane(hd, p)] = 1.0
            qconst[0, bias_lane(hd, p)] = 1.0
    sel = jnp.asarray(sel, BF16)
    qconst = jnp.asarray(qconst)

    hh = jnp.arange(rwid) // HEAD_DIM
    ones_blk = (hh[:, None] == hh[None, :]).astype(BF16)
    jj = jnp.arange(2 * CHUNK)
    tri_blk = ((jj[:, None] >= jj[None, :]) & (jj[:, None] // CHUNK == jj[None, :] // CHUNK)).astype(BF16)
    row = lambda a: a.reshape(1, -1)
    rwkv_params = [row(rwkv_w0), row(rwkv_a0), row(rwkv_k_k), row(rwkv_k_a), row(rwkv_r_k),
                   _pad_rows(rwkv_w_up, LANES).astype(BF16), _pad_rows(rwkv_a_up, LANES).astype(BF16),
                   _pad_rows(rwkv_g_up, 2 * LANES).astype(BF16), ones_blk, tri_blk]
    q_aug, k_aug, v, ops, bg, gc = _inproj(h, attn_norm_pre[None, :], wp, fbias, tri, sel, qconst, mu,
                                           rwkv_params, n_heads=nh, rw_w=rw_w, f_col=3 * rwid, tm=tm1)

    o_fox = _fox(q_aug, k_aug, v, tq=1024)

    o_rw = _rwkv(ops, bg, gc, row(rwkv_ln_w), row(rwkv_ln_b), ones_blk, width=rwid, tc=128, n_seq=4)

    out = _outffn(h.reshape(B * T, D), o_fox.reshape(B * T, fw), o_rw.reshape(B * T, rwid),
                  w_out[0:fw].astype(BF16), w_out[fw:].astype(BF16),
                  row(attn_norm_post), row(ffn_norm_pre), row(ffn_norm_post),
                  ffn_w_gate.astype(BF16), ffn_w_up.astype(BF16), ffn_w_down.astype(BF16),
                  tm=512, n_sub=2)
    return out.reshape(B, T, D)


def kernel(x, attn_norm_pre, attn_norm_post, w_in, fox_forget_bias, shift_mu, rwkv_w0, rwkv_w_up,
           rwkv_a0, rwkv_a_up, rwkv_g_up, rwkv_k_k, rwkv_k_a, rwkv_r_k, rwkv_ln_w, rwkv_ln_b, w_out,
           ffn_norm_pre, ffn_norm_post, ffn_w_gate, ffn_w_up, ffn_w_down):
    h = x
    for l in range(attn_norm_pre.shape[0]):
        h = _block_layer(h, attn_norm_pre[l], attn_norm_post[l], w_in[l], fox_forget_bias[l],
                         shift_mu[l], rwkv_w0[l], rwkv_w_up[l], rwkv_a0[l], rwkv_a_up[l],
                         rwkv_g_up[l], rwkv_k_k[l], rwkv_k_a[l], rwkv_r_k[l], rwkv_ln_w[l],
                         rwkv_ln_b[l], w_out[l], ffn_norm_pre[l], ffn_norm_post[l], ffn_w_gate[l],
                         ffn_w_up[l], ffn_w_down[l])
    return h
```

```python
import functools

import jax
import jax.numpy as jnp
import numpy as np
from jax import lax
from jax.experimental import pallas as pl
from jax.experimental.pallas import tpu as pltpu

F32 = jnp.float32
BF16 = jnp.bfloat16

HEAD_DIM = 64
LANES = 128
NORM_EPS = 1e-6
LNX_EPS = 64e-5
LOG2E = 1.4426950408889634
DECAY_LORA = 64
AAA_LORA = 64
GATE_LORA = 160
CHUNK = 64
GROUP_HEADS = 4
GROUP_W = GROUP_HEADS * HEAD_DIM
FORGET_LANE = DECAY_LORA
VMEM_LIMIT = 56 * 1024 * 1024


def _dot(a, b):
    return jnp.dot(a, b, preferred_element_type=F32)


def _dot_nt(a, b):
    return lax.dot_general(a, b, (((1,), (1,)), ((), ())), preferred_element_type=F32)


def _dot_tn(a, b):
    return lax.dot_general(a, b, (((0,), (0,)), ((), ())), preferred_element_type=F32)


def _split2(x):
    hi = x.astype(BF16)
    lo = (x - hi.astype(F32)).astype(BF16)
    return hi, lo


def _split3(x):
    hi = x.astype(BF16)
    r = x - hi.astype(F32)
    mid = r.astype(BF16)
    lo = (r - mid.astype(F32)).astype(BF16)
    return hi, mid, lo


def _dot_exact_lhs01(ones01, x):
    hi, mid, lo = _split3(x)
    return _dot(ones01, hi) + (_dot(ones01, mid) + _dot(ones01, lo))


def _rms(x):
    return x * lax.rsqrt(jnp.mean(x * x, axis=-1, keepdims=True) + NORM_EPS)


def _softplus(x):
    return jnp.maximum(x, 0.0) + jnp.log1p(jnp.exp(-jnp.abs(x)))


def _sigmoid(x):
    return 0.5 * jnp.tanh(0.5 * x) + 0.5


def _const_spec(shape):
    nd = len(shape)
    return pl.BlockSpec(shape, lambda *_: (0,) * nd, pipeline_mode=pl.Buffered(1))


def _inproj_kernel(x_ref, g_ref, w_ref, fb_ref, tri_ref, sel_ref, qc_ref, mu_ref,
                   q_out, k_out, v_out, rw_out, carry_ref, rwlast_ref, *, n_heads, rw_w, f_col):
    t = pl.program_id(1)

    @pl.when(t == 0)
    def _():
        carry_ref[...] = jnp.zeros_like(carry_ref)
        rwlast_ref[...] = jnp.zeros_like(rwlast_ref)

    fw = n_heads * HEAD_DIM
    tm = x_ref.shape[1]
    u = (_rms(x_ref[0]) * g_ref[...]).astype(BF16)
    lane = lax.broadcasted_iota(jnp.int32, (tm, LANES), 1)
    lower = lane < HEAD_DIM

    rw = _dot(u, w_ref[:, 3 * fw:3 * fw + rw_w])
    row0 = lax.broadcasted_iota(jnp.int32, (tm, 1), 0) == 0
    rw_prev = jnp.where(row0, rwlast_ref[...], pltpu.roll(rw, 1, 0))
    rwlast_ref[...] = rw[tm - 1:tm, :]
    rw_out[0] = rw + mu_ref[...] * (rw_prev - rw)

    fo = f_col
    in_f = (lane >= FORGET_LANE) & (lane < FORGET_LANE + n_heads)

    def pack3(x):
        hi = x.astype(BF16).astype(F32)
        r1 = x - hi
        mid = r1.astype(BF16).astype(F32)
        lo = r1 - mid
        keep = lambda a: jnp.where(in_f, a, 0.0)
        return (keep(hi) + pltpu.roll(keep(mid), 8, 1) + pltpu.roll(keep(lo), 16, 1)).astype(BF16)

    def unpack3(x):
        return x + pltpu.roll(x, LANES - 8, 1) + pltpu.roll(x, LANES - 16, 1)

    logf = -_softplus(-(rw[:, fo:fo + LANES] + fb_ref[...]))
    c = unpack3(_dot(tri_ref[...], pack3(logf))) + carry_ref[...]
    carry_ref[...] = c[tm - 1:tm, :]

    qk = _dot(u, w_ref[:, 0:2 * fw])
    k_bias = _dot(pack3(c * (-LOG2E)), sel_ref[...])
    scale = HEAD_DIM ** -0.5 * LOG2E
    for hp in range(n_heads // 2):
        qb = qk[:, hp * LANES:(hp + 1) * LANES] * scale
        kb = qk[:, fw + hp * LANES:fw + (hp + 1) * LANES]
        for h, keep in ((2 * hp, lower), (2 * hp + 1, jnp.logical_not(lower))):
            q_out[0, h] = jnp.where(keep, qb, qc_ref[:, h * LANES:(h + 1) * LANES]).astype(BF16)
            k_out[0, h] = jnp.where(keep, kb, k_bias[:, h * LANES:(h + 1) * LANES]).astype(BF16)

    v = _dot(u, w_ref[:, 2 * fw:3 * fw])
    one_even = jnp.where(lane == HEAD_DIM, 1.0, 0.0)
    one_odd = jnp.where(lane == 0, 1.0, 0.0)
    for hp in range(n_heads // 2):
        blk = v[:, hp * LANES:(hp + 1) * LANES]
        v_out[0, 2 * hp] = jnp.where(lower, blk, one_even).astype(BF16)
        v_out[0, 2 * hp + 1] = jnp.where(lower, one_odd, blk).astype(BF16)


def _inproj(x, g_pre, wp, fbias, tri, sel, qconst, mu, *, n_heads, rw_w, f_col, tm):
    B, T, D = x.shape
    assert T % tm == 0 and n_heads % 2 == 0 and tm == tri.shape[0]
    kern = functools.partial(_inproj_kernel, n_heads=n_heads, rw_w=rw_w, f_col=f_col)
    return pl.pallas_call(
        kern,
        grid=(B, T // tm),
        in_specs=[
            pl.BlockSpec((1, tm, D), lambda b, t: (b, t, 0)),
            _const_spec(g_pre.shape),
            _const_spec(wp.shape),
            _const_spec(fbias.shape),
            _const_spec(tri.shape),
            _const_spec(sel.shape),
            _const_spec(qconst.shape),
            _const_spec(mu.shape),
        ],
        out_specs=[
            pl.BlockSpec((1, n_heads, tm, LANES), lambda b, t: (b, 0, t, 0)),
            pl.BlockSpec((1, n_heads, tm, LANES), lambda b, t: (b, 0, t, 0)),
            pl.BlockSpec((1, n_heads, tm, LANES), lambda b, t: (b, 0, t, 0)),
            pl.BlockSpec((1, tm, rw_w), lambda b, t: (b, t, 0)),
        ],
        out_shape=[
            jax.ShapeDtypeStruct((B, n_heads, T, LANES), BF16),
            jax.ShapeDtypeStruct((B, n_heads, T, LANES), BF16),
            jax.ShapeDtypeStruct((B, n_heads, T, LANES), BF16),
            jax.ShapeDtypeStruct((B, T, rw_w), F32),
        ],
        scratch_shapes=[pltpu.VMEM((1, LANES), F32),
                        pltpu.VMEM((1, rw_w), F32)],
        compiler_params=pltpu.CompilerParams(
            dimension_semantics=("arbitrary", "arbitrary"), vmem_limit_bytes=VMEM_LIMIT),
        name="inproj",
    )(x, g_pre, wp, fbias, tri, sel, qconst, mu)


def _fox_kernel(q_ref, k_ref, v_ref, o_ref, m_scr, acc_scr, *, tq):
    qi = pl.program_id(2)
    heads = range(q_ref.shape[1])

    def step(jobs):
        todo = [(n, h) for n in range(len(jobs)) for h in heads]
        qrows = [pl.ds(q0, nq) for q0, nq, _, _, _ in jobs]
        krows = [pl.ds(pl.multiple_of(k0, tq), tk) for _, _, k0, tk, _ in jobs]
        s = {(n, h): _dot_nt(q_ref[0, h, qrows[n], :], k_ref[0, h, krows[n], :]) for n, h in todo}
        for n, (_, nq, _, tk, masked) in enumerate(jobs):
            if masked:
                row = lax.broadcasted_iota(jnp.int32, (nq, nq), 0)
                col = lax.broadcasted_iota(jnp.int32, (nq, nq), 1)
                for h in heads:
                    edge = jnp.where(col <= row, s[n, h][:, tk - nq:], -jnp.inf)
                    s[n, h] = edge if tk == nq else jnp.concatenate([s[n, h][:, :tk - nq], edge], axis=1)
        m_prev = {(n, h): m_scr[h, qrows[n], :] for n, h in todo}
        m_next = {nh: jnp.maximum(m_prev[nh], jnp.max(s[nh], axis=1, keepdims=True)) for nh in todo}
        p = {(n, h): jnp.exp2(s[n, h] - jnp.concatenate([m_next[n, h]] * (jobs[n][3] // LANES), axis=1)
                              ).astype(BF16) for n, h in todo}
        for n, h in todo:
            alpha = jnp.exp2(m_prev[n, h] - m_next[n, h])
            acc_scr[h, qrows[n], :] = (alpha * acc_scr[h, qrows[n], :]
                                       + _dot(p[n, h], v_ref[0, h, krows[n], :]))
            m_scr[h, qrows[n], :] = m_next[n, h]

    for h in heads:
        m_scr[h] = jnp.full(m_scr.shape[1:], -jnp.inf, F32)
        acc_scr[h] = jnp.zeros(acc_scr.shape[1:], F32)

    def body(j, carry):
        step([(0, tq, j * tq, tq, False)])
        return carry

    lax.fori_loop(0, qi, body, 0)
    quarter = tq // 4
    step([(n * quarter, quarter, qi * tq, (n + 1) * quarter, True) for n in range(4)])

    lane = lax.broadcasted_iota(jnp.int32, acc_scr.shape[1:], 1)
    for hp in range(len(heads) // 2):
        acc0 = acc_scr[2 * hp]
        acc1 = acc_scr[2 * hp + 1]
        o = jnp.where(lane < HEAD_DIM, acc0 / acc0[:, HEAD_DIM:HEAD_DIM + 1], acc1 / acc1[:, 0:1])
        o_ref[0, :, hp * LANES:(hp + 1) * LANES] = o.astype(o_ref.dtype)


def _fox(q_aug, k_aug, v_aug, *, tq, n_h):
    B, H, T, _ = q_aug.shape
    assert T % tq == 0 and H % n_h == 0 and n_h % 2 == 0
    kern = functools.partial(_fox_kernel, tq=tq)
    return pl.pallas_call(
        kern,
        grid=(B, H // n_h, T // tq),
        in_specs=[
            pl.BlockSpec((1, n_h, tq, LANES), lambda b, hg, qi: (b, hg, qi, 0)),
            pl.BlockSpec((1, n_h, T, LANES), lambda b, hg, qi: (b, hg, 0, 0)),
            pl.BlockSpec((1, n_h, T, LANES), lambda b, hg, qi: (b, hg, 0, 0)),
        ],
        out_specs=pl.BlockSpec((1, tq, n_h * HEAD_DIM), lambda b, hg, qi: (b, qi, hg)),
        out_shape=jax.ShapeDtypeStruct((B, T, H * HEAD_DIM), BF16),
        scratch_shapes=[pltpu.VMEM((n_h, tq, LANES), F32)] * 2,
        compiler_params=pltpu.CompilerParams(
            dimension_semantics=("arbitrary", "arbitrary", "arbitrary"),
            vmem_limit_bytes=VMEM_LIMIT),
        name="fox",
    )(q_aug, k_aug, v_aug)


def _unit_lower_inverse(Ls, stack):
    shape = Ls[0].shape
    t = lax.broadcasted_iota(jnp.int32, shape, 0)
    j = lax.broadcasted_iota(jnp.int32, shape, 1) % shape[0]
    same = (t >> 1) == (j >> 1)
    eye = jnp.where(t == j, 1.0, 0.0)
    ps = [eye + jnp.where(same, L, 0.0) for L in Ls]
    for s in range(2, 7):
        merged = (t >> s) == (j >> s)
        sel = merged & jnp.logical_not(same)
        offs = [stack(jnp.where(sel, L, 0.0).astype(BF16)) for L in Ls]
        pbs = [p.astype(BF16) for p in ps]
        tmps = [_dot(pb, off).astype(BF16) for pb, off in zip(pbs, offs)]
        ps = [p + _dot(tmp, stack(pb)) for p, tmp, pb in zip(ps, tmps, pbs)]
        same = merged
    return ps


def _rwkv_kernel(rw_ref, w0_ref, a0_ref, kk_ref, ka_ref, rk_ref,
                 lnw_ref, lnb_ref, wup_ref, aup_ref, gup_ref, ones_ref, tri_ref,
                 o_ref, s_scr, y_scr, *, width):
    t = pl.program_id(1)
    n_seq, tc, rw_w = rw_ref.shape
    n_rows = n_seq * tc
    n_chunks = tc // CHUNK
    n_groups = width // GROUP_W

    @pl.when(t == 0)
    def _():
        s_scr[...] = jnp.zeros_like(s_scr)

    ones_blk = ones_ref[...]

    def segsum(x):
        return _dot(x.astype(BF16), ones_blk)

    p = rw_ref[...].reshape(n_rows, rw_w)
    r = p[:, 0:width]
    k = p[:, width:2 * width]
    v = p[:, 2 * width:3 * width]
    o = 3 * width
    w_lat = p[:, o:o + LANES]
    a_lat = p[:, o + LANES:o + 2 * LANES]
    g_lat = p[:, o + 2 * LANES:o + 4 * LANES]

    z = w0_ref[...] + _dot(jnp.tanh(w_lat).astype(BF16), wup_ref[...])
    g_log = _sigmoid(z) * (-(2.718281828459045 ** -0.5))
    lr = _sigmoid(a0_ref[...] + _dot(a_lat.astype(BF16), aup_ref[...]))
    gate = _dot(_sigmoid(g_lat).astype(BF16), gup_ref[...])
    kk = k * kk_ref[...]
    kk = kk * lax.rsqrt(jnp.maximum(segsum(kk * kk), 1e-24))
    k = k * (1.0 + (lr - 1.0) * ka_ref[...])
    bonus = segsum(r * k * rk_ref[...]) * v

    tri = tri_ref[...]
    G = jnp.concatenate([_dot_exact_lhs01(tri, g_log[s * tc:(s + 1) * tc]) for s in range(n_seq)], axis=0)
    e_pos = jnp.exp(G)
    e_neg = jnp.exp(-G)
    a_t = (-kk * jnp.exp(G - g_log)).astype(BF16)
    b_t = (kk * lr * e_neg).astype(BF16)
    r_t = (r * e_pos).astype(BF16)
    k_t = (k * e_neg).astype(BF16)
    v_t = v.astype(BF16)

    lane_head = lax.broadcasted_iota(jnp.int32, (CHUNK, GROUP_W), 1) // HEAD_DIM
    head_masks = [lane_head == h for h in range(GROUP_HEADS)]
    rt = lax.broadcasted_iota(jnp.int32, (GROUP_W, GROUP_W), 0)
    ct = lax.broadcasted_iota(jnp.int32, (GROUP_W, GROUP_W), 1)
    same_head = (rt // HEAD_DIM) == (ct // HEAD_DIM)
    diag = rt == ct
    tw = lax.broadcasted_iota(jnp.int32, (CHUNK, GROUP_W), 0)
    jw = lax.broadcasted_iota(jnp.int32, (CHUNK, GROUP_W), 1) % CHUNK
    strict_w = tw > jw
    incl_w = tw >= jw
    rows = lambda s, ci: slice((s * n_chunks + ci) * CHUNK, (s * n_chunks + ci + 1) * CHUNK)
    cols = lambda g: slice(g * GROUP_W, (g + 1) * GROUP_W)
    wide = lambda x, u: x[rows(u[0], u[1]), cols(u[2])]

    def stack(x):
        zero = jnp.zeros_like(x)
        return jnp.concatenate([jnp.where(mk, x, zero) for mk in head_masks], axis=0)

    units = [(s, ci, g) for ci in range(n_chunks) for s in range(n_seq) for g in range(n_groups)]
    b_s = {u: stack(wide(b_t, u)) for u in units}
    k_s = {u: stack(wide(k_t, u)) for u in units}
    v_s = {u: stack(wide(v_t, u)) for u in units}
    l_all = {u: _dot_nt(jnp.concatenate([wide(a_t, u), wide(r_t, u)], axis=0),
                        jnp.concatenate([b_s[u], k_s[u]], axis=0)) for u in units}
    l_ab = {u: jnp.where(strict_w, l_all[u][0:CHUNK, 0:GROUP_W], 0.0) for u in units}
    l_ak = {u: jnp.where(strict_w, l_all[u][0:CHUNK, GROUP_W:], 0.0).astype(BF16) for u in units}
    l_rb = {u: jnp.where(incl_w, l_all[u][CHUNK:, 0:GROUP_W], 0.0).astype(BF16) for u in units}
    l_rk = {u: jnp.where(incl_w, l_all[u][CHUNK:, GROUP_W:], 0.0).astype(BF16) for u in units}
    av = {u: _dot(jnp.concatenate([l_ak[u], l_rk[u]], axis=0), v_s[u]) for u in units}
    akv = {u: av[u][0:CHUNK] for u in units}
    y_rk = {u: av[u][CHUNK:] for u in units}
    t_inv = {u: t.astype(BF16) for u, t in
             zip(units, _unit_lower_inverse([l_ab[u] for u in units], stack))}
    last = lambda u: (u[0] * n_chunks + u[1] + 1) * CHUNK - 1
    gc = {u: e_pos[last(u):last(u) + 1, cols(u[2])] for u in units}
    gcol = {u: jnp.sum(jnp.where(diag, jnp.broadcast_to(gc[u], (GROUP_W, GROUP_W)), 0.0),
                       axis=1, keepdims=True) for u in units}
    bk_h = {u: jnp.concatenate([(wide(b_t, u).astype(F32) * gc[u]).astype(BF16),
                                (wide(k_t, u).astype(F32) * gc[u]).astype(BF16)], axis=0) for u in units}

    chains = [(s, g) for s in range(n_seq) for g in range(n_groups)]
    state = {c: s_scr[c[0] * n_groups + c[1]] for c in chains}
    for ci in range(n_chunks):
        level = [(s, ci, g) for s, g in chains]
        zr = {u: _dot(jnp.concatenate([wide(a_t, u), wide(r_t, u)], axis=0),
                      state[u[0], u[2]].astype(BF16)) for u in level}
        z_b = {u: _dot(t_inv[u], stack((zr[u][0:CHUNK] + akv[u]).astype(BF16))).astype(BF16)
               for u in level}
        upd = {u: _dot_tn(bk_h[u], jnp.concatenate([z_b[u], wide(v_t, u)], axis=0)) for u in level}
        for u in level:
            state[u[0], u[2]] = state[u[0], u[2]] * gcol[u] + jnp.where(same_head, upd[u], 0.0)
        for u in level:
            y_scr[rows(u[0], ci), cols(u[2])] = (zr[u][CHUNK:] + _dot(l_rb[u], stack(z_b[u]))) + y_rk[u]
    for c in chains:
        s_scr[c[0] * n_groups + c[1]] = state[c]

    y = y_scr[...]
    d = y - segsum(y) * (1.0 / HEAD_DIM)
    var = segsum(d * d) * (1.0 / HEAD_DIM)
    yn = d * lax.rsqrt(var + LNX_EPS) * lnw_ref[...] + lnb_ref[...]
    o_ref[...] = ((yn + bonus) * gate).astype(o_ref.dtype).reshape(n_seq, tc, width)


def _rwkv(rw, w0, a0, k_k, k_a, r_k, ln_w, ln_b, w_up, a_up, g_up, ones_blk, tri_blk, *,
          width, tc, n_seq):
    B, T, rw_w = rw.shape
    assert B % n_seq == 0 and T % tc == 0 and tc % CHUNK == 0 and width % GROUP_W == 0
    n_groups = width // GROUP_W
    kern = functools.partial(_rwkv_kernel, width=width)
    small = [w0, a0, k_k, k_a, r_k, ln_w, ln_b, w_up, a_up, g_up, ones_blk, tri_blk]
    return pl.pallas_call(
        kern,
        grid=(B // n_seq, T // tc),
        in_specs=[
            pl.BlockSpec((n_seq, tc, rw_w), lambda b, t: (b, t, 0)),
        ] + [_const_spec(a.shape) for a in small],
        out_specs=pl.BlockSpec((n_seq, tc, width), lambda b, t: (b, t, 0)),
        out_shape=jax.ShapeDtypeStruct((B, T, width), BF16),
        scratch_shapes=[
            pltpu.VMEM((n_seq * n_groups, GROUP_W, GROUP_W), F32),
            pltpu.VMEM((n_seq * tc, width), F32),
        ],
        compiler_params=pltpu.CompilerParams(
            dimension_semantics=("arbitrary", "arbitrary"), vmem_limit_bytes=VMEM_LIMIT),
        name="rwkv",
    )(rw, *small)


def _outffn_kernel(x_ref, of_ref, or_ref, wof_ref, wor_ref, gpost_ref, gfpre_ref, gfpost_ref,
                   wg_ref, wu_ref, wd_ref, o_ref, *, n_sub):
    tm = x_ref.shape[0]
    sub = tm // n_sub
    parts = [slice(i * sub, (i + 1) * sub) for i in range(n_sub)]
    mix = [_dot(of_ref[sl, :], wof_ref[...]) + _dot(or_ref[sl, :], wor_ref[...]) for sl in parts]
    h = [x_ref[sl, :] + _rms(m) * gpost_ref[...] for sl, m in zip(parts, mix)]
    z = [(_rms(hh) * gfpre_ref[...]).astype(BF16) for hh in h]
    gate = [_dot(zz, wg_ref[...]) for zz in z]
    up = [_dot(zz, wu_ref[...]) for zz in z]
    act = [(g * _sigmoid(g) * u).astype(BF16) for g, u in zip(gate, up)]
    f = [_dot(a, wd_ref[...]) for a in act]
    for sl, hh, ff in zip(parts, h, f):
        o_ref[sl, :] = hh + _rms(ff) * gfpost_ref[...]


def _outffn(x2, o_fox, o_rw, wof, wor, g_post, g_fpre, g_fpost, wg, wu, wd, *, tm, n_sub):
    N, D = x2.shape
    assert N % tm == 0 and tm % n_sub == 0
    consts = [wof, wor, g_post, g_fpre, g_fpost, wg, wu, wd]
    return pl.pallas_call(
        functools.partial(_outffn_kernel, n_sub=n_sub),
        grid=(N // tm,),
        in_specs=[
            pl.BlockSpec((tm, D), lambda i: (i, 0)),
            pl.BlockSpec((tm, o_fox.shape[1]), lambda i: (i, 0)),
            pl.BlockSpec((tm, o_rw.shape[1]), lambda i: (i, 0)),
        ] + [_const_spec(a.shape) for a in consts],
        out_specs=pl.BlockSpec((tm, D), lambda i: (i, 0)),
        out_shape=jax.ShapeDtypeStruct((N, D), F32),
        compiler_params=pltpu.CompilerParams(
            dimension_semantics=("arbitrary",), vmem_limit_bytes=VMEM_LIMIT),
        name="outffn",
    )(x2, o_fox, o_rw, *consts)


def _pad_cols(a, n):
    return jnp.pad(a, ((0, 0), (0, n - a.shape[1])))


def _pad_rows(a, n):
    return jnp.pad(a, ((0, n - a.shape[0]), (0, 0)))


def _block_layer(h, attn_norm_pre, attn_norm_post, w_in, fox_forget_bias, shift_mu, rwkv_w0,
                 rwkv_w_up, rwkv_a0, rwkv_a_up, rwkv_g_up, rwkv_k_k, rwkv_k_a, rwkv_r_k,
                 rwkv_ln_w, rwkv_ln_b, w_out, ffn_norm_pre, ffn_norm_post, ffn_w_gate, ffn_w_up,
                 ffn_w_down):
    B, T, D = h.shape
    fw = D // 2
    rwid = D // 2
    nh = fw // HEAD_DIM
    hw = nh * LANES

    fq, fk, fv = w_in[:, 0:fw], w_in[:, fw:2 * fw], w_in[:, 2 * fw:3 * fw]
    ffw = w_in[:, 3 * fw:3 * fw + nh]
    rcol = 3 * fw + nh

    def rw_layout(a, spare):
        o = 3 * rwid
        return jnp.concatenate([
            a[:, 0:o],
            _pad_cols(jnp.concatenate([a[:, o:o + DECAY_LORA], spare], axis=1), LANES),
            _pad_cols(a[:, o + DECAY_LORA:o + DECAY_LORA + AAA_LORA], LANES),
            _pad_cols(a[:, o + DECAY_LORA + AAA_LORA:], 2 * LANES)], axis=1)

    w_rw = rw_layout(w_in[:, rcol:], ffw)
    rw_w = w_rw.shape[1]
    wp = jnp.concatenate([fq, fk, fv, w_rw], axis=1).astype(BF16)
    mu = rw_layout(shift_mu[None, :], jnp.zeros((1, nh), F32))
    fbias = _pad_cols(jnp.pad(fox_forget_bias[None, :], ((0, 0), (FORGET_LANE, 0))), LANES)

    tm1 = 512
    ii = jnp.arange(tm1)
    tri = (ii[:, None] >= ii[None, :]).astype(BF16)
    bias_lane = lambda hd, p: hd * LANES + (HEAD_DIM if hd % 2 == 0 else 0) + 3 + p
    sel = np.zeros((LANES, hw), np.float32)
    qconst = np.zeros((1, hw), np.float32)
    for hd in range(nh):
        for p in range(3):
            sel[FORGET_LANE + hd + 8 * p, bias_lane(hd, p)] = 1.0
            qconst[0, bias_lane(hd, p)] = 1.0
    sel = jnp.asarray(sel, BF16)
    qconst = jnp.asarray(qconst)

    q_aug, k_aug, v, rw = _inproj(h, attn_norm_pre[None, :], wp, fbias, tri, sel, qconst, mu,
                                  n_heads=nh, rw_w=rw_w, f_col=3 * rwid, tm=tm1)

    o_fox = _fox(q_aug, k_aug, v, tq=1024, n_h=4)

    tc = 128
    hh = jnp.arange(rwid) // HEAD_DIM
    ones_blk = (hh[:, None] == hh[None, :]).astype(BF16)
    jj = jnp.arange(tc)
    tri_blk = ((jj[:, None] >= jj[None, :]) & (jj[:, None] // CHUNK == jj[None, :] // CHUNK)).astype(BF16)
    row = lambda a: a.reshape(1, -1)
    o_rw = _rwkv(rw, row(rwkv_w0), row(rwkv_a0), row(rwkv_k_k), row(rwkv_k_a), row(rwkv_r_k),
                 row(rwkv_ln_w), row(rwkv_ln_b),
                 _pad_rows(rwkv_w_up, LANES).astype(BF16), _pad_rows(rwkv_a_up, LANES).astype(BF16),
                 _pad_rows(rwkv_g_up, 2 * LANES).astype(BF16), ones_blk, tri_blk,
                 width=rwid, tc=tc, n_seq=4)

    out = _outffn(h.reshape(B * T, D), o_fox.reshape(B * T, fw), o_rw.reshape(B * T, rwid),
                  w_out[0:fw].astype(BF16), w_out[fw:].astype(BF16),
                  row(attn_norm_post), row(ffn_norm_pre), row(ffn_norm_post),
                  ffn_w_gate.astype(BF16), ffn_w_up.astype(BF16), ffn_w_down.astype(BF16),
                  tm=512, n_sub=2)
    return out.reshape(B, T, D)


def kernel(x, attn_norm_pre, attn_norm_post, w_in, fox_forget_bias, shift_mu, rwkv_w0, rwkv_w_up,
           rwkv_a0, rwkv_a_up, rwkv_g_up, rwkv_k_k, rwkv_k_a, rwkv_r_k, rwkv_ln_w, rwkv_ln_b, w_out,
           ffn_norm_pre, ffn_norm_post, ffn_w_gate, ffn_w_up, ffn_w_down):
    h = x
    for l in range(attn_norm_pre.shape[0]):
        h = _block_layer(h, attn_norm_pre[l], attn_norm_post[l], w_in[l], fox_forget_bias[l],
                         shift_mu[l], rwkv_w0[l], rwkv_w_up[l], rwkv_a0[l], rwkv_a_up[l],
                         rwkv_g_up[l], rwkv_k_k[l], rwkv_k_a[l], rwkv_r_k[l], rwkv_ln_w[l],
                         rwkv_ln_b[l], w_out[l], ffn_norm_pre[l], ffn_norm_post[l], ffn_w_gate[l],
                         ffn_w_up[l], ffn_w_down[l])
    return h
```

```python
import functools

import jax
import jax.numpy as jnp
import numpy as np
from jax import lax
from jax.experimental import pallas as pl
from jax.experimental.pallas import tpu as pltpu

F32 = jnp.float32
BF16 = jnp.bfloat16

HEAD_DIM = 64
LANES = 128
NORM_EPS = 1e-6
LNX_EPS = 64e-5
LOG2E = 1.4426950408889634
DECAY_LORA = 64
AAA_LORA = 64
GATE_LORA = 160
CHUNK = 64
GROUP_HEADS = 4
GROUP_W = GROUP_HEADS * HEAD_DIM
FORGET_LANE = DECAY_LORA
VMEM_LIMIT = 56 * 1024 * 1024


def _dot(a, b):
    return jnp.dot(a, b, preferred_element_type=F32)


def _dot_nt(a, b):
    return lax.dot_general(a, b, (((1,), (1,)), ((), ())), preferred_element_type=F32)


def _dot_tn(a, b):
    return lax.dot_general(a, b, (((0,), (0,)), ((), ())), preferred_element_type=F32)


def _split2(x):
    hi = x.astype(BF16)
    lo = (x - hi.astype(F32)).astype(BF16)
    return hi, lo


def _split3(x):
    hi = x.astype(BF16)
    r = x - hi.astype(F32)
    mid = r.astype(BF16)
    lo = (r - mid.astype(F32)).astype(BF16)
    return hi, mid, lo


def _dot_exact_lhs01(ones01, x):
    hi, mid, lo = _split3(x)
    return _dot(ones01, hi) + (_dot(ones01, mid) + _dot(ones01, lo))


def _rms(x):
    return x * lax.rsqrt(jnp.mean(x * x, axis=-1, keepdims=True) + NORM_EPS)


def _softplus(x):
    return jnp.maximum(x, 0.0) + jnp.log1p(jnp.exp(-jnp.abs(x)))


def _sigmoid(x):
    return 0.5 * jnp.tanh(0.5 * x) + 0.5


def _const_spec(shape):
    nd = len(shape)
    return pl.BlockSpec(shape, lambda *_: (0,) * nd, pipeline_mode=pl.Buffered(1))


def _inproj_kernel(x_ref, g_ref, w_ref, fb_ref, tri_ref, sel_ref, qc_ref, mu_ref,
                   q_out, k_out, v_out, rw_out, carry_ref, rwlast_ref, *, n_heads, rw_w, f_col):
    t = pl.program_id(1)

    @pl.when(t == 0)
    def _():
        carry_ref[...] = jnp.zeros_like(carry_ref)
        rwlast_ref[...] = jnp.zeros_like(rwlast_ref)

    fw = n_heads * HEAD_DIM
    tm = x_ref.shape[1]
    u = (_rms(x_ref[0]) * g_ref[...]).astype(BF16)
    lane = lax.broadcasted_iota(jnp.int32, (tm, LANES), 1)
    lower = lane < HEAD_DIM

    rw = _dot(u, w_ref[:, 3 * fw:3 * fw + rw_w])
    row0 = lax.broadcasted_iota(jnp.int32, (tm, 1), 0) == 0
    rw_prev = jnp.where(row0, rwlast_ref[...], pltpu.roll(rw, 1, 0))
    rwlast_ref[...] = rw[tm - 1:tm, :]
    rw_out[0] = rw + mu_ref[...] * (rw_prev - rw)

    fo = f_col
    in_f = (lane >= FORGET_LANE) & (lane < FORGET_LANE + n_heads)

    def pack3(x):
        hi = x.astype(BF16).astype(F32)
        r1 = x - hi
        mid = r1.astype(BF16).astype(F32)
        lo = r1 - mid
        keep = lambda a: jnp.where(in_f, a, 0.0)
        return (keep(hi) + pltpu.roll(keep(mid), 8, 1) + pltpu.roll(keep(lo), 16, 1)).astype(BF16)

    def unpack3(x):
        return x + pltpu.roll(x, LANES - 8, 1) + pltpu.roll(x, LANES - 16, 1)

    logf = -_softplus(-(rw[:, fo:fo + LANES] + fb_ref[...]))
    c = unpack3(_dot(tri_ref[...], pack3(logf))) + carry_ref[...]
    carry_ref[...] = c[tm - 1:tm, :]

    qk = _dot(u, w_ref[:, 0:2 * fw])
    k_bias = _dot(pack3(c * (-LOG2E)), sel_ref[...])
    scale = HEAD_DIM ** -0.5 * LOG2E
    for hp in range(n_heads // 2):
        qb = qk[:, hp * LANES:(hp + 1) * LANES] * scale
        kb = qk[:, fw + hp * LANES:fw + (hp + 1) * LANES]
        for h, keep in ((2 * hp, lower), (2 * hp + 1, jnp.logical_not(lower))):
            q_out[0, h] = jnp.where(keep, qb, qc_ref[:, h * LANES:(h + 1) * LANES]).astype(BF16)
            k_out[0, h] = jnp.where(keep, kb, k_bias[:, h * LANES:(h + 1) * LANES]).astype(BF16)

    v = _dot(u, w_ref[:, 2 * fw:3 * fw])
    one_even = jnp.where(lane == HEAD_DIM, 1.0, 0.0)
    one_odd = jnp.where(lane == 0, 1.0, 0.0)
    for hp in range(n_heads // 2):
        blk = v[:, hp * LANES:(hp + 1) * LANES]
        v_out[0, 2 * hp] = jnp.where(lower, blk, one_even).astype(BF16)
        v_out[0, 2 * hp + 1] = jnp.where(lower, one_odd, blk).astype(BF16)


def _inproj(x, g_pre, wp, fbias, tri, sel, qconst, mu, *, n_heads, rw_w, f_col, tm):
    B, T, D = x.shape
    assert T % tm == 0 and n_heads % 2 == 0 and tm == tri.shape[0]
    kern = functools.partial(_inproj_kernel, n_heads=n_heads, rw_w=rw_w, f_col=f_col)
    return pl.pallas_call(
        kern,
        grid=(B, T // tm),
        in_specs=[
            pl.BlockSpec((1, tm, D), lambda b, t: (b, t, 0)),
            _const_spec(g_pre.shape),
            _const_spec(wp.shape),
            _const_spec(fbias.shape),
            _const_spec(tri.shape),
            _const_spec(sel.shape),
            _const_spec(qconst.shape),
            _const_spec(mu.shape),
        ],
        out_specs=[
            pl.BlockSpec((1, n_heads, tm, LANES), lambda b, t: (b, 0, t, 0)),
            pl.BlockSpec((1, n_heads, tm, LANES), lambda b, t: (b, 0, t, 0)),
            pl.BlockSpec((1, n_heads, tm, LANES), lambda b, t: (b, 0, t, 0)),
            pl.BlockSpec((1, tm, rw_w), lambda b, t: (b, t, 0)),
        ],
        out_shape=[
            jax.ShapeDtypeStruct((B, n_heads, T, LANES), BF16),
            jax.ShapeDtypeStruct((B, n_heads, T, LANES), BF16),
            jax.ShapeDtypeStruct((B, n_heads, T, LANES), BF16),
            jax.ShapeDtypeStruct((B, T, rw_w), F32),
        ],
        scratch_shapes=[pltpu.VMEM((1, LANES), F32),
                        pltpu.VMEM((1, rw_w), F32)],
        compiler_params=pltpu.CompilerParams(
            dimension_semantics=("arbitrary", "arbitrary"), vmem_limit_bytes=VMEM_LIMIT),
        name="inproj",
    )(x, g_pre, wp, fbias, tri, sel, qconst, mu)


def _fox_kernel(q_ref, k_ref, v_ref, o_ref, m_scr, acc_scr, *, tq):
    qi = pl.program_id(2)
    heads = range(q_ref.shape[1])

    def step(jobs):
        todo = [(n, h) for n in range(len(jobs)) for h in heads]
        qrows = [pl.ds(q0, nq) for q0, nq, _, _, _ in jobs]
        krows = [pl.ds(pl.multiple_of(k0, tq), tk) for _, _, k0, tk, _ in jobs]
        s = {(n, h): _dot_nt(q_ref[0, h, qrows[n], :], k_ref[0, h, krows[n], :]) for n, h in todo}
        for n, (_, nq, _, tk, masked) in enumerate(jobs):
            if masked:
                row = lax.broadcasted_iota(jnp.int32, (nq, nq), 0)
                col = lax.broadcasted_iota(jnp.int32, (nq, nq), 1)
                for h in heads:
                    edge = jnp.where(col <= row, s[n, h][:, tk - nq:], -jnp.inf)
                    s[n, h] = edge if tk == nq else jnp.concatenate([s[n, h][:, :tk - nq], edge], axis=1)
        m_prev = {(n, h): m_scr[h, qrows[n], :] for n, h in todo}
        m_next = {nh: jnp.maximum(m_prev[nh], jnp.max(s[nh], axis=1, keepdims=True)) for nh in todo}
        p = {(n, h): jnp.exp2(s[n, h] - jnp.concatenate([m_next[n, h]] * (jobs[n][3] // LANES), axis=1)
                              ).astype(BF16) for n, h in todo}
        for n, h in todo:
            alpha = jnp.exp2(m_prev[n, h] - m_next[n, h])
            acc_scr[h, qrows[n], :] = (alpha * acc_scr[h, qrows[n], :]
                                       + _dot(p[n, h], v_ref[0, h, krows[n], :]))
            m_scr[h, qrows[n], :] = m_next[n, h]

    for h in heads:
        m_scr[h] = jnp.full(m_scr.shape[1:], -jnp.inf, F32)
        acc_scr[h] = jnp.zeros(acc_scr.shape[1:], F32)

    def body(j, carry):
        step([(0, tq, j * tq, tq, False)])
        return carry

    lax.fori_loop(0, qi, body, 0)
    quarter = tq // 4
    step([(n * quarter, quarter, qi * tq, (n + 1) * quarter, True) for n in range(4)])

    lane = lax.broadcasted_iota(jnp.int32, acc_scr.shape[1:], 1)
    for hp in range(len(heads) // 2):
        acc0 = acc_scr[2 * hp]
        acc1 = acc_scr[2 * hp + 1]
        o = jnp.where(lane < HEAD_DIM, acc0 / acc0[:, HEAD_DIM:HEAD_DIM + 1], acc1 / acc1[:, 0:1])
        o_ref[0, :, hp * LANES:(hp + 1) * LANES] = o.astype(o_ref.dtype)


def _fox(q_aug, k_aug, v_aug, *, tq, n_h):
    B, H, T, _ = q_aug.shape
    assert T % tq == 0 and H % n_h == 0 and n_h % 2 == 0
    kern = functools.partial(_fox_kernel, tq=tq)
    return pl.pallas_call(
        kern,
        grid=(B, H // n_h, T // tq),
        in_specs=[
            pl.BlockSpec((1, n_h, tq, LANES), lambda b, hg, qi: (b, hg, qi, 0)),
            pl.BlockSpec((1, n_h, T, LANES), lambda b, hg, qi: (b, hg, 0, 0)),
            pl.BlockSpec((1, n_h, T, LANES), lambda b, hg, qi: (b, hg, 0, 0)),
        ],
        out_specs=pl.BlockSpec((1, tq, n_h * HEAD_DIM), lambda b, hg, qi: (b, qi, hg)),
        out_shape=jax.ShapeDtypeStruct((B, T, H * HEAD_DIM), BF16),
        scratch_shapes=[pltpu.VMEM((n_h, tq, LANES), F32)] * 2,
        compiler_params=pltpu.CompilerParams(
            dimension_semantics=("arbitrary", "arbitrary", "arbitrary"),
            vmem_limit_bytes=VMEM_LIMIT),
        name="fox",
    )(q_aug, k_aug, v_aug)


def _unit_lower_inverse(Ls, stack):
    shape = Ls[0].shape
    t = lax.broadcasted_iota(jnp.int32, shape, 0)
    j = lax.broadcasted_iota(jnp.int32, shape, 1) % shape[0]
    same = (t >> 1) == (j >> 1)
    eye = jnp.where(t == j, 1.0, 0.0)
    ps = [eye + jnp.where(same, L, 0.0) for L in Ls]
    for s in range(2, 7):
        merged = (t >> s) == (j >> s)
        sel = merged & jnp.logical_not(same)
        offs = [stack(jnp.where(sel, L, 0.0).astype(BF16)) for L in Ls]
        pbs = [p.astype(BF16) for p in ps]
        tmps = [_dot(pb, off).astype(BF16) for pb, off in zip(pbs, offs)]
        ps = [p + _dot(tmp, stack(pb)) for p, tmp, pb in zip(ps, tmps, pbs)]
        same = merged
    return ps


def _rwkv_kernel(rw_ref, w0_ref, a0_ref, kk_ref, ka_ref, rk_ref,
                 lnw_ref, lnb_ref, wup_ref, aup_ref, gup_ref, ones_ref, tri_ref,
                 o_ref, s_scr, y_scr, *, width):
    t = pl.program_id(1)
    n_seq, tc, rw_w = rw_ref.shape
    n_rows = n_seq * tc
    n_chunks = tc // CHUNK
    n_groups = width // GROUP_W

    @pl.when(t == 0)
    def _():
        s_scr[...] = jnp.zeros_like(s_scr)

    ones_blk = ones_ref[...]

    def segsum(x):
        return _dot(x.astype(BF16), ones_blk)

    p = rw_ref[...].reshape(n_rows, rw_w)
    r = p[:, 0:width]
    k = p[:, width:2 * width]
    v = p[:, 2 * width:3 * width]
    o = 3 * width
    w_lat = p[:, o:o + LANES]
    a_lat = p[:, o + LANES:o + 2 * LANES]
    g_lat = p[:, o + 2 * LANES:o + 4 * LANES]

    z = w0_ref[...] + _dot(jnp.tanh(w_lat).astype(BF16), wup_ref[...])
    g_log = _sigmoid(z) * (-(2.718281828459045 ** -0.5))
    lr = _sigmoid(a0_ref[...] + _dot(a_lat.astype(BF16), aup_ref[...]))
    gate = _dot(_sigmoid(g_lat).astype(BF16), gup_ref[...])
    kk = k * kk_ref[...]
    kk = kk * lax.rsqrt(jnp.maximum(segsum(kk * kk), 1e-24))
    k = k * (1.0 + (lr - 1.0) * ka_ref[...])
    bonus = segsum(r * k * rk_ref[...]) * v

    tri = tri_ref[...]
    G = jnp.concatenate([_dot_exact_lhs01(tri, g_log[s * tc:(s + 1) * tc]) for s in range(n_seq)], axis=0)
    e_pos = jnp.exp(G)
    e_neg = jnp.exp(-G)
    a_t = (-kk * jnp.exp(G - g_log)).astype(BF16)
    b_t = (kk * lr * e_neg).astype(BF16)
    r_t = (r * e_pos).astype(BF16)
    k_t = (k * e_neg).astype(BF16)
    v_t = v.astype(BF16)

    lane_head = lax.broadcasted_iota(jnp.int32, (CHUNK, GROUP_W), 1) // HEAD_DIM
    head_masks = [lane_head == h for h in range(GROUP_HEADS)]
    rt = lax.broadcasted_iota(jnp.int32, (GROUP_W, GROUP_W), 0)
    ct = lax.broadcasted_iota(jnp.int32, (GROUP_W, GROUP_W), 1)
    same_head = (rt // HEAD_DIM) == (ct // HEAD_DIM)
    diag = rt == ct
    tw = lax.broadcasted_iota(jnp.int32, (CHUNK, GROUP_W), 0)
    jw = lax.broadcasted_iota(jnp.int32, (CHUNK, GROUP_W), 1) % CHUNK
    strict_w = tw > jw
    incl_w = tw >= jw
    rows = lambda s, ci: slice((s * n_chunks + ci) * CHUNK, (s * n_chunks + ci + 1) * CHUNK)
    cols = lambda g: slice(g * GROUP_W, (g + 1) * GROUP_W)
    wide = lambda x, u: x[rows(u[0], u[1]), cols(u[2])]

    def stack(x):
        zero = jnp.zeros_like(x)
        return jnp.concatenate([jnp.where(mk, x, zero) for mk in head_masks], axis=0)

    units = [(s, ci, g) for ci in range(n_chunks) for s in range(n_seq) for g in range(n_groups)]
    b_s = {u: stack(wide(b_t, u)) for u in units}
    k_s = {u: stack(wide(k_t, u)) for u in units}
    v_s = {u: stack(wide(v_t, u)) for u in units}
    l_all = {u: _dot_nt(jnp.concatenate([wide(a_t, u), wide(r_t, u)], axis=0),
                        jnp.concatenate([b_s[u], k_s[u]], axis=0)) for u in units}
    l_ab = {u: jnp.where(strict_w, l_all[u][0:CHUNK, 0:GROUP_W], 0.0) for u in units}
    l_ak = {u: jnp.where(strict_w, l_all[u][0:CHUNK, GROUP_W:], 0.0).astype(BF16) for u in units}
    l_rb = {u: jnp.where(incl_w, l_all[u][CHUNK:, 0:GROUP_W], 0.0).astype(BF16) for u in units}
    l_rk = {u: jnp.where(incl_w, l_all[u][CHUNK:, GROUP_W:], 0.0).astype(BF16) for u in units}
    av = {u: _dot(jnp.concatenate([l_ak[u], l_rk[u]], axis=0), v_s[u]) for u in units}
    akv = {u: av[u][0:CHUNK] for u in units}
    y_rk = {u: av[u][CHUNK:] for u in units}
    t_inv = {u: t.astype(BF16) for u, t in
             zip(units, _unit_lower_inverse([l_ab[u] for u in units], stack))}
    last = lambda u: (u[0] * n_chunks + u[1] + 1) * CHUNK - 1
    gc = {u: e_pos[last(u):last(u) + 1, cols(u[2])] for u in units}
    gcol = {u: jnp.sum(jnp.where(diag, jnp.broadcast_to(gc[u], (GROUP_W, GROUP_W)), 0.0),
                       axis=1, keepdims=True) for u in units}
    bk_h = {u: jnp.concatenate([(wide(b_t, u).astype(F32) * gc[u]).astype(BF16),
                                (wide(k_t, u).astype(F32) * gc[u]).astype(BF16)], axis=0) for u in units}

    chains = [(s, g) for s in range(n_seq) for g in range(n_groups)]
    state = {c: s_scr[c[0] * n_groups + c[1]] for c in chains}
    for ci in range(n_chunks):
        level = [(s, ci, g) for s, g in chains]
        zr = {u: _dot(jnp.concatenate([wide(a_t, u), wide(r_t, u)], axis=0),
                      state[u[0], u[2]].astype(BF16)) for u in level}
        z_b = {u: _dot(t_inv[u], stack((zr[u][0:CHUNK] + akv[u]).astype(BF16))).astype(BF16)
               for u in level}
        upd = {u: _dot_tn(bk_h[u], jnp.concatenate([z_b[u], wide(v_t, u)], axis=0)) for u in level}
        for u in level:
            state[u[0], u[2]] = state[u[0], u[2]] * gcol[u] + jnp.where(same_head, upd[u], 0.0)
        for u in level:
            y_scr[rows(u[0], ci), cols(u[2])] = (zr[u][CHUNK:] + _dot(l_rb[u], stack(z_b[u]))) + y_rk[u]
    for c in chains:
        s_scr[c[0] * n_groups + c[1]] = state[c]

    y = y_scr[...]
    d = y - segsum(y) * (1.0 / HEAD_DIM)
    var = segsum(d * d) * (1.0 / HEAD_DIM)
    yn = d * lax.rsqrt(var + LNX_EPS) * lnw_ref[...] + lnb_ref[...]
    o_ref[...] = ((yn + bonus) * gate).astype(o_ref.dtype).reshape(n_seq, tc, width)


def _rwkv(rw, w0, a0, k_k, k_a, r_k, ln_w, ln_b, w_up, a_up, g_up, ones_blk, tri_blk, *,
          width, tc, n_seq):
    B, T, rw_w = rw.shape
    assert B % n_seq == 0 and T % tc == 0 and tc % CHUNK == 0 and width % GROUP_W == 0
    n_groups = width // GROUP_W
    kern = functools.partial(_rwkv_kernel, width=width)
    small = [w0, a0, k_k, k_a, r_k, ln_w, ln_b, w_up, a_up, g_up, ones_blk, tri_blk]
    return pl.pallas_call(
        kern,
        grid=(B // n_seq, T // tc),
        in_specs=[
            pl.BlockSpec((n_seq, tc, rw_w), lambda b, t: (b, t, 0)),
        ] + [_const_spec(a.shape) for a in small],
        out_specs=pl.BlockSpec((n_seq, tc, width), lambda b, t: (b, t, 0)),
        out_shape=jax.ShapeDtypeStruct((B, T, width), BF16),
        scratch_shapes=[
            pltpu.VMEM((n_seq * n_groups, GROUP_W, GROUP_W), F32),
            pltpu.VMEM((n_seq * tc, width), F32),
        ],
        compiler_params=pltpu.CompilerParams(
            dimension_semantics=("arbitrary", "arbitrary"), vmem_limit_bytes=VMEM_LIMIT),
        name="rwkv",
    )(rw, *small)


def _outffn_kernel(x_ref, of_ref, or_ref, wof_ref, wor_ref, gpost_ref, gfpre_ref, gfpost_ref,
                   wg_ref, wu_ref, wd_ref, o_ref, *, n_sub):
    tm = x_ref.shape[0]
    sub = tm // n_sub
    parts = [slice(i * sub, (i + 1) * sub) for i in range(n_sub)]
    mix = [_dot(of_ref[sl, :], wof_ref[...]) + _dot(or_ref[sl, :], wor_ref[...]) for sl in parts]
    h = [x_ref[sl, :] + _rms(m) * gpost_ref[...] for sl, m in zip(parts, mix)]
    z = [(_rms(hh) * gfpre_ref[...]).astype(BF16) for hh in h]
    gate = [_dot(zz, wg_ref[...]) for zz in z]
    up = [_dot(zz, wu_ref[...]) for zz in z]
    act = [(g * _sigmoid(g) * u).astype(BF16) for g, u in zip(gate, up)]
    f = [_dot(a, wd_ref[...]) for a in act]
    for sl, hh, ff in zip(parts, h, f):
        o_ref[sl, :] = hh + _rms(ff) * gfpost_ref[...]


def _outffn(x2, o_fox, o_rw, wof, wor, g_post, g_fpre, g_fpost, wg, wu, wd, *, tm, n_sub):
    N, D = x2.shape
    assert N % tm == 0 and tm % n_sub == 0
    consts = [wof, wor, g_post, g_fpre, g_fpost, wg, wu, wd]
    return pl.pallas_call(
        functools.partial(_outffn_kernel, n_sub=n_sub),
        grid=(N // tm,),
        in_specs=[
            pl.BlockSpec((tm, D), lambda i: (i, 0)),
            pl.BlockSpec((tm, o_fox.shape[1]), lambda i: (i, 0)),
            pl.BlockSpec((tm, o_rw.shape[1]), lambda i: (i, 0)),
        ] + [_const_spec(a.shape) for a in consts],
        out_specs=pl.BlockSpec((tm, D), lambda i: (i, 0)),
        out_shape=jax.ShapeDtypeStruct((N, D), F32),
        compiler_params=pltpu.CompilerParams(
            dimension_semantics=("arbitrary",), vmem_limit_bytes=VMEM_LIMIT),
        name="outffn",
    )(x2, o_fox, o_rw, *consts)


def _pad_cols(a, n):
    return jnp.pad(a, ((0, 0), (0, n - a.shape[1])))


def _pad_rows(a, n):
    return jnp.pad(a, ((0, n - a.shape[0]), (0, 0)))


def _block_layer(h, attn_norm_pre, attn_norm_post, w_in, fox_forget_bias, shift_mu, rwkv_w0,
                 rwkv_w_up, rwkv_a0, rwkv_a_up, rwkv_g_up, rwkv_k_k, rwkv_k_a, rwkv_r_k,
                 rwkv_ln_w, rwkv_ln_b, w_out, ffn_norm_pre, ffn_norm_post, ffn_w_gate, ffn_w_up,
                 ffn_w_down):
    B, T, D = h.shape
    fw = D // 2
    rwid = D // 2
    nh = fw // HEAD_DIM
    hw = nh * LANES

    fq, fk, fv = w_in[:, 0:fw], w_in[:, fw:2 * fw], w_in[:, 2 * fw:3 * fw]
    ffw = w_in[:, 3 * fw:3 * fw + nh]
    rcol = 3 * fw + nh

    def rw_layout(a, spare):
        o = 3 * rwid
        return jnp.concatenate([
            a[:, 0:o],
            _pad_cols(jnp.concatenate([a[:, o:o + DECAY_LORA], spare], axis=1), LANES),
            _pad_cols(a[:, o + DECAY_LORA:o + DECAY_LORA + AAA_LORA], LANES),
            _pad_cols(a[:, o + DECAY_LORA + AAA_LORA:], 2 * LANES)], axis=1)

    w_rw = rw_layout(w_in[:, rcol:], ffw)
    rw_w = w_rw.shape[1]
    wp = jnp.concatenate([fq, fk, fv, w_rw], axis=1).astype(BF16)
    mu = rw_layout(shift_mu[None, :], jnp.zeros((1, nh), F32))
    fbias = _pad_cols(jnp.pad(fox_forget_bias[None, :], ((0, 0), (FORGET_LANE, 0))), LANES)

    tm1 = 512
    ii = jnp.arange(tm1)
    tri = (ii[:, None] >= ii[None, :]).astype(BF16)
    bias_lane = lambda hd, p: hd * LANES + (HEAD_DIM if hd % 2 == 0 else 0) + 3 + p
    sel = np.zeros((LANES, hw), np.float32)
    qconst = np.zeros((1, hw), np.float32)
    for hd in range(nh):
        for p in range(3):
            sel[FORGET_LANE + hd + 8 * p, bias_lane(hd, p)] = 1.0
            qconst[0, bias_lane(hd, p)] = 1.0
    sel = jnp.asarray(sel, BF16)
    qconst = jnp.asarray(qconst)

    q_aug, k_aug, v, rw = _inproj(h, attn_norm_pre[None, :], wp, fbias, tri, sel, qconst, mu,
                                  n_heads=nh, rw_w=rw_w, f_col=3 * rwid, tm=tm1)

    o_fox = _fox(q_aug, k_aug, v, tq=1024, n_h=4)

    tc = 256
    hh = jnp.arange(rwid) // HEAD_DIM
    ones_blk = (hh[:, None] == hh[None, :]).astype(BF16)
    jj = jnp.arange(tc)
    tri_blk = ((jj[:, None] >= jj[None, :]) & (jj[:, None] // CHUNK == jj[None, :] // CHUNK)).astype(BF16)
    row = lambda a: a.reshape(1, -1)
    o_rw = _rwkv(rw, row(rwkv_w0), row(rwkv_a0), row(rwkv_k_k), row(rwkv_k_a), row(rwkv_r_k),
                 row(rwkv_ln_w), row(rwkv_ln_b),
                 _pad_rows(rwkv_w_up, LANES).astype(BF16), _pad_rows(rwkv_a_up, LANES).astype(BF16),
                 _pad_rows(rwkv_g_up, 2 * LANES).astype(BF16), ones_blk, tri_blk,
                 width=rwid, tc=tc, n_seq=4)

    out = _outffn(h.reshape(B * T, D), o_fox.reshape(B * T, fw), o_rw.reshape(B * T, rwid),
                  w_out[0:fw].astype(BF16), w_out[fw:].astype(BF16),
                  row(attn_norm_post), row(ffn_norm_pre), row(ffn_norm_post),
                  ffn_w_gate.astype(BF16), ffn_w_up.astype(BF16), ffn_w_down.astype(BF16),
                  tm=512, n_sub=2)
    return out.reshape(B, T, D)


def kernel(x, attn_norm_pre, attn_norm_post, w_in, fox_forget_bias, shift_mu, rwkv_w0, rwkv_w_up,
           rwkv_a0, rwkv_a_up, rwkv_g_up, rwkv_k_k, rwkv_k_a, rwkv_r_k, rwkv_ln_w, rwkv_ln_b, w_out,
           ffn_norm_pre, ffn_norm_post, ffn_w_gate, ffn_w_up, ffn_w_down):
    h = x
    for l in range(attn_norm_pre.shape[0]):
        h = _block_layer(h, attn_norm_pre[l], attn_norm_post[l], w_in[l], fox_forget_bias[l],
                         shift_mu[l], rwkv_w0[l], rwkv_w_up[l], rwkv_a0[l], rwkv_a_up[l],
                         rwkv_g_up[l], rwkv_k_k[l], rwkv_k_a[l], rwkv_r_k[l], rwkv_ln_w[l],
                         rwkv_ln_b[l], w_out[l], ffn_norm_pre[l], ffn_norm_post[l], ffn_w_gate[l],
                         ffn_w_up[l], ffn_w_down[l])
    return h
```
